```python
import jax
import jax.numpy as jnp
from jax import lax
import numpy as np

D_MODEL = 1024
BATCH = 4
SEQ = 8192
DEPTH = 2
DEC_BATCH = 32
DEC_SEQ = 4
PAST_LEN = 16384
PAGE_SIZE = 128

HEAD_DIM = 64
W_A = D_MODEL // 2
H_B = (D_MODEL - W_A) // HEAD_DIM
H_IDX = 8
D_IDX = 64
TOPK_MAX = 256
H_C = (D_MODEL // 2) // HEAD_DIM
W_D = D_MODEL - H_C * HEAD_DIM
LRU_BLOCKS = 8
LRU_BW = W_D // LRU_BLOCKS
CONV_A = 3
CONV_D = 4
RG_C = 8.0
D_FF = 2816
Q_BLOCK = 128
ROPE_THETA = 10000.0
EPS = 1e-6
FORGET_BIAS = 4.0
N_EVEN = (DEPTH + 1) // 2
N_ODD = DEPTH // 2
EVEN_SPLITS = (W_A, W_A, W_A, H_B * HEAD_DIM, H_B * HEAD_DIM, H_B * HEAD_DIM, H_IDX * D_IDX, D_IDX, H_IDX)
ODD_SPLITS = (H_C * HEAD_DIM, H_C * HEAD_DIM, H_C * HEAD_DIM, H_C, W_D, W_D)

kernel_name = 'hybrid_conv_dsa_fox_rglru_step'


def rmsnorm(x, g):
    xf = x.astype(jnp.float32)
    xf = xf * lax.rsqrt(jnp.mean(xf * xf, axis=-1, keepdims=True) + EPS)
    return (xf * g.astype(jnp.float32)).astype(x.dtype)


def swiglu(h, w_gate, w_up, w_down):
    return (jax.nn.silu(h @ w_gate) * (h @ w_up)) @ w_down


def split_cols(z, sizes):
    cuts = [int(c) for c in np.cumsum(sizes)[:-1]]
    return jnp.split(z, cuts, axis=-1)


def rope(x, pos):
    half = x.shape[-1] // 2
    inv = ROPE_THETA ** (-jnp.arange(half, dtype=jnp.float32) / half)
    ang = pos.astype(jnp.float32)[:, None] * inv[None, :]
    cos = jnp.cos(ang)[:, None, :].astype(x.dtype)
    sin = jnp.sin(ang)[:, None, :].astype(x.dtype)
    x1, x2 = x[..., :half], x[..., half:]
    return jnp.concatenate([x1 * cos - x2 * sin, x2 * cos + x1 * sin], axis=-1)


def causal_dwconv(u, buf, w):
    width, t = w.shape[0], u.shape[1]
    up = jnp.concatenate([buf.astype(u.dtype), u], axis=1)
    y = w[0] * up[:, :t]
    for j in range(1, width):
        y = y + w[j] * up[:, j:j + t]
    return y, up[:, up.shape[1] - (width - 1):]


def to_blocks(a):
    n, s = a.shape[:2]
    return jnp.swapaxes(a.reshape((n, s // Q_BLOCK, Q_BLOCK) + a.shape[2:]), 0, 1)


def from_blocks(a):
    a = jnp.swapaxes(a, 0, 1)
    return a.reshape((a.shape[0], a.shape[1] * a.shape[2]) + a.shape[3:])


def indexer_topk(qi, w_idx, ki, q_pos, k_pos, n_sel):
    dots = jnp.einsum('nqhd,nkd->nqhk', qi, ki).astype(jnp.float32) * (D_IDX ** -0.5)
    score = jnp.einsum('nqhk,nqh->nqk', jax.nn.relu(dots), w_idx.astype(jnp.float32) * (H_IDX ** -0.5))
    visible = k_pos[None, None, :] <= q_pos[None, :, None]
    score = jnp.where(visible, score, -jnp.inf)
    _, idx = lax.top_k(score, n_sel)
    return idx, idx <= q_pos[None, :, None]


def attend_selected(q, k_sel, v_sel, valid):
    logits = jnp.einsum('nqhd,nqkhd->nqhk', q, k_sel).astype(jnp.float32) * (HEAD_DIM ** -0.5)
    logits = jnp.where(valid[:, :, None, :], logits, -jnp.inf)
    p = jax.nn.softmax(logits, axis=-1).astype(v_sel.dtype)
    return jnp.einsum('nqhk,nqkhd->nqhd', p, v_sel)


def dsa_prompt(q, k, v, qi, ki, w_idx, pos):
    n, s = q.shape[:2]
    n_sel = min(TOPK_MAX, s // 4)
    nidx = jnp.arange(n)[:, None, None]

    def block(args):
        q_b, qi_b, w_b, pos_b = args
        idx, valid = indexer_topk(qi_b, w_b, ki, pos_b, pos, n_sel)
        return attend_selected(q_b, k[nidx, idx], v[nidx, idx], valid)

    out = lax.map(block, (to_blocks(q), to_blocks(qi), to_blocks(w_idx), pos.reshape(-1, Q_BLOCK)))
    return from_blocks(out)


def dsa_sample(q, k_new, v_new, qi, ki_new, w_idx, pos, k_pool, v_pool, ki_pool, page_table, li):
    n, t = q.shape[:2]
    page = k_pool.shape[2]
    past = page_table.shape[1] * page
    n_sel = min(TOPK_MAX, (past + t) // 4)
    ki_past = ki_pool[li, page_table].reshape(n, past, D_IDX).astype(ki_new.dtype)
    ki_all = jnp.concatenate([ki_past, ki_new], axis=1)
    idx, valid = indexer_topk(qi, w_idx, ki_all, pos, jnp.arange(past + t), n_sel)
    nidx = jnp.arange(n)[:, None, None]
    s_past = jnp.minimum(idx, past - 1)
    phys = page_table[nidx, s_past // page]
    off = s_past % page
    s_new = jnp.clip(idx - past, 0, t - 1)
    is_new = (idx >= past)[..., None, None]
    k_sel = jnp.where(is_new, k_new[nidx, s_new], k_pool[li, phys, off].astype(k_new.dtype))
    v_sel = jnp.where(is_new, v_new[nidx, s_new], v_pool[li, phys, off].astype(v_new.dtype))
    return attend_selected(q, k_sel, v_sel, valid)


def fox_prompt(q, k, v, logf, pos):
    cum_t = jnp.cumsum(logf, axis=1)
    cum = jnp.swapaxes(cum_t, 1, 2)

    def block(args):
        q_b, c_b, pos_b = args
        logits = jnp.einsum('nqhd,nkhd->nhqk', q_b, k).astype(jnp.float32) * (HEAD_DIM ** -0.5)
        logits = logits + jnp.swapaxes(c_b, 1, 2)[..., None] - cum[:, :, None, :]
        logits = jnp.where((pos[None, :] <= pos_b[:, None])[None, None], logits, -jnp.inf)
        p = jax.nn.softmax(logits, axis=-1).astype(v.dtype)
        return jnp.einsum('nhqk,nkhd->nqhd', p, v)

    out = lax.map(block, (to_blocks(q), to_blocks(cum_t), pos.reshape(-1, Q_BLOCK)))
    return from_blocks(out)


def fox_sample(q, k_new, v_new, logf_new, pos, k_pool, v_pool, logf_pool, page_table, li):
    n, t = q.shape[:2]
    past = page_table.shape[1] * k_pool.shape[2]
    k_past = k_pool[li, page_table].reshape(n, past, H_C, HEAD_DIM).astype(q.dtype)
    v_past = v_pool[li, page_table].reshape(n, past, H_C, HEAD_DIM).astype(v_new.dtype)
    logf_past = logf_pool[li, page_table].reshape(n, past, H_C).astype(jnp.float32)
    cum = jnp.swapaxes(jnp.cumsum(jnp.concatenate([logf_past, logf_new], axis=1), axis=1), 1, 2)
    logits = jnp.concatenate([jnp.einsum('nqhd,nkhd->nhqk', q, k_past),
                              jnp.einsum('nqhd,nkhd->nhqk', q, k_new)], axis=-1).astype(jnp.float32)
    logits = logits * (HEAD_DIM ** -0.5) + cum[:, :, past:, None] - cum[:, :, None, :]
    k_pos = jnp.arange(past + t)
    logits = jnp.where((k_pos[None, :] <= pos[:, None])[None, None], logits, -jnp.inf)
    p = jax.nn.softmax(logits, axis=-1).astype(v_new.dtype)
    return (jnp.einsum('nhqk,nkhd->nqhd', p[..., :past], v_past)
            + jnp.einsum('nhqk,nkhd->nqhd', p[..., past:], v_new))


def rg_lru(xc, h0, pos, w_a, b_a, w_x, b_x, lam):
    n, t, w = xc.shape
    xb = xc.reshape(n, t, LRU_BLOCKS, LRU_BW)
    r = jax.nn.sigmoid((jnp.einsum('ntgi,gij->ntgj', xb, w_a).reshape(n, t, w) + b_a).astype(jnp.float32))
    i = jax.nn.sigmoid((jnp.einsum('ntgi,gij->ntgj', xb, w_x).reshape(n, t, w) + b_x).astype(jnp.float32))
    log_a = -RG_C * r * jax.nn.softplus(-lam.astype(jnp.float32))
    a = jnp.exp(log_a)
    mult = jnp.where((pos == 0)[None, :, None], 1.0, jnp.sqrt(-jnp.expm1(2.0 * log_a)))
    b = mult * i * xc.astype(jnp.float32)
    b = b.at[:, 0].add(a[:, 0] * h0.astype(jnp.float32))

    def combine(c1, c2):
        a1, b1 = c1
        a2, b2 = c2
        return a1 * a2, a2 * b1 + b2

    _, hs = lax.associative_scan(combine, (a, b), axis=1)
    return hs.astype(xc.dtype), hs[:, -1]


def even_mixer(h, pos, w_in, conv_w, w_out, cache):
    n, t, _ = h.shape
    xin, gb, gc, q, k, v, qi, ki, w_idx = split_cols(h @ w_in, EVEN_SPLITS)
    buf = jnp.zeros((n, CONV_A - 1, W_A), h.dtype) if cache is None else cache[3]
    yconv, conv_state = causal_dwconv(gc * xin, buf, conv_w)
    ya = gb * yconv
    q = rope(q.reshape(n, t, H_B, HEAD_DIM), pos)
    k = rope(k.reshape(n, t, H_B, HEAD_DIM), pos)
    v = v.reshape(n, t, H_B, HEAD_DIM)
    qi = rope(qi.reshape(n, t, H_IDX, D_IDX), pos)
    ki = rope(ki[:, :, None, :], pos)[:, :, 0]
    if cache is None:
        yb = dsa_prompt(q, k, v, qi, ki, w_idx, pos)
    else:
        k_pool, v_pool, ki_pool, _, page_table, li = cache
        yb = dsa_sample(q, k, v, qi, ki, w_idx, pos, k_pool, v_pool, ki_pool, page_table, li)
    y = jnp.concatenate([ya, yb.reshape(n, t, H_B * HEAD_DIM)], axis=-1) @ w_out
    return y, (k, v, ki, conv_state)


def odd_mixer(h, pos, w_in, b_f, conv_w, conv_b, w_a, b_a, w_x, b_x, lam, w_out, cache):
    n, t, _ = h.shape
    q, k, v, f_lin, xd, gd = split_cols(h @ w_in, ODD_SPLITS)
    q = q.reshape(n, t, H_C, HEAD_DIM)
    k = k.reshape(n, t, H_C, HEAD_DIM)
    v = v.reshape(n, t, H_C, HEAD_DIM)
    logf = jax.nn.log_sigmoid((f_lin + b_f).astype(jnp.float32))
    if cache is None:
        yc = fox_prompt(q, k, v, logf, pos)
        buf = jnp.zeros((n, CONV_D - 1, W_D), h.dtype)
        h0 = jnp.zeros((n, W_D), jnp.float32)
    else:
        k_pool, v_pool, logf_pool, buf, h0, page_table, li = cache
        yc = fox_sample(q, k, v, logf, pos, k_pool, v_pool, logf_pool, page_table, li)
    xc, conv_state = causal_dwconv(xd, buf, conv_w)
    hs, h_last = rg_lru(xc + conv_b, h0, pos, w_a, b_a, w_x, b_x, lam)
    yd = hs * jax.nn.gelu(gd)
    y = jnp.concatenate([yc.reshape(n, t, H_C * HEAD_DIM), yd], axis=-1) @ w_out
    return y, (k, v, logf, conv_state, h_last)


def trunk(x, pos, p, cache):
    states = tuple([] for _ in range(9))
    for l in range(DEPTH):
        x = x + 0.5 * swiglu(rmsnorm(x, p['g_ff1'][l]), p['w_ff1_gate'][l], p['w_ff1_up'][l], p['w_ff1_down'][l])
        h = rmsnorm(x, p['g_mix'][l])
        if l % 2 == 0:
            e = l // 2
            c = None if cache is None else (cache['k_b'], cache['v_b'], cache['kidx_b'], cache['conv_a'][e], cache['page_table'], e)
            y, new = even_mixer(h, pos, p['w_in_e'][e], p['conv_a_w'][e], p['w_out_e'][e], c)
            for lst, arr in zip(states[:4], new):
                lst.append(arr)
        else:
            o = l // 2
            c = None if cache is None else (cache['k_c'], cache['v_c'], cache['logf_c'], cache['conv_d'][o], cache['lru_h'][o], cache['page_table'], o)
            y, new = odd_mixer(h, pos, p['w_in_o'][o], p['b_f'][o], p['conv_d_w'][o], p['conv_d_b'][o],
                               p['w_rg_a'][o], p['b_rg_a'][o], p['w_rg_x'][o], p['b_rg_x'][o], p['lam'][o], p['w_out_o'][o], c)
            for lst, arr in zip(states[4:], new):
                lst.append(arr)
        x = x + y
        x = x + 0.5 * swiglu(rmsnorm(x, p['g_ff2'][l]), p['w_ff2_gate'][l], p['w_ff2_up'][l], p['w_ff2_down'][l])
    return rmsnorm(x, p['g_final']), tuple(jnp.stack(s) for s in states)


def setup_inputs(seed: int = 0) -> dict:
    key = jax.random.key(seed)
    keys = iter(jax.random.split(key, 64))

    def nrm(shape, scale):
        return jax.random.normal(next(keys), shape, jnp.float32) * scale

    n_pages = PAST_LEN // PAGE_SIZE
    n_pool = (5 * DEC_BATCH * n_pages + 3) // 4
    page_table = jax.random.permutation(next(keys), n_pool)[:DEC_BATCH * n_pages].reshape(DEC_BATCH, n_pages).astype(jnp.int32)
    in_e = sum(EVEN_SPLITS)
    in_o = sum(ODD_SPLITS)
    w_mix_out = W_A + H_B * HEAD_DIM
    w_mix_out_o = H_C * HEAD_DIM + W_D
    u = jax.random.uniform(next(keys), (N_ODD, W_D), jnp.float32, minval=0.9, maxval=0.999)
    a0 = u ** (1.0 / RG_C)
    lam = jnp.log(a0) - jnp.log1p(-a0)
    return dict(
        x_prompt=nrm((BATCH, SEQ, D_MODEL), 1.0),
        x_sample=nrm((DEC_BATCH, DEC_SEQ, D_MODEL), 1.0),
        cache_k_b=nrm((N_EVEN, n_pool, PAGE_SIZE, H_B, HEAD_DIM), 1.0),
        cache_v_b=nrm((N_EVEN, n_pool, PAGE_SIZE, H_B, HEAD_DIM), 1.0),
        cache_kidx_b=nrm((N_EVEN, n_pool, PAGE_SIZE, D_IDX), 1.0),
        state_conv_a=nrm((N_EVEN, DEC_BATCH, CONV_A - 1, W_A), 1.0),
        cache_k_c=nrm((N_ODD, n_pool, PAGE_SIZE, H_C, HEAD_DIM), 1.0),
        cache_v_c=nrm((N_ODD, n_pool, PAGE_SIZE, H_C, HEAD_DIM), 1.0),
        cache_logf_c=jax.nn.log_sigmoid(FORGET_BIAS + nrm((N_ODD, n_pool, PAGE_SIZE, H_C), 0.5)),
        state_conv_d=nrm((N_ODD, DEC_BATCH, CONV_D - 1, W_D), 1.0),
        state_lru_h=nrm((N_ODD, DEC_BATCH, W_D), 1.0),
        page_table=page_table,
        g_ff1=1.0 + nrm((DEPTH, D_MODEL), 0.01),
        w_ff1_gate=nrm((DEPTH, D_MODEL, D_FF), D_MODEL ** -0.5),
        w_ff1_up=nrm((DEPTH, D_MODEL, D_FF), D_MODEL ** -0.5),
        w_ff1_down=nrm((DEPTH, D_FF, D_MODEL), D_FF ** -0.5),
        g_mix=1.0 + nrm((DEPTH, D_MODEL), 0.01),
        g_ff2=1.0 + nrm((DEPTH, D_MODEL), 0.01),
        w_ff2_gate=nrm((DEPTH, D_MODEL, D_FF), D_MODEL ** -0.5),
        w_ff2_up=nrm((DEPTH, D_MODEL, D_FF), D_MODEL ** -0.5),
        w_ff2_down=nrm((DEPTH, D_FF, D_MODEL), D_FF ** -0.5),
        w_in_e=nrm((N_EVEN, D_MODEL, in_e), D_MODEL ** -0.5),
        conv_a_w=nrm((N_EVEN, CONV_A, W_A), CONV_A ** -0.5),
        w_out_e=nrm((N_EVEN, w_mix_out, D_MODEL), w_mix_out ** -0.5),
        w_in_o=nrm((N_ODD, D_MODEL, in_o), D_MODEL ** -0.5),
        b_f=FORGET_BIAS + nrm((N_ODD, H_C), 0.5),
        conv_d_w=nrm((N_ODD, CONV_D, W_D), CONV_D ** -0.5),
        conv_d_b=nrm((N_ODD, W_D), 0.01),
        w_rg_a=nrm((N_ODD, LRU_BLOCKS, LRU_BW, LRU_BW), LRU_BW ** -0.5),
        b_rg_a=nrm((N_ODD, W_D), 0.01),
        w_rg_x=nrm((N_ODD, LRU_BLOCKS, LRU_BW, LRU_BW), LRU_BW ** -0.5),
        b_rg_x=nrm((N_ODD, W_D), 0.01),
        lam=lam,
        w_out_o=nrm((N_ODD, w_mix_out_o, D_MODEL), w_mix_out_o ** -0.5),
        g_final=1.0 + nrm((D_MODEL,), 0.01),
    )


def reference(x_prompt, x_sample, cache_k_b, cache_v_b, cache_kidx_b, state_conv_a, cache_k_c, cache_v_c,
              cache_logf_c, state_conv_d, state_lru_h, page_table, g_ff1, w_ff1_gate, w_ff1_up, w_ff1_down,
              g_mix, g_ff2, w_ff2_gate, w_ff2_up, w_ff2_down, w_in_e, conv_a_w, w_out_e, w_in_o, b_f,
              conv_d_w, conv_d_b, w_rg_a, b_rg_a, w_rg_x, b_rg_x, lam, w_out_o, g_final):
    p = dict(g_ff1=g_ff1, w_ff1_gate=w_ff1_gate, w_ff1_up=w_ff1_up, w_ff1_down=w_ff1_down,
             g_mix=g_mix, g_ff2=g_ff2, w_ff2_gate=w_ff2_gate, w_ff2_up=w_ff2_up, w_ff2_down=w_ff2_down,
             w_in_e=w_in_e, conv_a_w=conv_a_w, w_out_e=w_out_e, w_in_o=w_in_o, b_f=b_f,
             conv_d_w=conv_d_w, conv_d_b=conv_d_b, w_rg_a=w_rg_a, b_rg_a=b_rg_a, w_rg_x=w_rg_x,
             b_rg_x=b_rg_x, lam=lam, w_out_o=w_out_o, g_final=g_final)
    seq = x_prompt.shape[1]
    dec_seq = x_sample.shape[1]
    past = page_table.shape[1] * cache_k_b.shape[2]
    y_prompt, st_p = trunk(x_prompt, jnp.arange(seq), p, None)
    cache = dict(k_b=cache_k_b, v_b=cache_v_b, kidx_b=cache_kidx_b, conv_a=state_conv_a,
                 k_c=cache_k_c, v_c=cache_v_c, logf_c=cache_logf_c, conv_d=state_conv_d,
                 lru_h=state_lru_h, page_table=page_table)
    y_sample, st_s = trunk(x_sample, past + jnp.arange(dec_seq), p, cache)
    k_b_p, v_b_p, kidx_b_p, conv_a_p, k_c_p, v_c_p, logf_c_p, conv_d_p, lru_h_p = st_p
    k_b_s, v_b_s, kidx_b_s, conv_a_s, k_c_s, v_c_s, logf_c_s, conv_d_s, lru_h_s = st_s
    return (y_prompt, y_sample,
            k_b_p, v_b_p, kidx_b_p, conv_a_p, k_c_p, v_c_p, logf_c_p, conv_d_p, lru_h_p,
            k_b_s, v_b_s, kidx_b_s, conv_a_s, k_c_s, v_c_s, logf_c_s, conv_d_s, lru_h_s)
```

```python
import functools

import jax
import jax.numpy as jnp
import numpy as np
from jax import lax
from jax.experimental import pallas as pl
from jax.experimental.pallas import tpu as pltpu

F32 = jnp.float32
BF16 = jnp.bfloat16

HEAD_DIM = 64
D_IDX = 64
H_IDX = 8
TOPK_MAX = 256
LRU_BLOCKS = 8
RG_C = 8.0
ROPE_THETA = 10000.0
EPS = 1e-6

LANES = 128
VMEM_LIMIT = 56 * 1024 * 1024
NEG_INF = float("-inf")
M_INIT = -1e30
BISECT_ITERS = 40

_NT = (((1,), (1,)), ((), ()))


def _params():
    return pltpu.CompilerParams(vmem_limit_bytes=VMEM_LIMIT)


def _const_spec(shape):
    nd = len(shape)
    return pl.BlockSpec(shape, lambda *_: (0,) * nd, pipeline_mode=pl.Buffered(1))


def _dot(a, b):
    return jnp.dot(a, b, preferred_element_type=F32)


def _dot_nt(a, b, precision=None):
    return lax.dot_general(a, b, _NT, preferred_element_type=F32, precision=precision)


def _rmsnorm(x, g):
    ms = jnp.mean(x * x, axis=-1, keepdims=True)
    return x * lax.rsqrt(ms + EPS) * g


def _ffn_body(*refs, n_chunks, fc, has_mix, has_final):
    it = iter(refs)
    x_ref = next(it)
    if has_mix:
        ya_ref, yb_ref, woa_ref, wob_ref = next(it), next(it), next(it), next(it)
    g_ref, wg_ref, wu_ref, wd_ref = next(it), next(it), next(it), next(it)
    if has_final:
        gf_ref = next(it)
    out_ref = next(it)

    x = x_ref[...]
    if has_mix:
        x = x + _dot(ya_ref[...], woa_ref[...]) + _dot(yb_ref[...], wob_ref[...])
    h = _rmsnorm(x, g_ref[...]).astype(BF16)
    acc = None
    for c in range(n_chunks):
        gate = _dot(h, wg_ref[:, c * fc:(c + 1) * fc])
        up = _dot(h, wu_ref[:, c * fc:(c + 1) * fc])
        a = (gate * jax.nn.sigmoid(gate) * up).astype(BF16)
        part = _dot(a, wd_ref[c * fc:(c + 1) * fc, :])
        acc = part if acc is None else acc + part
    x = x + 0.5 * acc
    if has_final:
        x = _rmsnorm(x, gf_ref[...])
    out_ref[...] = x


def _ffn(x, g, wg, wu, wd, tm, mix=None, g_final=None):
    m, d = x.shape
    f = wg.shape[1]
    n_chunks = 2 if (f // 2) % LANES == 0 else 1
    fc = f // n_chunks
    row = lambda i: (i, 0)
    args, specs = [x], [pl.BlockSpec((tm, d), row)]
    if mix is not None:
        ya, yb, woa, wob = mix
        args += [ya, yb, woa, wob]
        specs += [pl.BlockSpec((tm, ya.shape[1]), row), pl.BlockSpec((tm, yb.shape[1]), row),
                  _const_spec(woa.shape), _const_spec(wob.shape)]
    args += [g, wg, wu, wd]
    specs += [_const_spec(g.shape), _const_spec(wg.shape), _const_spec(wu.shape), _const_spec(wd.shape)]
    if g_final is not None:
        args.append(g_final)
        specs.append(_const_spec(g_final.shape))
    body = functools.partial(_ffn_body, n_chunks=n_chunks, fc=fc, has_mix=mix is not None,
                             has_final=g_final is not None)
    return pl.pallas_call(
        body, grid=(m // tm,), in_specs=specs, out_specs=pl.BlockSpec((tm, d), row),
        out_shape=jax.ShapeDtypeStruct((m, d), F32), compiler_params=_params(), name="ffn")(*args)


def _rope_tile(xt, cos, sin_signed):
    lane = lax.broadcasted_iota(jnp.int32, xt.shape, 1)
    first_half = (lane % HEAD_DIM) < (HEAD_DIM // 2)
    partner = jnp.where(first_half, pltpu.roll(xt, LANES - HEAD_DIM // 2, 1), pltpu.roll(xt, HEAD_DIM // 2, 1))
    return xt * cos + partner * sin_signed


def _hist_rows(width, stride):
    need = (width - 1) * stride
    pad = -(-need // 8) * 8
    return need, pad


def _causal_conv(u, w_ref, ubuf, hist_ref, state_ref, first_tile, width, stride):
    tm = u.shape[0]
    need, pad = _hist_rows(width, stride)

    @pl.when(first_tile)
    def _():
        ubuf[pad - need:pad, :] = hist_ref[0]

    @pl.when(jnp.logical_not(first_tile))
    def _():
        ubuf[0:pad, :] = ubuf[tm:tm + pad, :]

    ubuf[pad:pad + tm, :] = u
    y = None
    for j in range(width):
        start = pad - (width - 1 - j) * stride
        term = w_ref[j:j + 1, :] * ubuf[start:start + tm, :]
        y = term if y is None else y + term
    state_ref[0] = ubuf[pad + tm - need:pad + tm, :]
    return y


def _proj_even_body(x_ref, g_ref, wc_ref, wq_ref, wk_ref, wv_ref, wqi_ref, wkw_ref, cos_ref, sin_ref,
                    cw_ref, hist_ref,
                    ya_ref, q_ref, k32_ref, kb_ref, v32_ref, vb_ref, qi_ref, kw_ref, ki2_ref, state_ref,
                    ubuf, *, tiles_per_seq, stride, wa):
    first_tile = (pl.program_id(0) % tiles_per_seq) == 0
    h = _rmsnorm(x_ref[...], g_ref[...]).astype(BF16)
    cos = cos_ref[...]
    sin = sin_ref[...]
    scale = HEAD_DIM ** -0.5

    zc = _dot(h, wc_ref[...])
    xin, gb, gc = zc[:, :wa], zc[:, wa:2 * wa], zc[:, 2 * wa:]
    yconv = _causal_conv(gc * xin, cw_ref, ubuf, hist_ref, state_ref, first_tile, cw_ref.shape[0], stride)
    ya_ref[...] = (gb * yconv).astype(BF16)

    n_tiles = wq_ref.shape[1] // LANES
    q = _dot(h, wq_ref[...])
    k = _dot(h, wk_ref[...])
    qi = _dot(h, wqi_ref[...])
    for t in range(n_tiles):
        sl = slice(t * LANES, (t + 1) * LANES)
        q_ref[:, sl] = (_rope_tile(q[:, sl], cos, sin) * scale).astype(BF16)
        kt = _rope_tile(k[:, sl], cos, sin)
        k32_ref[:, sl] = kt
        kb_ref[:, sl] = kt.astype(BF16)
        qi_ref[:, sl] = (_rope_tile(qi[:, sl], cos, sin) * (D_IDX ** -0.5)).astype(BF16)
    v = _dot(h, wv_ref[...])
    v32_ref[...] = v
    vb_ref[...] = v.astype(BF16)

    kw = _dot(h, wkw_ref[...])
    lane = lax.broadcasted_iota(jnp.int32, kw.shape, 1)
    is_ki = lane < D_IDX
    kw = jnp.where(is_ki, _rope_tile(kw, cos, sin), kw)
    kw_ref[...] = kw
    ki2_ref[...] = jnp.where(is_ki, kw, pltpu.roll(kw, D_IDX, 1)).astype(BF16)


def _proj_even(x, g, w, cos, sin, conv_w, hist, tm, tiles_per_seq, stride):
    m, d = x.shape
    wc, wq, wk, wv, wqi, wkw = w
    wa = wc.shape[1] // 3
    hb = wq.shape[1]
    width = conv_w.shape[0]
    need, pad = _hist_rows(width, stride)
    nb = hist.shape[0]
    row = lambda i: (i, 0)
    seq = lambda i: (i // tiles_per_seq, 0, 0)
    in_specs = [pl.BlockSpec((tm, d), row), _const_spec(g.shape)]
    in_specs += [_const_spec(a.shape) for a in (wc, wq, wk, wv, wqi, wkw)]
    in_specs += [pl.BlockSpec((tm, LANES), row), pl.BlockSpec((tm, LANES), row), _const_spec(conv_w.shape),
                 pl.BlockSpec((1, need, wa), seq)]
    out_shape = [jax.ShapeDtypeStruct((m, wa), BF16), jax.ShapeDtypeStruct((m, hb), BF16),
                 jax.ShapeDtypeStruct((m, hb), F32), jax.ShapeDtypeStruct((m, hb), BF16),
                 jax.ShapeDtypeStruct((m, hb), F32), jax.ShapeDtypeStruct((m, hb), BF16),
                 jax.ShapeDtypeStruct((m, hb), BF16), jax.ShapeDtypeStruct((m, LANES), F32),
                 jax.ShapeDtypeStruct((m, LANES), BF16), jax.ShapeDtypeStruct((nb, need, wa), F32)]
    out_specs = [pl.BlockSpec((tm, wa), row)] + [pl.BlockSpec((tm, hb), row)] * 6
    out_specs += [pl.BlockSpec((tm, LANES), row), pl.BlockSpec((tm, LANES), row), pl.BlockSpec((1, need, wa), seq)]
    body = functools.partial(_proj_even_body, tiles_per_seq=tiles_per_seq, stride=stride, wa=wa)
    return pl.pallas_call(
        body, grid=(m // tm,), in_specs=in_specs, out_specs=out_specs, out_shape=out_shape,
        scratch_shapes=[pltpu.VMEM((pad + tm, wa), F32)], compiler_params=_params(), name="proj_even")(
            x, g, wc, wq, wk, wv, wqi, wkw, cos, sin, conv_w, hist)


def _shift_rows(x, d, fill):
    row = lax.broadcasted_iota(jnp.int32, x.shape, 0)
    return jnp.where(row >= d, pltpu.roll(x, d, 0), fill)


def _proj_odd_body(x_ref, g_ref, wqkv_ref, wf_ref, wxd_ref, wgd_ref, bf_ref, cw_ref, cb_ref, wra_ref, bra_ref,
                   wrx_ref, brx_ref, lam_ref, hist_ref, h0_ref,
                   q_ref, k32_ref, kb_ref, v32_ref, vb_ref, logf_ref, yd_ref, state_ref, hlast_ref,
                   ubuf, hcar, *, tiles_per_seq, stride, pos_base, hc):
    tile_in_seq = pl.program_id(0) % tiles_per_seq
    first_tile = tile_in_seq == 0
    tm = x_ref.shape[0]
    h = _rmsnorm(x_ref[...], g_ref[...]).astype(BF16)

    z = _dot(h, wqkv_ref[...])
    q_ref[...] = (z[:, :hc] * (HEAD_DIM ** -0.5)).astype(BF16)
    k = z[:, hc:2 * hc]
    v = z[:, 2 * hc:]
    k32_ref[...] = k
    kb_ref[...] = k.astype(BF16)
    v32_ref[...] = v
    vb_ref[...] = v.astype(BF16)

    zf = _dot(h, wf_ref[...]) + bf_ref[...]
    logf = jnp.minimum(zf, 0.0) - jnp.log1p(jnp.exp(-jnp.abs(zf)))
    logf_ref[...] = logf[:, :logf_ref.shape[1]]

    xd = _dot(h, wxd_ref[...])
    gd = _dot(h, wgd_ref[...])
    xc = _causal_conv(xd, cw_ref, ubuf, hist_ref, state_ref, first_tile, cw_ref.shape[0], stride) + cb_ref[...]
    xcb = xc.astype(BF16)
    r = jax.nn.sigmoid(_dot(xcb, wra_ref[...]) + bra_ref[...])
    ig = jax.nn.sigmoid(_dot(xcb, wrx_ref[...]) + brx_ref[...])
    lam = lam_ref[...]
    softplus_neg_lam = jnp.maximum(-lam, 0.0) + jnp.log1p(jnp.exp(-jnp.abs(lam)))
    log_a = -RG_C * r * softplus_neg_lam
    a = jnp.exp(log_a)
    row = lax.broadcasted_iota(jnp.int32, (tm, 1), 0)
    pos = pos_base + tile_in_seq * (tm // stride) + row // stride
    mult = jnp.where(pos == 0, 1.0, jnp.sqrt(1.0 - jnp.exp(2.0 * log_a)))
    b = mult * ig * xc

    d = stride
    while d < tm:
        a_prev = _shift_rows(a, d, 1.0)
        b_prev = _shift_rows(b, d, 0.0)
        b = a * b_prev + b
        a = a * a_prev
        d *= 2

    @pl.when(first_tile)
    def _():
        hcar[...] = h0_ref[0]

    carry = hcar[...]
    if stride == 1:
        carry_rows = carry
    else:
        carry_rows = jnp.concatenate([carry] * (tm // stride), axis=0)
    hs = a * carry_rows + b
    hcar[...] = hs[tm - stride:, :]
    hlast_ref[0] = hs[tm - stride:, :]
    yd_ref[...] = (hs * jax.nn.gelu(gd)).astype(BF16)


def _proj_odd(x, g, w, small, hist, h0, tm, tiles_per_seq, stride, pos_base):
    m, d = x.shape
    wqkv, wf, wxd, wgd = w
    bf, conv_w, conv_b, wra, bra, wrx, brx, lam = small
    hc = wqkv.shape[1] // 3
    wd_ = wxd.shape[1]
    n_heads = hc // HEAD_DIM
    width = conv_w.shape[0]
    need, pad = _hist_rows(width, stride)
    nb = hist.shape[0]
    row = lambda i: (i, 0)
    seq = lambda i: (i // tiles_per_seq, 0, 0)
    ins = [x, g, wqkv, wf, wxd, wgd, bf, conv_w, conv_b, wra, bra, wrx, brx, lam, hist, h0]
    in_specs = [pl.BlockSpec((tm, d), row)] + [_const_spec(a.shape) for a in ins[1:14]]
    in_specs += [pl.BlockSpec((1, need, wd_), seq), pl.BlockSpec((1, stride, wd_), seq)]
    out_shape = [jax.ShapeDtypeStruct((m, hc), BF16), jax.ShapeDtypeStruct((m, hc), F32),
                 jax.ShapeDtypeStruct((m, hc), BF16), jax.ShapeDtypeStruct((m, hc), F32),
                 jax.ShapeDtypeStruct((m, hc), BF16), jax.ShapeDtypeStruct((m, n_heads), F32),
                 jax.ShapeDtypeStruct((m, wd_), BF16), jax.ShapeDtypeStruct((nb, need, wd_), F32),
                 jax.ShapeDtypeStruct((nb, stride, wd_), F32)]
    out_specs = [pl.BlockSpec((tm, hc), row)] * 5 + [pl.BlockSpec((tm, n_heads), row), pl.BlockSpec((tm, wd_), row),
                                                      pl.BlockSpec((1, need, wd_), seq),
                                                      pl.BlockSpec((1, stride, wd_), seq)]
    body = functools.partial(_proj_odd_body, tiles_per_seq=tiles_per_seq, stride=stride, pos_base=pos_base, hc=hc)
    return pl.pallas_call(
        body, grid=(m // tm,), in_specs=in_specs, out_specs=out_specs, out_shape=out_shape,
        scratch_shapes=[pltpu.VMEM((pad + tm, wd_), F32), pltpu.VMEM((stride, wd_), F32)],
        compiler_params=_params(), name="proj_odd")(*ins)


def _cumsum_lanes(lf, carry):
    t, nh = lf.shape
    eye = (lax.broadcasted_iota(jnp.int32, (nh, nh), 0) == lax.broadcasted_iota(jnp.int32, (nh, nh), 1)).astype(F32)
    lf_t = _dot_nt(eye, lf, precision=lax.Precision.HIGHEST)
    tri = (lax.broadcasted_iota(jnp.int32, (t, t), 1) <= lax.broadcasted_iota(jnp.int32, (t, t), 0)).astype(F32)
    return _dot_nt(lf_t, tri, precision=lax.Precision.HIGHEST) + carry


def _cumsum_body(lf_ref, out_ref, carry):
    @pl.when(pl.program_id(1) == 0)
    def _():
        carry[...] = jnp.zeros_like(carry)

    cum = _cumsum_lanes(lf_ref[...], carry[:, 0:1])
    out_ref[0] = cum
    carry[...] = jnp.broadcast_to(cum[:, cum.shape[1] - 1:], carry.shape)


def _cumsum_heads(logf, n, s, tc):
    nh = logf.shape[1]
    tiles = s // tc
    return pl.pallas_call(
        _cumsum_body, grid=(n, tiles),
        in_specs=[pl.BlockSpec((tc, nh), lambda b, i: (b * tiles + i, 0))],
        out_specs=pl.BlockSpec((1, nh, tc), lambda b, i: (b, 0, i)),
        out_shape=jax.ShapeDtypeStruct((n, nh, s), F32),
        scratch_shapes=[pltpu.VMEM((nh, LANES), F32)], compiler_params=_params(), name="cumsum")(logf)


def _pair_mask(shape, parity):
    lane = lax.broadcasted_iota(jnp.int32, shape, 1)
    return (lane >= HEAD_DIM) == (parity == 1)


def _softmax_step(s, vc, carry):
    m, l, acc = carry
    m_new = jnp.maximum(m, jnp.max(s, axis=1, keepdims=True))
    alpha = jnp.exp(m - m_new)
    p = jnp.exp(s - m_new)
    l = alpha * l + jnp.sum(p, axis=1, keepdims=True)
    acc = alpha * acc + _dot(p.astype(BF16), vc)
    return m_new, l, acc


def _softmax_init(rows):
    return (jnp.full((rows, 1), M_INIT, F32), jnp.zeros((rows, 1), F32), jnp.zeros((rows, LANES), F32))


def _fox_body(q_ref, k_ref, v_ref, cum_ref, out_ref, *, tq, kc):
    i = pl.program_id(1)
    n_pairs = q_ref.shape[1] // LANES
    n_full = (i * tq) // kc
    n_tot = ((i + 1) * tq + kc - 1) // kc
    qpos = i * tq + lax.broadcasted_iota(jnp.int32, (tq, kc), 0)
    kiota = lax.broadcasted_iota(jnp.int32, (tq, kc), 1)

    def pair(j, _):
        col = pl.ds(pl.multiple_of(j * LANES, LANES), LANES)
        q_pair = q_ref[:, col]
        outs = []
        for parity in (0, 1):
            qh = jnp.where(_pair_mask(q_pair.shape, parity), q_pair, jnp.zeros_like(q_pair))
            hrow = pl.ds(2 * j + parity, 1)

            def chunk(c, carry, masked):
                rows = pl.ds(pl.multiple_of(c * kc, kc), kc)
                s = _dot_nt(qh, k_ref[rows, col]) - cum_ref[0, hrow, rows]
                if masked:
                    s = jnp.where(c * kc + kiota <= qpos, s, NEG_INF)
                return _softmax_step(s, v_ref[rows, col], carry)

            carry = lax.fori_loop(0, n_full, functools.partial(chunk, masked=False), _softmax_init(tq))
            m, l, acc = lax.fori_loop(n_full, n_tot, functools.partial(chunk, masked=True), carry)
            outs.append(acc / l)
        out_ref[:, col] = jnp.where(_pair_mask(outs[0].shape, 1), outs[1], outs[0]).astype(BF16)
        return 0

    lax.fori_loop(0, n_pairs, pair, 0)


def _fox_prompt(q, kb, vb, cum_t, n, s, tq, kc):
    hc = q.shape[1]
    nh = cum_t.shape[1]
    nq = s // tq
    body = functools.partial(_fox_body, tq=tq, kc=kc)
    one = pl.Buffered(1)
    return pl.pallas_call(
        body, grid=(n, nq),
        in_specs=[pl.BlockSpec((tq, hc), lambda b, i: (b * nq + i, 0)),
                  pl.BlockSpec((s, hc), lambda b, i: (b, 0), pipeline_mode=one),
                  pl.BlockSpec((s, hc), lambda b, i: (b, 0), pipeline_mode=one),
                  pl.BlockSpec((1, nh, s), lambda b, i: (b, 0, 0), pipeline_mode=one)],
        out_specs=pl.BlockSpec((tq, hc), lambda b, i: (b * nq + i, 0)),
        out_shape=jax.ShapeDtypeStruct((n * s, hc), BF16), compiler_params=_params(), name="fox_prompt")(
            q, kb, vb, cum_t)


def _topk_to_bias(sc_ref, nkc, kc, n_sel):
    rows = sc_ref.shape[0]
    tiles = kc // LANES
    pos_inf = float("inf")

    def chunk_tiles(c):
        base = pl.multiple_of(c * kc, kc)
        return [sc_ref[:, pl.ds(base + t * LANES, LANES)] for t in range(tiles)]

    def lane_fold(fn, init):
        def body(c, acc):
            for x in chunk_tiles(c):
                acc = fn(acc, x)
            return acc
        return lax.fori_loop(0, nkc, body, init)

    def count(pred_fn):
        part = lane_fold(lambda acc, x: acc + jnp.where(pred_fn(x), 1.0, 0.0), jnp.zeros((rows, LANES), F32))
        return jnp.sum(part, axis=1, keepdims=True)

    mx = jnp.max(lane_fold(jnp.maximum, jnp.full((rows, LANES), NEG_INF, F32)), axis=1, keepdims=True)
    mn = jnp.min(lane_fold(lambda acc, x: jnp.minimum(acc, jnp.where(x == NEG_INF, pos_inf, x)),
                           jnp.full((rows, LANES), pos_inf, F32)), axis=1, keepdims=True)

    def bisect(_, carry):
        lo, hi = carry
        mid = lo + 0.5 * (hi - lo)
        midb = jnp.broadcast_to(mid, (rows, LANES))
        enough = count(lambda x: x >= midb) >= n_sel
        return jnp.where(enough, mid, lo), jnp.where(enough, hi, mid)

    lo, _ = lax.fori_loop(0, BISECT_ITERS, bisect, (mn, mx))
    lob = jnp.broadcast_to(lo, (rows, LANES))
    thr = jnp.min(lane_fold(lambda acc, x: jnp.minimum(acc, jnp.where(x >= lob, x, pos_inf)),
                            jnp.full((rows, LANES), pos_inf, F32)), axis=1, keepdims=True)
    thrb = jnp.broadcast_to(thr, (rows, LANES))
    n_gt = count(lambda x: x > thrb)
    n_ge = count(lambda x: x >= thrb)
    has_ties = jnp.max(n_ge) > n_sel

    @pl.when(jnp.logical_not(has_ties))
    def _():
        def body(c, _):
            base = pl.multiple_of(c * kc, kc)
            for t in range(tiles):
                col = pl.ds(base + t * LANES, LANES)
                sc_ref[:, col] = jnp.where(sc_ref[:, col] >= thrb, 0.0, NEG_INF)
            return 0
        lax.fori_loop(0, nkc, body, 0)

    @pl.when(has_ties)
    def _():
        need = n_sel - n_gt
        upper = (lax.broadcasted_iota(jnp.int32, (kc, kc), 0) <= lax.broadcasted_iota(jnp.int32, (kc, kc), 1))
        upper = jnp.where(upper, 1.0, 0.0).astype(BF16)

        def body(c, seen):
            col = pl.ds(pl.multiple_of(c * kc, kc), kc)
            x = sc_ref[:, col]
            eq = jnp.where(x == thr, 1.0, 0.0)
            rank = _dot(eq.astype(BF16), upper) + seen
            keep_eq = jnp.where(rank <= need, eq, 0.0)
            sc_ref[:, col] = jnp.where(x > thr, 0.0, jnp.where(keep_eq > 0.5, 0.0, NEG_INF))
            return seen + jnp.sum(eq, axis=1, keepdims=True)
        lax.fori_loop(0, nkc, body, jnp.zeros((rows, 1), F32))


def _dsa_body(qi_ref, kw_ref, ki2_ref, q_ref, k_ref, v_ref, out_ref, sc_ref, wb_ref, *, tq, kc, n_sel):
    i = pl.program_id(1)
    n_pairs = q_ref.shape[1] // LANES
    tiles = kc // LANES
    nkc = ((i + 1) * tq + kc - 1) // kc

    kw = kw_ref[...]
    for h in range(H_IDX):
        wb_ref[h] = jnp.broadcast_to(kw[:, D_IDX + h:D_IDX + h + 1] * (H_IDX ** -0.5), (tq, LANES))
    qi_heads = []
    for h in range(H_IDX):
        j, parity = divmod(h, 2)
        qi_pair = qi_ref[:, j * LANES:(j + 1) * LANES]
        qi_heads.append(jnp.where(_pair_mask(qi_pair.shape, parity), qi_pair, jnp.zeros_like(qi_pair)))
    qpos = i * tq + lax.broadcasted_iota(jnp.int32, (tq, LANES), 0)
    kiota = lax.broadcasted_iota(jnp.int32, (tq, LANES), 1)

    def score_chunk(c, _):
        base = pl.multiple_of(c * kc, kc)
        kic = ki2_ref[pl.ds(base, kc), :]
        acc = [None] * tiles
        for h in range(H_IDX):
            d = _dot_nt(qi_heads[h], kic)
            wbh = wb_ref[h]
            for t in range(tiles):
                term = jnp.maximum(d[:, t * LANES:(t + 1) * LANES], 0.0) * wbh
                acc[t] = term if acc[t] is None else acc[t] + term
        for t in range(tiles):
            kpos = base + t * LANES + kiota
            sc_ref[:, pl.ds(base + t * LANES, LANES)] = jnp.where(kpos <= qpos, acc[t], NEG_INF)
        return 0

    lax.fori_loop(0, nkc, score_chunk, 0)
    _topk_to_bias(sc_ref, nkc, kc, n_sel)

    def pair(j, _):
        col = pl.ds(pl.multiple_of(j * LANES, LANES), LANES)
        q_pair = q_ref[:, col]
        outs = []
        for parity in (0, 1):
            qh = jnp.where(_pair_mask(q_pair.shape, parity), q_pair, jnp.zeros_like(q_pair))

            def chunk(c, carry):
                rows = pl.ds(pl.multiple_of(c * kc, kc), kc)
                s = _dot_nt(qh, k_ref[rows, col]) + sc_ref[:, rows]
                return _softmax_step(s, v_ref[rows, col], carry)

            m, l, acc = lax.fori_loop(0, nkc, chunk, _softmax_init(tq))
            outs.append(acc / l)
        out_ref[:, col] = jnp.where(_pair_mask(outs[0].shape, 1), outs[1], outs[0]).astype(BF16)
        return 0

    lax.fori_loop(0, n_pairs, pair, 0)


def _dsa_prompt(qi, kw, ki2, q, kb, vb, n, s, tq, kc):
    hb = q.shape[1]
    nq = s // tq
    n_sel = min(TOPK_MAX, s // 4)
    body = functools.partial(_dsa_body, tq=tq, kc=kc, n_sel=n_sel)
    one = pl.Buffered(1)
    qrow = lambda b, i: (b * nq + i, 0)
    return pl.pallas_call(
        body, grid=(n, nq),
        in_specs=[pl.BlockSpec((tq, hb), qrow), pl.BlockSpec((tq, LANES), qrow),
                  pl.BlockSpec((s, LANES), lambda b, i: (b, 0), pipeline_mode=one),
                  pl.BlockSpec((tq, hb), qrow),
                  pl.BlockSpec((s, hb), lambda b, i: (b, 0), pipeline_mode=one),
                  pl.BlockSpec((s, hb), lambda b, i: (b, 0), pipeline_mode=one)],
        out_specs=pl.BlockSpec((tq, hb), qrow),
        out_shape=jax.ShapeDtypeStruct((n * s, hb), BF16),
        scratch_shapes=[pltpu.VMEM((tq, s), F32), pltpu.VMEM((H_IDX, tq, LANES), F32)],
        compiler_params=_params(), name="dsa_prompt")(qi, kw, ki2, q, kb, vb)


def _page_specs(pool, li, pages_per_step, n_pages):
    def spec(r):
        def index(n, p, pt):
            return (li, pt[n, jnp.minimum(p * pages_per_step + r, n_pages - 1)], 0, 0)
        return pl.BlockSpec((1, 1) + pool.shape[2:], index)
    return [spec(r) for r in range(pages_per_step)]


def _idx_scores_body(pt_ref, qi_ref, w_ref, *rest, pages_per_step, n_main, page, t_new):
    page_refs = rest[:pages_per_step]
    new_ref, out_ref = rest[pages_per_step], rest[pages_per_step + 1]
    p = pl.program_id(1)
    qi = qi_ref[0]
    w = w_ref[0] * (H_IDX ** -0.5)

    def scores(ki):
        d = _dot_nt(qi, ki.astype(BF16))
        contrib = jnp.maximum(d, 0.0) * w
        return jnp.sum(contrib.reshape(t_new, H_IDX, page), axis=1)

    @pl.when(p < n_main)
    def _():
        for r in range(pages_per_step):
            off = pl.multiple_of((p * pages_per_step + r) * page, page)
            out_ref[0, :, pl.ds(off, page)] = scores(page_refs[r][0, 0])

    @pl.when(p == n_main)
    def _():
        sc = scores(new_ref[0])
        t = lax.broadcasted_iota(jnp.int32, sc.shape, 0)
        j = lax.broadcasted_iota(jnp.int32, sc.shape, 1)
        out_ref[0, :, n_main * pages_per_step * page:] = jnp.where(j <= t, sc, NEG_INF)


def _idx_scores_sample(page_table, qi_rows, w_rows, ki_pool, li, ki_new_pages, pages_per_step):
    nseq, n_pages = page_table.shape
    page = ki_pool.shape[2]
    t_new = qi_rows.shape[1] // H_IDX
    n_main = n_pages // pages_per_step
    length = (n_pages + 1) * page
    body = functools.partial(_idx_scores_body, pages_per_step=pages_per_step, n_main=n_main, page=page, t_new=t_new)
    per_seq = lambda n, p, pt: (n, 0, 0)
    grid_spec = pltpu.PrefetchScalarGridSpec(
        num_scalar_prefetch=1, grid=(nseq, n_main + 1),
        in_specs=[pl.BlockSpec((1,) + qi_rows.shape[1:], per_seq), pl.BlockSpec((1,) + w_rows.shape[1:], per_seq)]
        + _page_specs(ki_pool, li, pages_per_step, n_pages)
        + [pl.BlockSpec((1,) + ki_new_pages.shape[1:], per_seq)],
        out_specs=pl.BlockSpec((1, t_new, length), per_seq))
    return pl.pallas_call(
        body, grid_spec=grid_spec, out_shape=jax.ShapeDtypeStruct((nseq, t_new, length), F32),
        compiler_params=_params(), name="idx_scores_sample")(
            page_table, qi_rows, w_rows, *([ki_pool] * pages_per_step), ki_new_pages)


def _topk_rows_body(sc_ref, out_ref, *, kc, n_sel):
    out_ref[...] = sc_ref[...]
    _topk_to_bias(out_ref, out_ref.shape[1] // kc, kc, n_sel)


def _topk_rows(scores, n_sel, rows, kc):
    r, length = scores.shape
    body = functools.partial(_topk_rows_body, kc=kc, n_sel=n_sel)
    return pl.pallas_call(
        body, grid=(r // rows,), in_specs=[pl.BlockSpec((rows, length), lambda i: (i, 0))],
        out_specs=pl.BlockSpec((rows, length), lambda i: (i, 0)),
        out_shape=jax.ShapeDtypeStruct((r, length), F32), compiler_params=_params(), name="topk_rows")(scores)


def _paged_attn_body(pt_ref, q_ref, *rest, pages_per_step, n_main, page, t_new, n_heads, decay):
    it = iter(rest)
    k_refs = [next(it) for _ in range(pages_per_step)]
    v_refs = [next(it) for _ in range(pages_per_step)]
    if decay:
        f_refs = [next(it) for _ in range(pages_per_step)]
    knew_ref, vnew_ref = next(it), next(it)
    if decay:
        fnew_ref = next(it)
    else:
        bias_ref = next(it)
    out_ref = next(it)
    m_ref, l_ref, acc_ref = next(it), next(it), next(it)
    if decay:
        cum_ref = next(it)
    p = pl.program_id(1)
    rows = t_new * n_heads
    q = q_ref[0]

    @pl.when(p == 0)
    def _():
        m_ref[...] = jnp.full(m_ref.shape, M_INIT, F32)
        l_ref[...] = jnp.zeros(l_ref.shape, F32)
        acc_ref[...] = jnp.zeros(acc_ref.shape, F32)
        if decay:
            cum_ref[...] = jnp.zeros(cum_ref.shape, F32)

    def step(kp, vp, bias):
        s = _dot_nt(q, kp.astype(BF16)) + bias
        m = m_ref[...]
        m_new = jnp.maximum(m, jnp.max(s, axis=1, keepdims=True))
        alpha = jnp.exp(m - m_new)
        pr = jnp.exp(s - m_new)
        l_ref[...] = alpha * l_ref[...] + jnp.sum(pr, axis=1, keepdims=True)
        acc_ref[...] = alpha * acc_ref[...] + _dot(pr.astype(BF16), vp.astype(BF16))
        m_ref[...] = m_new

    def decay_bias(f_page):
        cum = _cumsum_lanes(f_page, cum_ref[:, 0:1])
        cum_ref[...] = jnp.broadcast_to(cum[:, page - 1:], cum_ref.shape)
        return -jnp.concatenate([cum] * t_new, axis=0)

    def select_bias(off, width):
        b = bias_ref[0, :, pl.ds(off, width)]
        return jnp.broadcast_to(b[:, None, :], (t_new, n_heads, width)).reshape(rows, width)

    @pl.when(p < n_main)
    def _():
        for r in range(pages_per_step):
            if decay:
                bias = decay_bias(f_refs[r][0, 0])
            else:
                bias = select_bias(pl.multiple_of((p * pages_per_step + r) * page, page), page)
            step(k_refs[r][0, 0], v_refs[r][0, 0], bias)

    @pl.when(p == n_main)
    def _():
        if decay:
            bias = decay_bias(fnew_ref[0])
            t = lax.broadcasted_iota(jnp.int32, (rows, page), 0) // n_heads
            j = lax.broadcasted_iota(jnp.int32, (rows, page), 1)
            bias = jnp.where(j <= t, bias, NEG_INF)
        else:
            bias = select_bias(n_main * pages_per_step * page, page)
        step(knew_ref[0], vnew_ref[0], bias)
        o = acc_ref[...] / l_ref[...]
        head_of_row = lax.broadcasted_iota(jnp.int32, o.shape, 0) % n_heads
        head_of_lane = lax.broadcasted_iota(jnp.int32, o.shape, 1) // HEAD_DIM
        o = jnp.where(head_of_row == head_of_lane, o, 0.0)
        out_ref[0] = jnp.sum(o.reshape(t_new, n_heads, o.shape[1]), axis=1).astype(BF16)


def _paged_attn(page_table, q_rows, k_pool, v_pool, li, k_new_pages, v_new_pages, pages_per_step,
                f_pool=None, f_new_pages=None, bias=None):
    nseq, n_pages = page_table.shape
    page = k_pool.shape[2]
    width = k_pool.shape[3]
    n_heads = width // HEAD_DIM
    rows = q_rows.shape[1]
    t_new = rows // n_heads
    n_main = n_pages // pages_per_step
    decay = f_pool is not None
    per_seq = lambda n, p, pt: (n, 0, 0)
    body = functools.partial(_paged_attn_body, pages_per_step=pages_per_step, n_main=n_main, page=page,
                             t_new=t_new, n_heads=n_heads, decay=decay)
    args = [page_table, q_rows] + [k_pool] * pages_per_step + [v_pool] * pages_per_step
    in_specs = [pl.BlockSpec((1,) + q_rows.shape[1:], per_seq)]
    in_specs += _page_specs(k_pool, li, pages_per_step, n_pages) + _page_specs(v_pool, li, pages_per_step, n_pages)
    if decay:
        args += [f_pool] * pages_per_step
        in_specs += _page_specs(f_pool, li, pages_per_step, n_pages)
    args += [k_new_pages, v_new_pages]
    in_specs += [pl.BlockSpec((1,) + k_new_pages.shape[1:], per_seq), pl.BlockSpec((1,) + v_new_pages.shape[1:], per_seq)]
    scratch = [pltpu.VMEM((rows, 1), F32), pltpu.VMEM((rows, 1), F32), pltpu.VMEM((rows, width), F32)]
    if decay:
        args.append(f_new_pages)
        in_specs.append(pl.BlockSpec((1,) + f_new_pages.shape[1:], per_seq))
        scratch.append(pltpu.VMEM((n_heads, LANES), F32))
    else:
        args.append(bias)
        in_specs.append(pl.BlockSpec((1,) + bias.shape[1:], per_seq))
    grid_spec = pltpu.PrefetchScalarGridSpec(
        num_scalar_prefetch=1, grid=(nseq, n_main + 1), in_specs=in_specs,
        out_specs=pl.BlockSpec((1, t_new, width), per_seq), scratch_shapes=scratch)
    return pl.pallas_call(
        body, grid_spec=grid_spec, out_shape=jax.ShapeDtypeStruct((nseq, t_new, width), BF16),
        compiler_params=_params(), name="paged_attn_decay" if decay else "paged_attn_select")(*args)


def _rope_tables(pos):
    half = HEAD_DIM // 2
    inv = ROPE_THETA ** (-jnp.arange(half, dtype=F32) / half)
    ang = pos.astype(F32)[:, None] * inv[None, :]
    cos, sin = jnp.cos(ang), jnp.sin(ang)
    reps = LANES // HEAD_DIM
    return jnp.tile(jnp.concatenate([cos, cos], axis=1), (1, reps)), jnp.tile(jnp.concatenate([-sin, sin], axis=1), (1, reps))


def _block_diag(w):
    g, bw, _ = w.shape
    eye = jnp.eye(g, dtype=w.dtype)
    return (w[:, :, None, :] * eye[:, None, :, None]).reshape(g * bw, g * bw)


def _pad_cols(w, width):
    return jnp.pad(w, ((0, 0), (0, width - w.shape[1])))


def _to_time_major(a):
    return jnp.swapaxes(a, 0, 1).reshape((a.shape[0] * a.shape[1],) + a.shape[2:])


def _from_time_major(a, n):
    return jnp.swapaxes(a.reshape((a.shape[0] // n, n) + a.shape[1:]), 0, 1)


def _expand_heads(a, n, n_heads):
    width = a.shape[1]
    per_seq = _from_time_major(a, n)
    mask = (jnp.arange(width)[None, :] // HEAD_DIM) == jnp.arange(n_heads)[:, None]
    out = jnp.where(mask[None, None], per_seq[:, :, None, :], jnp.zeros((), a.dtype))
    return out.reshape(n, -1, width)


def _new_pages(a, n, page):
    per_seq = _from_time_major(a, n)
    return jnp.pad(per_seq, ((0, 0), (0, page - per_seq.shape[1]), (0, 0)))


def _pick_tile(m, pref):
    t = min(m, pref)
    while m % t:
        t //= 2
    return t


def kernel(x_prompt, x_sample, cache_k_b, cache_v_b, cache_kidx_b, state_conv_a, cache_k_c, cache_v_c, cache_logf_c, state_conv_d, state_lru_h, page_table, g_ff1, w_ff1_gate, w_ff1_up, w_ff1_down, g_mix, g_ff2, w_ff2_gate, w_ff2_up, w_ff2_down, w_in_e, conv_a_w, w_out_e, w_in_o, b_f, conv_d_w, conv_d_b, w_rg_a, b_rg_a, w_rg_x, b_rg_x, lam, w_out_o, g_final):
    batch, seq, d_model = x_prompt.shape
    nseq, t_new, _ = x_sample.shape
    depth = g_ff1.shape[0]
    n_pool, page = cache_k_b.shape[1], cache_k_b.shape[2]
    n_pages = page_table.shape[1]
    past = n_pages * page
    wa = conv_a_w.shape[2]
    hb = w_out_e.shape[1] - wa
    wd = conv_d_w.shape[2]
    hc = w_out_o.shape[1] - wd
    n_heads_c = hc // HEAD_DIM
    pages_per_step = 4 if n_pages % 4 == 0 else 1

    mp, ms = batch * seq, nseq * t_new
    tm_p = _pick_tile(seq, 512)
    xp = x_prompt.reshape(mp, d_model)
    xs = _to_time_major(x_sample)
    cos_p, sin_p = _rope_tables(jnp.tile(jnp.arange(seq), batch))
    cos_s, sin_s = _rope_tables(past + jnp.repeat(jnp.arange(t_new), nseq))
    row2 = lambda v: v.reshape(1, -1)
    bf = lambda w: w.astype(BF16)

    outs_p, outs_s = {}, {}
    for l in range(depth):
        ffn1 = (row2(g_ff1[l]), bf(w_ff1_gate[l]), bf(w_ff1_up[l]), bf(w_ff1_down[l]))
        ffn2 = (row2(g_ff2[l]), bf(w_ff2_gate[l]), bf(w_ff2_up[l]), bf(w_ff2_down[l]))
        xp = _ffn(xp, *ffn1, tm=tm_p)
        xs = _ffn(xs, *ffn1, tm=ms)
        g = row2(g_mix[l])
        if l % 2 == 0:
            e = l // 2
            w = w_in_e[e]
            o_q, o_k, o_v, o_qi, o_kw = 3 * wa, 3 * wa + hb, 3 * wa + 2 * hb, 3 * wa + 3 * hb, 3 * wa + 3 * hb + H_IDX * D_IDX
            weights = (bf(w[:, :o_q]), bf(w[:, o_q:o_k]), bf(w[:, o_k:o_v]), bf(w[:, o_v:o_qi]), bf(w[:, o_qi:o_kw]),
                       bf(_pad_cols(w[:, o_kw:], LANES)))
            w_out = (bf(w_out_e[e][:wa]), bf(w_out_e[e][wa:]))

            hist_p = jnp.zeros((batch, conv_a_w.shape[1] - 1, wa), F32)
            ya, q, k32, kb, v32, vb, qi, kw, ki2, cstate = _proj_even(
                xp, g, weights, cos_p, sin_p, conv_a_w[e], hist_p, tm_p, seq // tm_p, 1)
            yb = _dsa_prompt(qi, kw, ki2, q, kb, vb, batch, seq, _pick_tile(seq, 128), _pick_tile(seq, 512))
            outs_p.setdefault("k_b", []).append(k32.reshape(batch, seq, hb // HEAD_DIM, HEAD_DIM))
            outs_p.setdefault("v_b", []).append(v32.reshape(batch, seq, hb // HEAD_DIM, HEAD_DIM))
            outs_p.setdefault("kidx_b", []).append(kw[:, :D_IDX].reshape(batch, seq, D_IDX))
            outs_p.setdefault("conv_a", []).append(cstate)
            mix_p = (ya, yb, *w_out)

            hist_s = _to_time_major(state_conv_a[e])[None]
            ya, q, k32, kb, v32, vb, qi, kw, ki2, cstate = _proj_even(
                xs, g, weights, cos_s, sin_s, conv_a_w[e], hist_s, ms, 1, nseq)
            qi_rows = _from_time_major(qi, nseq).reshape(nseq, t_new * H_IDX, D_IDX)
            w_rows = _from_time_major(kw[:, D_IDX:D_IDX + H_IDX], nseq).reshape(nseq, t_new * H_IDX, 1)
            scores = _idx_scores_sample(page_table, qi_rows, w_rows, cache_kidx_b, e,
                                        _new_pages(kw[:, :D_IDX], nseq, page), pages_per_step)
            length = scores.shape[2]
            n_sel = min(TOPK_MAX, (past + t_new) // 4)
            kc = page * 3 if (length // page) % 3 == 0 else page
            bias = _topk_rows(scores.reshape(ms, length), n_sel, _pick_tile(ms, 32), kc).reshape(nseq, t_new, length)
            yb = _paged_attn(page_table, _expand_heads(q, nseq, hb // HEAD_DIM),
                             cache_k_b.reshape(cache_k_b.shape[:3] + (hb,)), cache_v_b.reshape(cache_v_b.shape[:3] + (hb,)),
                             e, _new_pages(k32, nseq, page), _new_pages(v32, nseq, page), pages_per_step, bias=bias)
            yb = _to_time_major(yb)
            outs_s.setdefault("k_b", []).append(_from_time_major(k32, nseq).reshape(nseq, t_new, hb // HEAD_DIM, HEAD_DIM))
            outs_s.setdefault("v_b", []).append(_from_time_major(v32, nseq).reshape(nseq, t_new, hb // HEAD_DIM, HEAD_DIM))
            outs_s.setdefault("kidx_b", []).append(_from_time_major(kw[:, :D_IDX], nseq))
            outs_s.setdefault("conv_a", []).append(_from_time_major(cstate[0], nseq))
            mix_s = (ya, yb, *w_out)
        else:
            o = l // 2
            w = w_in_o[o]
            o_f, o_xd, o_gd = 3 * hc, 3 * hc + n_heads_c, 3 * hc + n_heads_c + wd
            weights = (bf(w[:, :o_f]), bf(_pad_cols(w[:, o_f:o_xd], LANES)), bf(w[:, o_xd:o_gd]), bf(w[:, o_gd:]))
            small = (_pad_cols(row2(b_f[o]), LANES), conv_d_w[o], row2(conv_d_b[o]), bf(_block_diag(w_rg_a[o])),
                     row2(b_rg_a[o]), bf(_block_diag(w_rg_x[o])), row2(b_rg_x[o]), row2(lam[o]))
            w_out = (bf(w_out_o[o][:hc]), bf(w_out_o[o][hc:]))

            hist_p = jnp.zeros((batch, conv_d_w.shape[1] - 1, wd), F32)
            h0_p = jnp.zeros((batch, 1, wd), F32)
            q, k32, kb, v32, vb, logf, yd, cstate, hlast = _proj_odd(
                xp, g, weights, small, hist_p, h0_p, tm_p, seq // tm_p, 1, 0)
            cum_t = _cumsum_heads(logf, batch, seq, _pick_tile(seq, 512))
            yc = _fox_prompt(q, kb, vb, cum_t, batch, seq, _pick_tile(seq, 256), _pick_tile(seq, 512))
            outs_p.setdefault("k_c", []).append(k32.reshape(batch, seq, n_heads_c, HEAD_DIM))
            outs_p.setdefault("v_c", []).append(v32.reshape(batch, seq, n_heads_c, HEAD_DIM))
            outs_p.setdefault("logf_c", []).append(logf.reshape(batch, seq, n_heads_c))
            outs_p.setdefault("conv_d", []).append(cstate)
            outs_p.setdefault("lru_h", []).append(hlast[:, 0])
            mix_p = (yc, yd, *w_out)

            hist_s = _to_time_major(state_conv_d[o])[None]
            h0_s = state_lru_h[o][None]
            q, k32, kb, v32, vb, logf, yd, cstate, hlast = _proj_odd(
                xs, g, weights, small, hist_s, h0_s, ms, 1, nseq, past)
            yc = _paged_attn(page_table, _expand_heads(q, nseq, n_heads_c),
                             cache_k_c.reshape(cache_k_c.shape[:3] + (hc,)), cache_v_c.reshape(cache_v_c.shape[:3] + (hc,)),
                             o, _new_pages(k32, nseq, page), _new_pages(v32, nseq, page), pages_per_step,
                             f_pool=cache_logf_c, f_new_pages=_new_pages(logf, nseq, page))
            yc = _to_time_major(yc)
            outs_s.setdefault("k_c", []).append(_from_time_major(k32, nseq).reshape(nseq, t_new, n_heads_c, HEAD_DIM))
            outs_s.setdefault("v_c", []).append(_from_time_major(v32, nseq).reshape(nseq, t_new, n_heads_c, HEAD_DIM))
            outs_s.setdefault("logf_c", []).append(_from_time_major(logf, nseq))
            outs_s.setdefault("conv_d", []).append(_from_time_major(cstate[0], nseq))
            outs_s.setdefault("lru_h", []).append(hlast[0])
            mix_s = (yc, yd, *w_out)
        gfin = row2(g_final) if l == depth - 1 else None
        xp = _ffn(xp, *ffn2, tm=tm_p, mix=mix_p, g_final=gfin)
        xs = _ffn(xs, *ffn2, tm=ms, mix=mix_s, g_final=gfin)

    names = ("k_b", "v_b", "kidx_b", "conv_a", "k_c", "v_c", "logf_c", "conv_d", "lru_h")
    st_p = tuple(jnp.stack(outs_p[nm]) for nm in names)
    st_s = tuple(jnp.stack(outs_s[nm]) for nm in names)
    return (xp.reshape(batch, seq, d_model), _from_time_major(xs, nseq)) + st_p + st_s
```

```python
import functools

import jax
import jax.numpy as jnp
import numpy as np
from jax import lax
from jax.experimental import pallas as pl
from jax.experimental.pallas import tpu as pltpu

F32 = jnp.float32
BF16 = jnp.bfloat16

HEAD_DIM = 64
D_IDX = 64
H_IDX = 8
TOPK_MAX = 256
LRU_BLOCKS = 8
RG_C = 8.0
ROPE_THETA = 10000.0
EPS = 1e-6

LANES = 128
VMEM_LIMIT = 56 * 1024 * 1024
NEG_INF = float("-inf")
M_INIT = -1e30
BISECT_ITERS = 48
TOPK_ROW_BLOCK = 64

_NT = (((1,), (1,)), ((), ()))


def _params():
    return pltpu.CompilerParams(vmem_limit_bytes=VMEM_LIMIT)


def _const_spec(shape):
    nd = len(shape)
    return pl.BlockSpec(shape, lambda *_: (0,) * nd, pipeline_mode=pl.Buffered(1))


def _dot(a, b):
    return jnp.dot(a, b, preferred_element_type=F32)


def _dot_nt(a, b, precision=None):
    return lax.dot_general(a, b, _NT, preferred_element_type=F32, precision=precision)


def _rmsnorm(x, g):
    ms = jnp.mean(x * x, axis=-1, keepdims=True)
    return x * lax.rsqrt(ms + EPS) * g


def _ffn_body(*refs, n_chunks, fc, has_mix, has_final):
    it = iter(refs)
    x_ref = next(it)
    if has_mix:
        ya_ref, yb_ref, woa_ref, wob_ref = next(it), next(it), next(it), next(it)
    g_ref, wg_ref, wu_ref, wd_ref = next(it), next(it), next(it), next(it)
    if has_final:
        gf_ref = next(it)
    out_ref = next(it)

    x = x_ref[...]
    if has_mix:
        x = x + _dot(ya_ref[...], woa_ref[...]) + _dot(yb_ref[...], wob_ref[...])
    h = _rmsnorm(x, g_ref[...]).astype(BF16)
    acc = None
    for c in range(n_chunks):
        gate = _dot(h, wg_ref[:, c * fc:(c + 1) * fc])
        up = _dot(h, wu_ref[:, c * fc:(c + 1) * fc])
        a = (gate * jax.nn.sigmoid(gate) * up).astype(BF16)
        part = _dot(a, wd_ref[c * fc:(c + 1) * fc, :])
        acc = part if acc is None else acc + part
    x = x + 0.5 * acc
    if has_final:
        x = _rmsnorm(x, gf_ref[...])
    out_ref[...] = x


def _ffn(x, g, wg, wu, wd, tm, mix=None, g_final=None):
    m, d = x.shape
    f = wg.shape[1]
    n_chunks = 2 if (f // 2) % LANES == 0 else 1
    fc = f // n_chunks
    row = lambda i: (i, 0)
    args, specs = [x], [pl.BlockSpec((tm, d), row)]
    if mix is not None:
        ya, yb, woa, wob = mix
        args += [ya, yb, woa, wob]
        specs += [pl.BlockSpec((tm, ya.shape[1]), row), pl.BlockSpec((tm, yb.shape[1]), row),
                  _const_spec(woa.shape), _const_spec(wob.shape)]
    args += [g, wg, wu, wd]
    specs += [_const_spec(g.shape), _const_spec(wg.shape), _const_spec(wu.shape), _const_spec(wd.shape)]
    if g_final is not None:
        args.append(g_final)
        specs.append(_const_spec(g_final.shape))
    body = functools.partial(_ffn_body, n_chunks=n_chunks, fc=fc, has_mix=mix is not None,
                             has_final=g_final is not None)
    return pl.pallas_call(
        body, grid=(m // tm,), in_specs=specs, out_specs=pl.BlockSpec((tm, d), row),
        out_shape=jax.ShapeDtypeStruct((m, d), F32), compiler_params=_params(), name="ffn")(*args)


def _rope_tile(xt, cos, sin_signed):
    lane = lax.broadcasted_iota(jnp.int32, xt.shape, 1)
    first_half = (lane % HEAD_DIM) < (HEAD_DIM // 2)
    partner = jnp.where(first_half, pltpu.roll(xt, LANES - HEAD_DIM // 2, 1), pltpu.roll(xt, HEAD_DIM // 2, 1))
    return xt * cos + partner * sin_signed


def _hist_rows(width, stride):
    need = (width - 1) * stride
    pad = -(-need // 8) * 8
    return need, pad


def _causal_conv(u, w_ref, ubuf, hist_ref, state_ref, first_tile, width, stride):
    tm = u.shape[0]
    need, pad = _hist_rows(width, stride)

    @pl.when(first_tile)
    def _():
        ubuf[pad - need:pad, :] = hist_ref[0]

    @pl.when(jnp.logical_not(first_tile))
    def _():
        ubuf[0:pad, :] = ubuf[tm:tm + pad, :]

    ubuf[pad:pad + tm, :] = u
    y = None
    for j in range(width):
        start = pad - (width - 1 - j) * stride
        term = w_ref[j:j + 1, :] * ubuf[start:start + tm, :]
        y = term if y is None else y + term
    state_ref[0] = ubuf[pad + tm - need:pad + tm, :]
    return y


def _proj_even_body(x_ref, g_ref, wc_ref, wq_ref, wk_ref, wv_ref, wqi_ref, wkw_ref, cos_ref, sin_ref,
                    cw_ref, hist_ref,
                    ya_ref, q_ref, k32_ref, kb_ref, v32_ref, vb_ref, qi_ref, kw_ref, ki2_ref, state_ref,
                    ubuf, *, tiles_per_seq, stride, wa):
    first_tile = (pl.program_id(0) % tiles_per_seq) == 0
    h = _rmsnorm(x_ref[...], g_ref[...]).astype(BF16)
    cos = cos_ref[...]
    sin = sin_ref[...]
    scale = HEAD_DIM ** -0.5

    zc = _dot(h, wc_ref[...])
    xin, gb, gc = zc[:, :wa], zc[:, wa:2 * wa], zc[:, 2 * wa:]
    yconv = _causal_conv(gc * xin, cw_ref, ubuf, hist_ref, state_ref, first_tile, cw_ref.shape[0], stride)
    ya_ref[...] = (gb * yconv).astype(BF16)

    n_tiles = wq_ref.shape[1] // LANES
    q = _dot(h, wq_ref[...])
    k = _dot(h, wk_ref[...])
    qi = _dot(h, wqi_ref[...])
    for t in range(n_tiles):
        sl = slice(t * LANES, (t + 1) * LANES)
        q_ref[:, sl] = (_rope_tile(q[:, sl], cos, sin) * scale).astype(BF16)
        kt = _rope_tile(k[:, sl], cos, sin)
        k32_ref[:, sl] = kt
        kb_ref[:, sl] = kt.astype(BF16)
        qi_ref[:, sl] = (_rope_tile(qi[:, sl], cos, sin) * (D_IDX ** -0.5)).astype(BF16)
    v = _dot(h, wv_ref[...])
    v32_ref[...] = v
    vb_ref[...] = v.astype(BF16)

    kw = _dot(h, wkw_ref[...])
    lane = lax.broadcasted_iota(jnp.int32, kw.shape, 1)
    is_ki = lane < D_IDX
    kw = jnp.where(is_ki, _rope_tile(kw, cos, sin), kw)
    kw_ref[...] = kw
    ki2_ref[...] = jnp.where(is_ki, kw, pltpu.roll(kw, D_IDX, 1)).astype(BF16)


def _proj_even(x, g, w, cos, sin, conv_w, hist, tm, tiles_per_seq, stride):
    m, d = x.shape
    wc, wq, wk, wv, wqi, wkw = w
    wa = wc.shape[1] // 3
    hb = wq.shape[1]
    width = conv_w.shape[0]
    need, pad = _hist_rows(width, stride)
    nb = hist.shape[0]
    row = lambda i: (i, 0)
    seq = lambda i: (i // tiles_per_seq, 0, 0)
    in_specs = [pl.BlockSpec((tm, d), row), _const_spec(g.shape)]
    in_specs += [_const_spec(a.shape) for a in (wc, wq, wk, wv, wqi, wkw)]
    in_specs += [pl.BlockSpec((tm, LANES), row), pl.BlockSpec((tm, LANES), row), _const_spec(conv_w.shape),
                 pl.BlockSpec((1, need, wa), seq)]
    out_shape = [jax.ShapeDtypeStruct((m, wa), BF16), jax.ShapeDtypeStruct((m, hb), BF16),
                 jax.ShapeDtypeStruct((m, hb), F32), jax.ShapeDtypeStruct((m, hb), BF16),
                 jax.ShapeDtypeStruct((m, hb), F32), jax.ShapeDtypeStruct((m, hb), BF16),
                 jax.ShapeDtypeStruct((m, hb), BF16), jax.ShapeDtypeStruct((m, LANES), F32),
                 jax.ShapeDtypeStruct((m, LANES), BF16), jax.ShapeDtypeStruct((nb, need, wa), F32)]
    out_specs = [pl.BlockSpec((tm, wa), row)] + [pl.BlockSpec((tm, hb), row)] * 6
    out_specs += [pl.BlockSpec((tm, LANES), row), pl.BlockSpec((tm, LANES), row), pl.BlockSpec((1, need, wa), seq)]
    body = functools.partial(_proj_even_body, tiles_per_seq=tiles_per_seq, stride=stride, wa=wa)
    return pl.pallas_call(
        body, grid=(m // tm,), in_specs=in_specs, out_specs=out_specs, out_shape=out_shape,
        scratch_shapes=[pltpu.VMEM((pad + tm, wa), F32)], compiler_params=_params(), name="proj_even")(
            x, g, wc, wq, wk, wv, wqi, wkw, cos, sin, conv_w, hist)


def _shift_rows(x, d, fill):
    row = lax.broadcasted_iota(jnp.int32, x.shape, 0)
    return jnp.where(row >= d, pltpu.roll(x, d, 0), fill)


def _proj_odd_body(x_ref, g_ref, wqkv_ref, wf_ref, wxd_ref, wgd_ref, bf_ref, cw_ref, cb_ref, wra_ref, bra_ref,
                   wrx_ref, brx_ref, lam_ref, hist_ref, h0_ref,
                   q_ref, k32_ref, kb_ref, v32_ref, vb_ref, logf_ref, yd_ref, state_ref, hlast_ref,
                   ubuf, hcar, *, tiles_per_seq, stride, pos_base, hc):
    tile_in_seq = pl.program_id(0) % tiles_per_seq
    first_tile = tile_in_seq == 0
    tm = x_ref.shape[0]
    h = _rmsnorm(x_ref[...], g_ref[...]).astype(BF16)

    z = _dot(h, wqkv_ref[...])
    q_ref[...] = (z[:, :hc] * (HEAD_DIM ** -0.5)).astype(BF16)
    k = z[:, hc:2 * hc]
    v = z[:, 2 * hc:]
    k32_ref[...] = k
    kb_ref[...] = k.astype(BF16)
    v32_ref[...] = v
    vb_ref[...] = v.astype(BF16)

    zf = _dot(h, wf_ref[...]) + bf_ref[...]
    logf = jnp.minimum(zf, 0.0) - jnp.log1p(jnp.exp(-jnp.abs(zf)))
    logf_ref[...] = logf[:, :logf_ref.shape[1]]

    xd = _dot(h, wxd_ref[...])
    gd = _dot(h, wgd_ref[...])
    xc = _causal_conv(xd, cw_ref, ubuf, hist_ref, state_ref, first_tile, cw_ref.shape[0], stride) + cb_ref[...]
    xcb = xc.astype(BF16)
    r = jax.nn.sigmoid(_dot(xcb, wra_ref[...]) + bra_ref[...])
    ig = jax.nn.sigmoid(_dot(xcb, wrx_ref[...]) + brx_ref[...])
    lam = lam_ref[...]
    softplus_neg_lam = jnp.maximum(-lam, 0.0) + jnp.log1p(jnp.exp(-jnp.abs(lam)))
    log_a = -RG_C * r * softplus_neg_lam
    a = jnp.exp(log_a)
    row = lax.broadcasted_iota(jnp.int32, (tm, 1), 0)
    pos = pos_base + tile_in_seq * (tm // stride) + row // stride
    mult = jnp.where(pos == 0, 1.0, jnp.sqrt(1.0 - jnp.exp(2.0 * log_a)))
    b = mult * ig * xc

    d = stride
    while d < tm:
        a_prev = _shift_rows(a, d, 1.0)
        b_prev = _shift_rows(b, d, 0.0)
        b = a * b_prev + b
        a = a * a_prev
        d *= 2

    @pl.when(first_tile)
    def _():
        hcar[...] = h0_ref[0]

    carry = hcar[...]
    if stride == 1:
        carry_rows = carry
    else:
        carry_rows = jnp.concatenate([carry] * (tm // stride), axis=0)
    hs = a * carry_rows + b
    hcar[...] = hs[tm - stride:, :]
    hlast_ref[0] = hs[tm - stride:, :]
    yd_ref[...] = (hs * jax.nn.gelu(gd)).astype(BF16)


def _proj_odd(x, g, w, small, hist, h0, tm, tiles_per_seq, stride, pos_base):
    m, d = x.shape
    wqkv, wf, wxd, wgd = w
    bf, conv_w, conv_b, wra, bra, wrx, brx, lam = small
    hc = wqkv.shape[1] // 3
    wd_ = wxd.shape[1]
    n_heads = hc // HEAD_DIM
    width = conv_w.shape[0]
    need, pad = _hist_rows(width, stride)
    nb = hist.shape[0]
    row = lambda i: (i, 0)
    seq = lambda i: (i // tiles_per_seq, 0, 0)
    ins = [x, g, wqkv, wf, wxd, wgd, bf, conv_w, conv_b, wra, bra, wrx, brx, lam, hist, h0]
    in_specs = [pl.BlockSpec((tm, d), row)] + [_const_spec(a.shape) for a in ins[1:14]]
    in_specs += [pl.BlockSpec((1, need, wd_), seq), pl.BlockSpec((1, stride, wd_), seq)]
    out_shape = [jax.ShapeDtypeStruct((m, hc), BF16), jax.ShapeDtypeStruct((m, hc), F32),
                 jax.ShapeDtypeStruct((m, hc), BF16), jax.ShapeDtypeStruct((m, hc), F32),
                 jax.ShapeDtypeStruct((m, hc), BF16), jax.ShapeDtypeStruct((m, n_heads), F32),
                 jax.ShapeDtypeStruct((m, wd_), BF16), jax.ShapeDtypeStruct((nb, need, wd_), F32),
                 jax.ShapeDtypeStruct((nb, stride, wd_), F32)]
    out_specs = [pl.BlockSpec((tm, hc), row)] * 5 + [pl.BlockSpec((tm, n_heads), row), pl.BlockSpec((tm, wd_), row),
                                                      pl.BlockSpec((1, need, wd_), seq),
                                                      pl.BlockSpec((1, stride, wd_), seq)]
    body = functools.partial(_proj_odd_body, tiles_per_seq=tiles_per_seq, stride=stride, pos_base=pos_base, hc=hc)
    return pl.pallas_call(
        body, grid=(m // tm,), in_specs=in_specs, out_specs=out_specs, out_shape=out_shape,
        scratch_shapes=[pltpu.VMEM((pad + tm, wd_), F32), pltpu.VMEM((stride, wd_), F32)],
        compiler_params=_params(), name="proj_odd")(*ins)


def _cumsum_lanes(lf, carry):
    t, nh = lf.shape
    eye = (lax.broadcasted_iota(jnp.int32, (nh, nh), 0) == lax.broadcasted_iota(jnp.int32, (nh, nh), 1)).astype(F32)
    lf_t = _dot_nt(eye, lf, precision=lax.Precision.HIGHEST)
    tri = (lax.broadcasted_iota(jnp.int32, (t, t), 1) <= lax.broadcasted_iota(jnp.int32, (t, t), 0)).astype(F32)
    return _dot_nt(lf_t, tri, precision=lax.Precision.HIGHEST) + carry


def _cumsum_body(lf_ref, out_ref, carry):
    @pl.when(pl.program_id(1) == 0)
    def _():
        carry[...] = jnp.zeros_like(carry)

    cum = _cumsum_lanes(lf_ref[...], carry[:, 0:1])
    out_ref[0] = cum
    carry[...] = jnp.broadcast_to(cum[:, cum.shape[1] - 1:], carry.shape)


def _cumsum_heads(logf, n, s, tc):
    nh = logf.shape[1]
    tiles = s // tc
    return pl.pallas_call(
        _cumsum_body, grid=(n, tiles),
        in_specs=[pl.BlockSpec((tc, nh), lambda b, i: (b * tiles + i, 0))],
        out_specs=pl.BlockSpec((1, nh, tc), lambda b, i: (b, 0, i)),
        out_shape=jax.ShapeDtypeStruct((n, nh, s), F32),
        scratch_shapes=[pltpu.VMEM((nh, LANES), F32)], compiler_params=_params(), name="cumsum")(logf)


def _pair_mask(shape, parity):
    lane = lax.broadcasted_iota(jnp.int32, shape, 1)
    return (lane >= HEAD_DIM) == (parity == 1)


def _stack_pair_queries(q_ref, qs_ref, tq):
    for j in range(q_ref.shape[1] // LANES):
        q_pair = q_ref[:, j * LANES:(j + 1) * LANES]
        zero = jnp.zeros_like(q_pair)
        qs_ref[j, 0:tq, :] = jnp.where(_pair_mask(q_pair.shape, 0), q_pair, zero)
        qs_ref[j, tq:2 * tq, :] = jnp.where(_pair_mask(q_pair.shape, 1), q_pair, zero)


def _attend_pairs(qs_ref, k_ref, v_ref, out_ref, s_buf, stat_ref, lsum_ref, acc_ref, *, tq, kb, n_plain, n_chunks,
                  logits_fn):
    tiles = kb // LANES

    def pair(j, _):
        col = pl.ds(pl.multiple_of(j * LANES, LANES), LANES)
        stat_ref[...] = jnp.full(stat_ref.shape, NEG_INF, F32)

        def pass1(c, _, masked):
            rows = pl.ds(pl.multiple_of(c * kb, kb), kb)
            s = logits_fn(j, c, _dot_nt(qs_ref[j], k_ref[rows, col]), masked)
            s_buf[:, rows] = s
            mt = s[:, 0:LANES]
            for t in range(1, tiles):
                mt = jnp.maximum(mt, s[:, t * LANES:(t + 1) * LANES])
            stat_ref[...] = jnp.maximum(stat_ref[...], mt)
            return 0

        lax.fori_loop(0, n_plain, functools.partial(pass1, masked=False), 0)
        lax.fori_loop(n_plain, n_chunks, functools.partial(pass1, masked=True), 0)
        m = jnp.max(stat_ref[...], axis=1, keepdims=True)
        stat_ref[...] = jnp.broadcast_to(m, stat_ref.shape)
        lsum_ref[...] = jnp.zeros(lsum_ref.shape, F32)
        acc_ref[...] = jnp.zeros(acc_ref.shape, F32)

        def pass2(c, _):
            rows = pl.ds(pl.multiple_of(c * kb, kb), kb)
            m_rows = jnp.concatenate([stat_ref[...]] * tiles, axis=1)
            p = jnp.exp(s_buf[:, rows] - m_rows)
            part = p[:, 0:LANES]
            for t in range(1, tiles):
                part = part + p[:, t * LANES:(t + 1) * LANES]
            lsum_ref[...] += part
            acc_ref[...] += _dot(p.astype(BF16), v_ref[rows, col])
            return 0

        lax.fori_loop(0, n_chunks, pass2, 0)
        o = acc_ref[...] / jnp.sum(lsum_ref[...], axis=1, keepdims=True)
        even, odd = o[0:tq], o[tq:2 * tq]
        out_ref[:, col] = jnp.where(_pair_mask(even.shape, 1), odd, even).astype(BF16)
        return 0

    lax.fori_loop(0, qs_ref.shape[0], pair, 0)


def _pair_scratch(n_pairs, tq, s):
    return [pltpu.VMEM((n_pairs, 2 * tq, LANES), BF16), pltpu.VMEM((2 * tq, s), F32),
            pltpu.VMEM((2 * tq, LANES), F32), pltpu.VMEM((2 * tq, LANES), F32), pltpu.VMEM((2 * tq, LANES), F32)]


def _fox_body(q_ref, k_ref, v_ref, cum_ref, out_ref, qs_ref, s_buf, stat_ref, lsum_ref, acc_ref, *, tq, kc):
    i = pl.program_id(1)
    n_full = (i * tq) // kc
    n_tot = ((i + 1) * tq + kc - 1) // kc
    _stack_pair_queries(q_ref, qs_ref, tq)

    def logits(j, c, raw, masked):
        rows = pl.ds(pl.multiple_of(c * kc, kc), kc)
        halves = []
        for parity in (0, 1):
            sh = raw[parity * tq:(parity + 1) * tq] - cum_ref[0, pl.ds(2 * j + parity, 1), rows]
            if masked:
                qpos = i * tq + lax.broadcasted_iota(jnp.int32, (tq, kc), 0)
                kpos = c * kc + lax.broadcasted_iota(jnp.int32, (tq, kc), 1)
                sh = jnp.where(kpos <= qpos, sh, NEG_INF)
            halves.append(sh)
        return jnp.concatenate(halves, axis=0)

    _attend_pairs(qs_ref, k_ref, v_ref, out_ref, s_buf, stat_ref, lsum_ref, acc_ref, tq=tq, kb=kc,
                  n_plain=n_full, n_chunks=n_tot, logits_fn=logits)


def _fox_prompt(q, kb, vb, cum_t, n, s, tq, kc):
    hc = q.shape[1]
    nh = cum_t.shape[1]
    nq = s // tq
    body = functools.partial(_fox_body, tq=tq, kc=kc)
    scratch = _pair_scratch(hc // LANES, tq, s)
    one = pl.Buffered(1)
    return pl.pallas_call(
        body, grid=(n, nq),
        in_specs=[pl.BlockSpec((tq, hc), lambda b, i: (b * nq + i, 0)),
                  pl.BlockSpec((s, hc), lambda b, i: (b, 0), pipeline_mode=one),
                  pl.BlockSpec((s, hc), lambda b, i: (b, 0), pipeline_mode=one),
                  pl.BlockSpec((1, nh, s), lambda b, i: (b, 0, 0), pipeline_mode=one)],
        out_specs=pl.BlockSpec((tq, hc), lambda b, i: (b * nq + i, 0)),
        out_shape=jax.ShapeDtypeStruct((n * s, hc), BF16), scratch_shapes=scratch,
        compiler_params=_params(), name="fox_prompt")(
            q, kb, vb, cum_t)


def _topk_to_bias(sc_ref, nkc, kc, n_sel, on=0.0, off=NEG_INF):
    rows = sc_ref.shape[0]
    tiles = kc // LANES
    pos_inf = float("inf")

    rb = min(rows, TOPK_ROW_BLOCK)

    def lane_fold(fn, init):
        parts = []
        for r0 in range(0, rows, rb):
            rs = slice(r0, r0 + rb)

            def body(c, acc, rs=rs):
                base = pl.multiple_of(c * kc, kc)
                for t in range(tiles):
                    acc = fn(acc, sc_ref[rs, pl.ds(base + t * LANES, LANES)], rs)
                return acc
            parts.append(lax.fori_loop(0, nkc, body, jnp.full((rb, LANES), init, F32)))
        return jnp.concatenate(parts, axis=0)

    def count(pred_fn):
        part = lane_fold(lambda acc, x, rs: acc + jnp.where(pred_fn(x, rs), 1.0, 0.0), 0.0)
        return jnp.sum(part, axis=1, keepdims=True)

    mx = jnp.max(lane_fold(lambda acc, x, rs: jnp.maximum(acc, x), NEG_INF), axis=1, keepdims=True)
    mn = jnp.min(lane_fold(lambda acc, x, rs: jnp.minimum(acc, jnp.where(x == NEG_INF, pos_inf, x)), pos_inf),
                 axis=1, keepdims=True)
    n_visible = count(lambda x, rs: x > NEG_INF)

    def live_rows(lo, hi, cnt):
        mid = lo + 0.5 * (hi - lo)
        splittable = jnp.logical_and(mid > lo, mid < hi)
        return jnp.where(jnp.logical_and(cnt > n_sel, splittable), 1.0, 0.0)

    def bisect(carry):
        it, _, lo, hi, cnt = carry
        mid = lo + 0.5 * (hi - lo)
        midb = jnp.broadcast_to(mid, (rows, LANES))
        c_mid = count(lambda x, rs: x >= midb[rs])
        enough = c_mid >= n_sel
        lo, hi, cnt = jnp.where(enough, mid, lo), jnp.where(enough, hi, mid), jnp.where(enough, c_mid, cnt)
        return it + 1, jnp.max(live_rows(lo, hi, cnt)), lo, hi, cnt

    def unfinished(carry):
        return jnp.logical_and(carry[0] < BISECT_ITERS, carry[1] > 0.0)

    _, _, lo, _, cnt = lax.while_loop(
        unfinished, bisect, (jnp.int32(0), jnp.max(live_rows(mn, mx, n_visible)), mn, mx, n_visible))
    lob = jnp.broadcast_to(lo, (rows, LANES))
    has_ties = jnp.max(cnt) > n_sel

    @pl.when(jnp.logical_not(has_ties))
    def _():
        def body(c, _):
            base = pl.multiple_of(c * kc, kc)
            for t in range(tiles):
                col = pl.ds(base + t * LANES, LANES)
                sc_ref[:, col] = jnp.where(sc_ref[:, col] >= lob, on, off)
            return 0
        lax.fori_loop(0, nkc, body, 0)

    @pl.when(has_ties)
    def _():
        thr = jnp.min(lane_fold(lambda acc, x, rs: jnp.minimum(acc, jnp.where(x >= lob[rs], x, pos_inf)), pos_inf),
                      axis=1, keepdims=True)
        thrb = jnp.broadcast_to(thr, (rows, LANES))
        need = n_sel - count(lambda x, rs: x > thrb[rs])
        upper = (lax.broadcasted_iota(jnp.int32, (kc, kc), 0) <= lax.broadcasted_iota(jnp.int32, (kc, kc), 1))
        upper = jnp.where(upper, 1.0, 0.0).astype(BF16)

        def body(c, seen):
            col = pl.ds(pl.multiple_of(c * kc, kc), kc)
            x = sc_ref[:, col]
            eq = jnp.where(x == thr, 1.0, 0.0)
            rank = _dot(eq.astype(BF16), upper) + seen
            keep_eq = jnp.where(rank <= need, eq, 0.0)
            sc_ref[:, col] = jnp.where(x > thr, on, jnp.where(keep_eq > 0.5, on, off))
            return seen + jnp.sum(eq, axis=1, keepdims=True)
        lax.fori_loop(0, nkc, body, jnp.zeros((rows, 1), F32))


def _dsa_body(qi_ref, kw_ref, ki2_ref, q_ref, k_ref, v_ref, out_ref, sc_ref, wb_ref, qis_ref,
              qs_ref, s_buf, stat_ref, lsum_ref, acc_ref, *, tq, kc, ks, kb, n_sel):
    i = pl.program_id(1)
    nkc = ((i + 1) * tq + kc - 1) // kc

    kw = kw_ref[...]
    for h in range(H_IDX):
        wb_ref[h] = jnp.broadcast_to(kw[:, D_IDX + h:D_IDX + h + 1] * (H_IDX ** -0.5), (tq, LANES))
        j, parity = divmod(h, 2)
        qi_pair = qi_ref[:, j * LANES:(j + 1) * LANES]
        qis_ref[h * tq:(h + 1) * tq, :] = jnp.where(_pair_mask(qi_pair.shape, parity), qi_pair, jnp.zeros_like(qi_pair))
    qpos = i * tq + lax.broadcasted_iota(jnp.int32, (tq, LANES), 0)
    kiota = lax.broadcasted_iota(jnp.int32, (tq, LANES), 1)

    def score_chunk(c, _):
        base = pl.multiple_of(c * ks, ks)
        d = _dot_nt(qis_ref[...], ki2_ref[pl.ds(base, ks), :])
        for t in range(ks // LANES):
            acc = None
            for h in range(H_IDX):
                term = jnp.maximum(d[h * tq:(h + 1) * tq, t * LANES:(t + 1) * LANES], 0.0) * wb_ref[h]
                acc = term if acc is None else acc + term
            kpos = base + t * LANES + kiota
            sc_ref[:, pl.ds(base + t * LANES, LANES)] = jnp.where(kpos <= qpos, acc, NEG_INF)
        return 0

    n_big = ((i + 1) * tq + kb - 1) // kb
    n_score = ((i + 1) * tq + ks - 1) // ks
    lax.fori_loop(0, n_score, score_chunk, 0)

    def fill_hidden(c, _):
        sc_ref[:, pl.ds(pl.multiple_of(c * ks, ks), ks)] = jnp.full((tq, ks), NEG_INF, F32)
        return 0

    lax.fori_loop(n_score, n_big * (kb // ks), fill_hidden, 0)
    _topk_to_bias(sc_ref, nkc, kc, n_sel)

    _stack_pair_queries(q_ref, qs_ref, tq)

    def logits(j, c, raw, masked):
        bias = sc_ref[:, pl.ds(pl.multiple_of(c * kb, kb), kb)]
        return raw + jnp.concatenate([bias, bias], axis=0)

    _attend_pairs(qs_ref, k_ref, v_ref, out_ref, s_buf, stat_ref, lsum_ref, acc_ref, tq=tq, kb=kb,
                  n_plain=n_big, n_chunks=n_big, logits_fn=logits)


def _dsa_prompt(qi, kw, ki2, q, kb, vb, n, s, tq, kc, kbig):
    hb = q.shape[1]
    nq = s // tq
    n_sel = min(TOPK_MAX, s // 4)
    ks = min(kc, 2 * LANES)
    body = functools.partial(_dsa_body, tq=tq, kc=kc, ks=ks, kb=kbig, n_sel=n_sel)
    scratch = [pltpu.VMEM((tq, s), F32), pltpu.VMEM((H_IDX, tq, LANES), F32), pltpu.VMEM((H_IDX * tq, LANES), BF16)]
    scratch += _pair_scratch(hb // LANES, tq, s)
    one = pl.Buffered(1)
    qrow = lambda b, i: (b * nq + i, 0)
    return pl.pallas_call(
        body, grid=(n, nq),
        in_specs=[pl.BlockSpec((tq, hb), qrow), pl.BlockSpec((tq, LANES), qrow),
                  pl.BlockSpec((s, LANES), lambda b, i: (b, 0), pipeline_mode=one),
                  pl.BlockSpec((tq, hb), qrow),
                  pl.BlockSpec((s, hb), lambda b, i: (b, 0), pipeline_mode=one),
                  pl.BlockSpec((s, hb), lambda b, i: (b, 0), pipeline_mode=one)],
        out_specs=pl.BlockSpec((tq, hb), qrow),
        out_shape=jax.ShapeDtypeStruct((n * s, hb), BF16),
        scratch_shapes=scratch, compiler_params=_params(), name="dsa_prompt")(qi, kw, ki2, q, kb, vb)


def _page_specs(pool, li, pages_per_step, n_pages):
    def spec(r):
        def index(n, p, pt):
            return (li, pt[n, jnp.minimum(p * pages_per_step + r, n_pages - 1)]) + (0,) * (pool.ndim - 2)
        return pl.BlockSpec((1, 1) + pool.shape[2:], index)
    return [spec(r) for r in range(pages_per_step)]


def _idx_scores_body(pt_ref, qi_ref, w_ref, *rest, pages_per_step, n_main, page, t_new):
    page_refs = rest[:pages_per_step]
    new_ref, out_ref = rest[pages_per_step], rest[pages_per_step + 1]
    p = pl.program_id(1)
    qi = qi_ref[0]
    w = w_ref[0] * (H_IDX ** -0.5)

    def scores(ki):
        d = _dot_nt(qi, ki.astype(BF16))
        contrib = jnp.maximum(d, 0.0) * w
        return jnp.sum(contrib.reshape(t_new, H_IDX, page), axis=1)

    @pl.when(p < n_main)
    def _():
        for r in range(pages_per_step):
            off = pl.multiple_of((p * pages_per_step + r) * page, page)
            out_ref[0, :, pl.ds(off, page)] = scores(page_refs[r][0, 0])

    @pl.when(p == n_main)
    def _():
        sc = scores(new_ref[0])
        t = lax.broadcasted_iota(jnp.int32, sc.shape, 0)
        j = lax.broadcasted_iota(jnp.int32, sc.shape, 1)
        out_ref[0, :, n_main * pages_per_step * page:] = jnp.where(j <= t, sc, NEG_INF)


def _idx_scores_sample(page_table, qi_rows, w_rows, ki_pool, li, ki_new_pages, pages_per_step):
    nseq, n_pages = page_table.shape
    page = ki_pool.shape[2]
    t_new = qi_rows.shape[1] // H_IDX
    n_main = n_pages // pages_per_step
    length = (n_pages + 1) * page
    body = functools.partial(_idx_scores_body, pages_per_step=pages_per_step, n_main=n_main, page=page, t_new=t_new)
    per_seq = lambda n, p, pt: (n, 0, 0)
    grid_spec = pltpu.PrefetchScalarGridSpec(
        num_scalar_prefetch=1, grid=(nseq, n_main + 1),
        in_specs=[pl.BlockSpec((1,) + qi_rows.shape[1:], per_seq), pl.BlockSpec((1,) + w_rows.shape[1:], per_seq)]
        + _page_specs(ki_pool, li, pages_per_step, n_pages)
        + [pl.BlockSpec((1,) + ki_new_pages.shape[1:], per_seq)],
        out_specs=pl.BlockSpec((1, t_new, length), per_seq))
    return pl.pallas_call(
        body, grid_spec=grid_spec, out_shape=jax.ShapeDtypeStruct((nseq, t_new, length), F32),
        compiler_params=_params(), name="idx_scores_sample")(
            page_table, qi_rows, w_rows, *([ki_pool] * pages_per_step), ki_new_pages)


def _topk_rows_body(sc_ref, out_ref, *, kc, n_sel):
    out_ref[...] = sc_ref[...]
    _topk_to_bias(out_ref, out_ref.shape[1] // kc, kc, n_sel, on=1.0, off=0.0)


def _topk_rows(scores, n_sel, rows, kc):
    r, length = scores.shape
    body = functools.partial(_topk_rows_body, kc=kc, n_sel=n_sel)
    return pl.pallas_call(
        body, grid=(r // rows,), in_specs=[pl.BlockSpec((rows, length), lambda i: (i, 0))],
        out_specs=pl.BlockSpec((rows, length), lambda i: (i, 0)),
        out_shape=jax.ShapeDtypeStruct((r, length), F32), compiler_params=_params(), name="topk_rows")(scores)


def _split3(x):
    hi = x.astype(BF16)
    r1 = x - hi.astype(F32)
    mid = r1.astype(BF16)
    lo = (r1 - mid.astype(F32)).astype(BF16)
    return hi, mid, lo


def _decay_bias_body(pt_ref, *rest, pages_per_step, n_pages, n_heads):
    page_refs = rest[:pages_per_step]
    new_ref, out_ref, lf_ref = rest[pages_per_step:]
    p = pl.program_id(1)
    n_pad, width = lf_ref.shape

    @pl.when(p == 0)
    def _():
        lf_ref[...] = jnp.zeros(lf_ref.shape, F32)
        lf_ref[n_pages:n_pages + 1, :] = new_ref[0]

    for r in range(pages_per_step):
        lf_ref[pl.ds(p * pages_per_step + r, 1), :] = page_refs[r][0, 0]

    @pl.when(p == pl.num_programs(1) - 1)
    def _():
        src = lax.broadcasted_iota(jnp.int32, (width, width), 0)
        dst = lax.broadcasted_iota(jnp.int32, (width, width), 1)
        same_head = (src % n_heads) == (dst % n_heads)
        within = jnp.where(same_head, jnp.where(src <= dst, 1.0, 0.0), 0.0).astype(BF16)
        total = jnp.where(same_head, 1.0, 0.0).astype(BF16)
        earlier = jnp.where(lax.broadcasted_iota(jnp.int32, (n_pad, n_pad), 1)
                            < lax.broadcasted_iota(jnp.int32, (n_pad, n_pad), 0), 1.0, 0.0).astype(BF16)
        parts = _split3(lf_ref[...])
        cum = sum(_dot(x, within) for x in parts)
        tot = sum(_dot(x, total) for x in parts)
        cum = cum + sum(_dot(earlier, x) for x in _split3(tot))
        out_ref[0] = -cum


def _decay_bias_sample(page_table, f_pool_flat, li, f_new_flat, n_heads):
    nseq, n_pages = page_table.shape
    width = f_pool_flat.shape[3]
    pages_per_step = 8 if n_pages % 8 == 0 else 1
    n_pad = -(-(n_pages + 1) // 8) * 8
    per_seq = lambda n, p, pt: (n, 0, 0)
    body = functools.partial(_decay_bias_body, pages_per_step=pages_per_step, n_pages=n_pages, n_heads=n_heads)
    grid_spec = pltpu.PrefetchScalarGridSpec(
        num_scalar_prefetch=1, grid=(nseq, n_pages // pages_per_step),
        in_specs=_page_specs(f_pool_flat, li, pages_per_step, n_pages) + [pl.BlockSpec((1, 1, width), per_seq)],
        out_specs=pl.BlockSpec((1, n_pad, width), per_seq), scratch_shapes=[pltpu.VMEM((n_pad, width), F32)])
    return pl.pallas_call(
        body, grid_spec=grid_spec, out_shape=jax.ShapeDtypeStruct((nseq, n_pad, width), F32),
        compiler_params=_params(), name="decay_bias_sample")(
            page_table, *([f_pool_flat] * pages_per_step), f_new_flat)


def _paged_attn_body(pt_ref, q_ref, *rest, pages_per_step, n_main, page, t_new, n_heads, decay):
    it = iter(rest)
    k_refs = [next(it) for _ in range(pages_per_step)]
    v_refs = [next(it) for _ in range(pages_per_step)]
    knew_ref, vnew_ref = next(it), next(it)
    side_ref = next(it)
    if not decay:
        expand_ref = next(it)
    out_ref = next(it)
    m_ref, l_ref, acc_ref = next(it), next(it), next(it)
    p = pl.program_id(1)
    rows, cols = t_new * n_heads, page * n_heads
    q = q_ref[0]
    row_id = lax.broadcasted_iota(jnp.int32, (rows, cols), 0)
    col_id = lax.broadcasted_iota(jnp.int32, (rows, cols), 1)
    head_ok = (col_id % n_heads) == (row_id % n_heads)

    @pl.when(p == 0)
    def _():
        m_ref[...] = jnp.full(m_ref.shape, M_INIT, F32)
        l_ref[...] = jnp.zeros(l_ref.shape, F32)
        acc_ref[...] = jnp.zeros(acc_ref.shape, F32)

    def page_logits(kp, page_idx, new_page):
        s = _dot_nt(q, kp.reshape(cols, HEAD_DIM).astype(BF16))
        if decay:
            s = s + side_ref[0, pl.ds(page_idx, 1), :]
            if new_page:
                s = jnp.where(col_id // n_heads <= row_id // n_heads, s, NEG_INF)
        else:
            sel = side_ref[0, :, pl.ds(pl.multiple_of(page_idx * page, page), page)]
            sel_rows = jnp.broadcast_to(sel[:, None, :], (t_new, n_heads, page)).reshape(rows, page)
            picked = _dot(sel_rows.astype(BF16), expand_ref[...])
            s = jnp.where(picked > 0.5, s, NEG_INF)
        return jnp.where(head_ok, s, NEG_INF)

    def update(logits, values):
        m = m_ref[...]
        m_new = m
        for s in logits:
            m_new = jnp.maximum(m_new, jnp.max(s, axis=1, keepdims=True))
        alpha = jnp.exp(m - m_new)
        l_new = alpha * l_ref[...]
        acc = alpha * acc_ref[...]
        for s, vp in zip(logits, values):
            pr = jnp.exp(s - m_new)
            l_new = l_new + jnp.sum(pr, axis=1, keepdims=True)
            acc = acc + _dot(pr.astype(BF16), vp.reshape(cols, HEAD_DIM).astype(BF16))
        l_ref[...] = l_new
        acc_ref[...] = acc
        m_ref[...] = m_new

    @pl.when(p < n_main)
    def _():
        update([page_logits(k_refs[r][0, 0], p * pages_per_step + r, False) for r in range(pages_per_step)],
               [v_refs[r][0, 0] for r in range(pages_per_step)])

    @pl.when(p == n_main)
    def _():
        update([page_logits(knew_ref[0], n_main * pages_per_step, True)], [vnew_ref[0]])
        out_ref[0] = (acc_ref[...] / l_ref[...]).astype(BF16)


def _paged_attn(page_table, q_rows, k_pool, v_pool, li, k_new_pages, v_new_pages, pages_per_step,
                neg_cum=None, selected=None):
    nseq, n_pages = page_table.shape
    page, n_heads = k_pool.shape[2], k_pool.shape[3]
    rows = q_rows.shape[1]
    t_new = rows // n_heads
    n_main = n_pages // pages_per_step
    decay = neg_cum is not None
    per_seq = lambda n, p, pt: (n,) + (0,) * 2
    per_seq4 = lambda n, p, pt: (n,) + (0,) * 3
    body = functools.partial(_paged_attn_body, pages_per_step=pages_per_step, n_main=n_main, page=page,
                             t_new=t_new, n_heads=n_heads, decay=decay)
    args = [page_table, q_rows] + [k_pool] * pages_per_step + [v_pool] * pages_per_step + [k_new_pages, v_new_pages]
    in_specs = [pl.BlockSpec((1,) + q_rows.shape[1:], per_seq)]
    in_specs += _page_specs(k_pool, li, pages_per_step, n_pages) + _page_specs(v_pool, li, pages_per_step, n_pages)
    in_specs += [pl.BlockSpec((1,) + k_new_pages.shape[1:], per_seq4), pl.BlockSpec((1,) + v_new_pages.shape[1:], per_seq4)]
    if decay:
        args.append(neg_cum)
        in_specs.append(pl.BlockSpec((1,) + neg_cum.shape[1:], per_seq))
    else:
        key_of_col = jnp.arange(page * n_heads)[None, :] // n_heads
        expand = (key_of_col == jnp.arange(page)[:, None]).astype(BF16)
        args += [selected, expand]
        in_specs += [pl.BlockSpec((1,) + selected.shape[1:], per_seq), _const_spec(expand.shape)]
    scratch = [pltpu.VMEM((rows, 1), F32), pltpu.VMEM((rows, 1), F32), pltpu.VMEM((rows, HEAD_DIM), F32)]
    grid_spec = pltpu.PrefetchScalarGridSpec(
        num_scalar_prefetch=1, grid=(nseq, n_main + 1), in_specs=in_specs,
        out_specs=pl.BlockSpec((1, rows, HEAD_DIM), per_seq), scratch_shapes=scratch)
    return pl.pallas_call(
        body, grid_spec=grid_spec, out_shape=jax.ShapeDtypeStruct((nseq, rows, HEAD_DIM), BF16),
        compiler_params=_params(), name="paged_attn_decay" if decay else "paged_attn_select")(*args)


def _rope_tables(pos):
    half = HEAD_DIM // 2
    inv = ROPE_THETA ** (-jnp.arange(half, dtype=F32) / half)
    ang = pos.astype(F32)[:, None] * inv[None, :]
    cos, sin = jnp.cos(ang), jnp.sin(ang)
    reps = LANES // HEAD_DIM
    return jnp.tile(jnp.concatenate([cos, cos], axis=1), (1, reps)), jnp.tile(jnp.concatenate([-sin, sin], axis=1), (1, reps))


def _block_diag(w):
    g, bw, _ = w.shape
    eye = jnp.eye(g, dtype=w.dtype)
    return (w[:, :, None, :] * eye[:, None, :, None]).reshape(g * bw, g * bw)


def _pad_cols(w, width):
    return jnp.pad(w, ((0, 0), (0, width - w.shape[1])))


def _to_time_major(a):
    return jnp.swapaxes(a, 0, 1).reshape((a.shape[0] * a.shape[1],) + a.shape[2:])


def _from_time_major(a, n):
    return jnp.swapaxes(a.reshape((a.shape[0] // n, n) + a.shape[1:]), 0, 1)


def _head_rows(a, n):
    return _from_time_major(a, n).reshape(n, -1, HEAD_DIM)


def _head_pages(a, n, page):
    pages = _new_pages(a, n, page)
    return pages.reshape(pages.shape[:2] + (-1, HEAD_DIM))


def _new_pages(a, n, page):
    per_seq = _from_time_major(a, n)
    return jnp.pad(per_seq, ((0, 0), (0, page - per_seq.shape[1]), (0, 0)))


def _pick_tile(m, pref):
    t = min(m, pref)
    while m % t:
        t //= 2
    return t


def kernel(x_prompt, x_sample, cache_k_b, cache_v_b, cache_kidx_b, state_conv_a, cache_k_c, cache_v_c, cache_logf_c, state_conv_d, state_lru_h, page_table, g_ff1, w_ff1_gate, w_ff1_up, w_ff1_down, g_mix, g_ff2, w_ff2_gate, w_ff2_up, w_ff2_down, w_in_e, conv_a_w, w_out_e, w_in_o, b_f, conv_d_w, conv_d_b, w_rg_a, b_rg_a, w_rg_x, b_rg_x, lam, w_out_o, g_final):
    batch, seq, d_model = x_prompt.shape
    nseq, t_new, _ = x_sample.shape
    depth = g_ff1.shape[0]
    n_pool, page = cache_k_b.shape[1], cache_k_b.shape[2]
    n_pages = page_table.shape[1]
    past = n_pages * page
    wa = conv_a_w.shape[2]
    hb = w_out_e.shape[1] - wa
    wd = conv_d_w.shape[2]
    hc = w_out_o.shape[1] - wd
    n_heads_c = hc // HEAD_DIM
    pages_per_step = 8 if n_pages % 8 == 0 else 1

    mp, ms = batch * seq, nseq * t_new
    tm_p = _pick_tile(seq, 512)
    xp = x_prompt.reshape(mp, d_model)
    xs = _to_time_major(x_sample)
    cos_p, sin_p = _rope_tables(jnp.tile(jnp.arange(seq), batch))
    cos_s, sin_s = _rope_tables(past + jnp.repeat(jnp.arange(t_new), nseq))
    row2 = lambda v: v.reshape(1, -1)
    bf = lambda w: w.astype(BF16)

    outs_p, outs_s = {}, {}
    for l in range(depth):
        ffn1 = (row2(g_ff1[l]), bf(w_ff1_gate[l]), bf(w_ff1_up[l]), bf(w_ff1_down[l]))
        ffn2 = (row2(g_ff2[l]), bf(w_ff2_gate[l]), bf(w_ff2_up[l]), bf(w_ff2_down[l]))
        xp = _ffn(xp, *ffn1, tm=tm_p)
        xs = _ffn(xs, *ffn1, tm=ms)
        g = row2(g_mix[l])
        if l % 2 == 0:
            e = l // 2
            w = w_in_e[e]
            o_q, o_k, o_v, o_qi, o_kw = 3 * wa, 3 * wa + hb, 3 * wa + 2 * hb, 3 * wa + 3 * hb, 3 * wa + 3 * hb + H_IDX * D_IDX
            weights = (bf(w[:, :o_q]), bf(w[:, o_q:o_k]), bf(w[:, o_k:o_v]), bf(w[:, o_v:o_qi]), bf(w[:, o_qi:o_kw]),
                       bf(_pad_cols(w[:, o_kw:], LANES)))
            w_out = (bf(w_out_e[e][:wa]), bf(w_out_e[e][wa:]))

            hist_p = jnp.zeros((batch, conv_a_w.shape[1] - 1, wa), F32)
            ya, q, k32, kb, v32, vb, qi, kw, ki2, cstate = _proj_even(
                xp, g, weights, cos_p, sin_p, conv_a_w[e], hist_p, tm_p, seq // tm_p, 1)
            yb = _dsa_prompt(qi, kw, ki2, q, kb, vb, batch, seq, _pick_tile(seq, 128), _pick_tile(seq, 512),
                             _pick_tile(seq, 1024))
            outs_p.setdefault("k_b", []).append(k32.reshape(batch, seq, hb // HEAD_DIM, HEAD_DIM))
            outs_p.setdefault("v_b", []).append(v32.reshape(batch, seq, hb // HEAD_DIM, HEAD_DIM))
            outs_p.setdefault("kidx_b", []).append(kw[:, :D_IDX].reshape(batch, seq, D_IDX))
            outs_p.setdefault("conv_a", []).append(cstate)
            mix_p = (ya, yb, *w_out)

            hist_s = _to_time_major(state_conv_a[e])[None]
            ya, q, k32, kb, v32, vb, qi, kw, ki2, cstate = _proj_even(
                xs, g, weights, cos_s, sin_s, conv_a_w[e], hist_s, ms, 1, nseq)
            qi_rows = _from_time_major(qi, nseq).reshape(nseq, t_new * H_IDX, D_IDX)
            w_rows = _from_time_major(kw[:, D_IDX:D_IDX + H_IDX], nseq).reshape(nseq, t_new * H_IDX, 1)
            scores = _idx_scores_sample(page_table, qi_rows, w_rows, cache_kidx_b, e,
                                        _new_pages(kw[:, :D_IDX], nseq, page), pages_per_step)
            length = scores.shape[2]
            n_sel = min(TOPK_MAX, (past + t_new) // 4)
            kc = page * 3 if (length // page) % 3 == 0 else page
            picked = _topk_rows(scores.reshape(ms, length), n_sel, _pick_tile(ms, 32), kc).reshape(nseq, t_new, length)
            yb = _paged_attn(page_table, _head_rows(q, nseq), cache_k_b, cache_v_b, e, _head_pages(k32, nseq, page),
                             _head_pages(v32, nseq, page), pages_per_step, selected=picked)
            yb = _to_time_major(yb.reshape(nseq, t_new, hb))
            outs_s.setdefault("k_b", []).append(_from_time_major(k32, nseq).reshape(nseq, t_new, hb // HEAD_DIM, HEAD_DIM))
            outs_s.setdefault("v_b", []).append(_from_time_major(v32, nseq).reshape(nseq, t_new, hb // HEAD_DIM, HEAD_DIM))
            outs_s.setdefault("kidx_b", []).append(_from_time_major(kw[:, :D_IDX], nseq))
            outs_s.setdefault("conv_a", []).append(_from_time_major(cstate[0], nseq))
            mix_s = (ya, yb, *w_out)
        else:
            o = l // 2
            w = w_in_o[o]
            o_f, o_xd, o_gd = 3 * hc, 3 * hc + n_heads_c, 3 * hc + n_heads_c + wd
            weights = (bf(w[:, :o_f]), bf(_pad_cols(w[:, o_f:o_xd], LANES)), bf(w[:, o_xd:o_gd]), bf(w[:, o_gd:]))
            small = (_pad_cols(row2(b_f[o]), LANES), conv_d_w[o], row2(conv_d_b[o]), bf(_block_diag(w_rg_a[o])),
                     row2(b_rg_a[o]), bf(_block_diag(w_rg_x[o])), row2(b_rg_x[o]), row2(lam[o]))
            w_out = (bf(w_out_o[o][:hc]), bf(w_out_o[o][hc:]))

            hist_p = jnp.zeros((batch, conv_d_w.shape[1] - 1, wd), F32)
            h0_p = jnp.zeros((batch, 1, wd), F32)
            q, k32, kb, v32, vb, logf, yd, cstate, hlast = _proj_odd(
                xp, g, weights, small, hist_p, h0_p, tm_p, seq // tm_p, 1, 0)
            cum_t = _cumsum_heads(logf, batch, seq, _pick_tile(seq, 512))
            yc = _fox_prompt(q, kb, vb, cum_t, batch, seq, _pick_tile(seq, 128), _pick_tile(seq, 1024))
            outs_p.setdefault("k_c", []).append(k32.reshape(batch, seq, n_heads_c, HEAD_DIM))
            outs_p.setdefault("v_c", []).append(v32.reshape(batch, seq, n_heads_c, HEAD_DIM))
            outs_p.setdefault("logf_c", []).append(logf.reshape(batch, seq, n_heads_c))
            outs_p.setdefault("conv_d", []).append(cstate)
            outs_p.setdefault("lru_h", []).append(hlast[:, 0])
            mix_p = (yc, yd, *w_out)

            hist_s = _to_time_major(state_conv_d[o])[None]
            h0_s = state_lru_h[o][None]
            q, k32, kb, v32, vb, logf, yd, cstate, hlast = _proj_odd(
                xs, g, weights, small, hist_s, h0_s, ms, 1, nseq, past)
            f_flat = cache_logf_c.reshape(cache_logf_c.shape[:2] + (1, page * n_heads_c))
            f_new_flat = _new_pages(logf, nseq, page).reshape(nseq, 1, page * n_heads_c)
            neg_cum = _decay_bias_sample(page_table, f_flat, o, f_new_flat, n_heads_c)
            yc = _paged_attn(page_table, _head_rows(q, nseq), cache_k_c, cache_v_c, o, _head_pages(k32, nseq, page),
                             _head_pages(v32, nseq, page), pages_per_step, neg_cum=neg_cum)
            yc = _to_time_major(yc.reshape(nseq, t_new, hc))
            outs_s.setdefault("k_c", []).append(_from_time_major(k32, nseq).reshape(nseq, t_new, n_heads_c, HEAD_DIM))
            outs_s.setdefault("v_c", []).append(_from_time_major(v32, nseq).reshape(nseq, t_new, n_heads_c, HEAD_DIM))
            outs_s.setdefault("logf_c", []).append(_from_time_major(logf, nseq))
            outs_s.setdefault("conv_d", []).append(_from_time_major(cstate[0], nseq))
            outs_s.setdefault("lru_h", []).append(hlast[0])
            mix_s = (yc, yd, *w_out)
        gfin = row2(g_final) if l == depth - 1 else None
        xp = _ffn(xp, *ffn2, tm=tm_p, mix=mix_p, g_final=gfin)
        xs = _ffn(xs, *ffn2, tm=ms, mix=mix_s, g_final=gfin)

    names = ("k_b", "v_b", "kidx_b", "conv_a", "k_c", "v_c", "logf_c", "conv_d", "lru_h")
    st_p = tuple(jnp.stack(outs_p[nm]) for nm in names)
    st_s = tuple(jnp.stack(outs_s[nm]) for nm in names)
    return (xp.reshape(batch, seq, d_model), _from_time_major(xs, nseq)) + st_p + st_s
```

```python
import functools

import jax
import jax.numpy as jnp
import numpy as np
from jax import lax
from jax.experimental import pallas as pl
from jax.experimental.pallas import tpu as pltpu

F32 = jnp.float32
BF16 = jnp.bfloat16

HEAD_DIM = 64
D_IDX = 64
H_IDX = 8
TOPK_MAX = 256
LRU_BLOCKS = 8
RG_C = 8.0
ROPE_THETA = 10000.0
EPS = 1e-6

LANES = 128
VMEM_LIMIT = 56 * 1024 * 1024
NEG_INF = float("-inf")
M_INIT = -1e30
BISECT_ITERS = 400
TIE_CHECK_EVERY = 32
TOPK_ROW_BLOCK = 64

_NT = (((1,), (1,)), ((), ()))


def _params():
    return pltpu.CompilerParams(vmem_limit_bytes=VMEM_LIMIT)


def _const_spec(shape):
    nd = len(shape)
    return pl.BlockSpec(shape, lambda *_: (0,) * nd, pipeline_mode=pl.Buffered(1))


def _dot(a, b):
    return jnp.dot(a, b, preferred_element_type=F32)


def _dot_nt(a, b, precision=None):
    return lax.dot_general(a, b, _NT, preferred_element_type=F32, precision=precision)


def _rmsnorm(x, g):
    ms = jnp.mean(x * x, axis=-1, keepdims=True)
    return x * lax.rsqrt(ms + EPS) * g


def _ffn_body(*refs, n_chunks, fc, has_mix, has_final):
    it = iter(refs)
    x_ref = next(it)
    if has_mix:
        ya_ref, yb_ref, woa_ref, wob_ref = next(it), next(it), next(it), next(it)
    g_ref, wg_ref, wu_ref, wd_ref = next(it), next(it), next(it), next(it)
    if has_final:
        gf_ref = next(it)
    out_ref = next(it)

    x = x_ref[...]
    if has_mix:
        x = x + _dot(ya_ref[...], woa_ref[...]) + _dot(yb_ref[...], wob_ref[...])
    h = _rmsnorm(x, g_ref[...]).astype(BF16)
    acc = None
    for c in range(n_chunks):
        gate = _dot(h, wg_ref[:, c * fc:(c + 1) * fc])
        up = _dot(h, wu_ref[:, c * fc:(c + 1) * fc])
        a = (gate * jax.nn.sigmoid(gate) * up).astype(BF16)
        part = _dot(a, wd_ref[c * fc:(c + 1) * fc, :])
        acc = part if acc is None else acc + part
    x = x + 0.5 * acc
    if has_final:
        x = _rmsnorm(x, gf_ref[...])
    out_ref[...] = x


def _ffn(x, g, wg, wu, wd, tm, mix=None, g_final=None):
    m, d = x.shape
    f = wg.shape[1]
    n_chunks = 2 if (f // 2) % LANES == 0 else 1
    fc = f // n_chunks
    row = lambda i: (i, 0)
    args, specs = [x], [pl.BlockSpec((tm, d), row)]
    if mix is not None:
        ya, yb, woa, wob = mix
        args += [ya, yb, woa, wob]
        specs += [pl.BlockSpec((tm, ya.shape[1]), row), pl.BlockSpec((tm, yb.shape[1]), row),
                  _const_spec(woa.shape), _const_spec(wob.shape)]
    args += [g, wg, wu, wd]
    specs += [_const_spec(g.shape), _const_spec(wg.shape), _const_spec(wu.shape), _const_spec(wd.shape)]
    if g_final is not None:
        args.append(g_final)
        specs.append(_const_spec(g_final.shape))
    body = functools.partial(_ffn_body, n_chunks=n_chunks, fc=fc, has_mix=mix is not None,
                             has_final=g_final is not None)
    return pl.pallas_call(
        body, grid=(m // tm,), in_specs=specs, out_specs=pl.BlockSpec((tm, d), row),
        out_shape=jax.ShapeDtypeStruct((m, d), F32), compiler_params=_params(), name="ffn")(*args)


def _rope_tile(xt, cos, sin_signed):
    lane = lax.broadcasted_iota(jnp.int32, xt.shape, 1)
    first_half = (lane % HEAD_DIM) < (HEAD_DIM // 2)
    partner = jnp.where(first_half, pltpu.roll(xt, LANES - HEAD_DIM // 2, 1), pltpu.roll(xt, HEAD_DIM // 2, 1))
    return xt * cos + partner * sin_signed


def _hist_rows(width, stride):
    need = (width - 1) * stride
    pad = -(-need // 8) * 8
    return need, pad


def _causal_conv(u, w_ref, ubuf, hist_ref, state_ref, first_tile, width, stride):
    tm = u.shape[0]
    need, pad = _hist_rows(width, stride)

    @pl.when(first_tile)
    def _():
        ubuf[pad - need:pad, :] = hist_ref[0]

    @pl.when(jnp.logical_not(first_tile))
    def _():
        ubuf[0:pad, :] = ubuf[tm:tm + pad, :]

    ubuf[pad:pad + tm, :] = u
    y = None
    for j in range(width):
        start = pad - (width - 1 - j) * stride
        term = w_ref[j:j + 1, :] * ubuf[start:start + tm, :]
        y = term if y is None else y + term
    state_ref[0] = ubuf[pad + tm - need:pad + tm, :]
    return y


def _proj_even_body(x_ref, g_ref, wc_ref, wq_ref, wk_ref, wv_ref, wqi_ref, wkw_ref, cos_ref, sin_ref,
                    cw_ref, hist_ref,
                    ya_ref, q_ref, k32_ref, kb_ref, v32_ref, vb_ref, qi_ref, kw_ref, ki2_ref, state_ref,
                    ubuf, *, tiles_per_seq, stride, wa):
    first_tile = (pl.program_id(0) % tiles_per_seq) == 0
    h = _rmsnorm(x_ref[...], g_ref[...]).astype(BF16)
    cos = cos_ref[...]
    sin = sin_ref[...]
    scale = HEAD_DIM ** -0.5

    zc = _dot(h, wc_ref[...])
    xin, gb, gc = zc[:, :wa], zc[:, wa:2 * wa], zc[:, 2 * wa:]
    yconv = _causal_conv(gc * xin, cw_ref, ubuf, hist_ref, state_ref, first_tile, cw_ref.shape[0], stride)
    ya_ref[...] = (gb * yconv).astype(BF16)

    n_tiles = wq_ref.shape[1] // LANES
    q = _dot(h, wq_ref[...])
    k = _dot(h, wk_ref[...])
    qi = _dot(h, wqi_ref[...])
    for t in range(n_tiles):
        sl = slice(t * LANES, (t + 1) * LANES)
        q_ref[:, sl] = (_rope_tile(q[:, sl], cos, sin) * scale).astype(BF16)
        kt = _rope_tile(k[:, sl], cos, sin)
        k32_ref[:, sl] = kt
        kb_ref[:, sl] = kt.astype(BF16)
        qi_ref[:, sl] = (_rope_tile(qi[:, sl], cos, sin) * (D_IDX ** -0.5)).astype(BF16)
    v = _dot(h, wv_ref[...])
    v32_ref[...] = v
    vb_ref[...] = v.astype(BF16)

    kw = _dot(h, wkw_ref[...])
    lane = lax.broadcasted_iota(jnp.int32, kw.shape, 1)
    is_ki = lane < D_IDX
    kw = jnp.where(is_ki, _rope_tile(kw, cos, sin), kw)
    kw_ref[...] = kw
    ki2_ref[...] = jnp.where(is_ki, kw, pltpu.roll(kw, D_IDX, 1)).astype(BF16)


def _proj_even(x, g, w, cos, sin, conv_w, hist, tm, tiles_per_seq, stride):
    m, d = x.shape
    wc, wq, wk, wv, wqi, wkw = w
    wa = wc.shape[1] // 3
    hb = wq.shape[1]
    width = conv_w.shape[0]
    need, pad = _hist_rows(width, stride)
    nb = hist.shape[0]
    row = lambda i: (i, 0)
    seq = lambda i: (i // tiles_per_seq, 0, 0)
    in_specs = [pl.BlockSpec((tm, d), row), _const_spec(g.shape)]
    in_specs += [_const_spec(a.shape) for a in (wc, wq, wk, wv, wqi, wkw)]
    in_specs += [pl.BlockSpec((tm, LANES), row), pl.BlockSpec((tm, LANES), row), _const_spec(conv_w.shape),
                 pl.BlockSpec((1, need, wa), seq)]
    out_shape = [jax.ShapeDtypeStruct((m, wa), BF16), jax.ShapeDtypeStruct((m, hb), BF16),
                 jax.ShapeDtypeStruct((m, hb), F32), jax.ShapeDtypeStruct((m, hb), BF16),
                 jax.ShapeDtypeStruct((m, hb), F32), jax.ShapeDtypeStruct((m, hb), BF16),
                 jax.ShapeDtypeStruct((m, hb), BF16), jax.ShapeDtypeStruct((m, LANES), F32),
                 jax.ShapeDtypeStruct((m, LANES), BF16), jax.ShapeDtypeStruct((nb, need, wa), F32)]
    out_specs = [pl.BlockSpec((tm, wa), row)] + [pl.BlockSpec((tm, hb), row)] * 6
    out_specs += [pl.BlockSpec((tm, LANES), row), pl.BlockSpec((tm, LANES), row), pl.BlockSpec((1, need, wa), seq)]
    body = functools.partial(_proj_even_body, tiles_per_seq=tiles_per_seq, stride=stride, wa=wa)
    return pl.pallas_call(
        body, grid=(m // tm,), in_specs=in_specs, out_specs=out_specs, out_shape=out_shape,
        scratch_shapes=[pltpu.VMEM((pad + tm, wa), F32)], compiler_params=_params(), name="proj_even")(
            x, g, wc, wq, wk, wv, wqi, wkw, cos, sin, conv_w, hist)


def _shift_rows(x, d, fill):
    row = lax.broadcasted_iota(jnp.int32, x.shape, 0)
    return jnp.where(row >= d, pltpu.roll(x, d, 0), fill)


def _proj_odd_body(x_ref, g_ref, wqkv_ref, wf_ref, wxd_ref, wgd_ref, bf_ref, cw_ref, cb_ref, wra_ref, bra_ref,
                   wrx_ref, brx_ref, lam_ref, hist_ref, h0_ref,
                   q_ref, k32_ref, kb_ref, v32_ref, vb_ref, logf_ref, yd_ref, state_ref, hlast_ref,
                   ubuf, hcar, *, tiles_per_seq, stride, pos_base, hc):
    tile_in_seq = pl.program_id(0) % tiles_per_seq
    first_tile = tile_in_seq == 0
    tm = x_ref.shape[0]
    h = _rmsnorm(x_ref[...], g_ref[...]).astype(BF16)

    z = _dot(h, wqkv_ref[...])
    q_ref[...] = (z[:, :hc] * (HEAD_DIM ** -0.5)).astype(BF16)
    k = z[:, hc:2 * hc]
    v = z[:, 2 * hc:]
    k32_ref[...] = k
    kb_ref[...] = k.astype(BF16)
    v32_ref[...] = v
    vb_ref[...] = v.astype(BF16)

    zf = _dot(h, wf_ref[...]) + bf_ref[...]
    logf = jnp.minimum(zf, 0.0) - jnp.log1p(jnp.exp(-jnp.abs(zf)))
    logf_ref[...] = logf[:, :logf_ref.shape[1]]

    xd = _dot(h, wxd_ref[...])
    gd = _dot(h, wgd_ref[...])
    xc = _causal_conv(xd, cw_ref, ubuf, hist_ref, state_ref, first_tile, cw_ref.shape[0], stride) + cb_ref[...]
    xcb = xc.astype(BF16)
    r = jax.nn.sigmoid(_dot(xcb, wra_ref[...]) + bra_ref[...])
    ig = jax.nn.sigmoid(_dot(xcb, wrx_ref[...]) + brx_ref[...])
    lam = lam_ref[...]
    softplus_neg_lam = jnp.maximum(-lam, 0.0) + jnp.log1p(jnp.exp(-jnp.abs(lam)))
    log_a = -RG_C * r * softplus_neg_lam
    a = jnp.exp(log_a)
    row = lax.broadcasted_iota(jnp.int32, (tm, 1), 0)
    pos = pos_base + tile_in_seq * (tm // stride) + row // stride
    mult = jnp.where(pos == 0, 1.0, jnp.sqrt(1.0 - jnp.exp(2.0 * log_a)))
    b = mult * ig * xc

    d = stride
    while d < tm:
        a_prev = _shift_rows(a, d, 1.0)
        b_prev = _shift_rows(b, d, 0.0)
        b = a * b_prev + b
        a = a * a_prev
        d *= 2

    @pl.when(first_tile)
    def _():
        hcar[...] = h0_ref[0]

    carry = hcar[...]
    if stride == 1:
        carry_rows = carry
    else:
        carry_rows = jnp.concatenate([carry] * (tm // stride), axis=0)
    hs = a * carry_rows + b
    hcar[...] = hs[tm - stride:, :]
    hlast_ref[0] = hs[tm - stride:, :]
    yd_ref[...] = (hs * jax.nn.gelu(gd)).astype(BF16)


def _proj_odd(x, g, w, small, hist, h0, tm, tiles_per_seq, stride, pos_base):
    m, d = x.shape
    wqkv, wf, wxd, wgd = w
    bf, conv_w, conv_b, wra, bra, wrx, brx, lam = small
    hc = wqkv.shape[1] // 3
    wd_ = wxd.shape[1]
    n_heads = hc // HEAD_DIM
    width = conv_w.shape[0]
    need, pad = _hist_rows(width, stride)
    nb = hist.shape[0]
    row = lambda i: (i, 0)
    seq = lambda i: (i // tiles_per_seq, 0, 0)
    ins = [x, g, wqkv, wf, wxd, wgd, bf, conv_w, conv_b, wra, bra, wrx, brx, lam, hist, h0]
    in_specs = [pl.BlockSpec((tm, d), row)] + [_const_spec(a.shape) for a in ins[1:14]]
    in_specs += [pl.BlockSpec((1, need, wd_), seq), pl.BlockSpec((1, stride, wd_), seq)]
    out_shape = [jax.ShapeDtypeStruct((m, hc), BF16), jax.ShapeDtypeStruct((m, hc), F32),
                 jax.ShapeDtypeStruct((m, hc), BF16), jax.ShapeDtypeStruct((m, hc), F32),
                 jax.ShapeDtypeStruct((m, hc), BF16), jax.ShapeDtypeStruct((m, n_heads), F32),
                 jax.ShapeDtypeStruct((m, wd_), BF16), jax.ShapeDtypeStruct((nb, need, wd_), F32),
                 jax.ShapeDtypeStruct((nb, stride, wd_), F32)]
    out_specs = [pl.BlockSpec((tm, hc), row)] * 5 + [pl.BlockSpec((tm, n_heads), row), pl.BlockSpec((tm, wd_), row),
                                                      pl.BlockSpec((1, need, wd_), seq),
                                                      pl.BlockSpec((1, stride, wd_), seq)]
    body = functools.partial(_proj_odd_body, tiles_per_seq=tiles_per_seq, stride=stride, pos_base=pos_base, hc=hc)
    return pl.pallas_call(
        body, grid=(m // tm,), in_specs=in_specs, out_specs=out_specs, out_shape=out_shape,
        scratch_shapes=[pltpu.VMEM((pad + tm, wd_), F32), pltpu.VMEM((stride, wd_), F32)],
        compiler_params=_params(), name="proj_odd")(*ins)


def _cumsum_lanes(lf, carry):
    t, nh = lf.shape
    eye = (lax.broadcasted_iota(jnp.int32, (nh, nh), 0) == lax.broadcasted_iota(jnp.int32, (nh, nh), 1)).astype(F32)
    lf_t = _dot_nt(eye, lf, precision=lax.Precision.HIGHEST)
    tri = (lax.broadcasted_iota(jnp.int32, (t, t), 1) <= lax.broadcasted_iota(jnp.int32, (t, t), 0)).astype(F32)
    return _dot_nt(lf_t, tri, precision=lax.Precision.HIGHEST) + carry


def _cumsum_body(lf_ref, out_ref, carry):
    @pl.when(pl.program_id(1) == 0)
    def _():
        carry[...] = jnp.zeros_like(carry)

    cum = _cumsum_lanes(lf_ref[...], carry[:, 0:1])
    out_ref[0] = cum
    carry[...] = jnp.broadcast_to(cum[:, cum.shape[1] - 1:], carry.shape)


def _cumsum_heads(logf, n, s, tc):
    nh = logf.shape[1]
    tiles = s // tc
    return pl.pallas_call(
        _cumsum_body, grid=(n, tiles),
        in_specs=[pl.BlockSpec((tc, nh), lambda b, i: (b * tiles + i, 0))],
        out_specs=pl.BlockSpec((1, nh, tc), lambda b, i: (b, 0, i)),
        out_shape=jax.ShapeDtypeStruct((n, nh, s), F32),
        scratch_shapes=[pltpu.VMEM((nh, LANES), F32)], compiler_params=_params(), name="cumsum")(logf)


def _pair_mask(shape, parity):
    lane = lax.broadcasted_iota(jnp.int32, shape, 1)
    return (lane >= HEAD_DIM) == (parity == 1)


def _stack_pair_queries(q_ref, qs_ref, tq):
    for j in range(q_ref.shape[1] // LANES):
        q_pair = q_ref[:, j * LANES:(j + 1) * LANES]
        zero = jnp.zeros_like(q_pair)
        qs_ref[j, 0:tq, :] = jnp.where(_pair_mask(q_pair.shape, 0), q_pair, zero)
        qs_ref[j, tq:2 * tq, :] = jnp.where(_pair_mask(q_pair.shape, 1), q_pair, zero)


def _attend_pairs(qs_ref, k_ref, v_ref, out_ref, s_buf, stat_ref, lsum_ref, acc_ref, *, tq, kb, n_plain, n_chunks,
                  logits_fn):
    tiles = kb // LANES

    def pair(j, _):
        col = pl.ds(pl.multiple_of(j * LANES, LANES), LANES)
        stat_ref[...] = jnp.full(stat_ref.shape, NEG_INF, F32)

        def pass1(c, _, masked):
            rows = pl.ds(pl.multiple_of(c * kb, kb), kb)
            s = logits_fn(j, c, _dot_nt(qs_ref[j], k_ref[rows, col]), masked)
            s_buf[:, rows] = s
            mt = s[:, 0:LANES]
            for t in range(1, tiles):
                mt = jnp.maximum(mt, s[:, t * LANES:(t + 1) * LANES])
            stat_ref[...] = jnp.maximum(stat_ref[...], mt)
            return 0

        lax.fori_loop(0, n_plain, functools.partial(pass1, masked=False), 0)
        lax.fori_loop(n_plain, n_chunks, functools.partial(pass1, masked=True), 0)
        m = jnp.max(stat_ref[...], axis=1, keepdims=True)
        stat_ref[...] = jnp.broadcast_to(m, stat_ref.shape)
        lsum_ref[...] = jnp.zeros(lsum_ref.shape, F32)
        acc_ref[...] = jnp.zeros(acc_ref.shape, F32)

        def pass2(c, _):
            rows = pl.ds(pl.multiple_of(c * kb, kb), kb)
            m_rows = jnp.concatenate([stat_ref[...]] * tiles, axis=1)
            p = jnp.exp(s_buf[:, rows] - m_rows)
            part = p[:, 0:LANES]
            for t in range(1, tiles):
                part = part + p[:, t * LANES:(t + 1) * LANES]
            lsum_ref[...] += part
            acc_ref[...] += _dot(p.astype(BF16), v_ref[rows, col])
            return 0

        lax.fori_loop(0, n_chunks, pass2, 0)
        o = acc_ref[...] / jnp.sum(lsum_ref[...], axis=1, keepdims=True)
        even, odd = o[0:tq], o[tq:2 * tq]
        out_ref[:, col] = jnp.where(_pair_mask(even.shape, 1), odd, even).astype(BF16)
        return 0

    lax.fori_loop(0, qs_ref.shape[0], pair, 0)


def _pair_scratch(n_pairs, tq, s):
    return [pltpu.VMEM((n_pairs, 2 * tq, LANES), BF16), pltpu.VMEM((2 * tq, s), F32),
            pltpu.VMEM((2 * tq, LANES), F32), pltpu.VMEM((2 * tq, LANES), F32), pltpu.VMEM((2 * tq, LANES), F32)]


def _fox_body(q_ref, k_ref, v_ref, cum_ref, out_ref, qs_ref, s_buf, stat_ref, lsum_ref, acc_ref, *, tq, kc):
    i = pl.program_id(1)
    n_full = (i * tq) // kc
    n_tot = ((i + 1) * tq + kc - 1) // kc
    _stack_pair_queries(q_ref, qs_ref, tq)

    def logits(j, c, raw, masked):
        rows = pl.ds(pl.multiple_of(c * kc, kc), kc)
        halves = []
        for parity in (0, 1):
            sh = raw[parity * tq:(parity + 1) * tq] - cum_ref[0, pl.ds(2 * j + parity, 1), rows]
            if masked:
                qpos = i * tq + lax.broadcasted_iota(jnp.int32, (tq, kc), 0)
                kpos = c * kc + lax.broadcasted_iota(jnp.int32, (tq, kc), 1)
                sh = jnp.where(kpos <= qpos, sh, NEG_INF)
            halves.append(sh)
        return jnp.concatenate(halves, axis=0)

    _attend_pairs(qs_ref, k_ref, v_ref, out_ref, s_buf, stat_ref, lsum_ref, acc_ref, tq=tq, kb=kc,
                  n_plain=n_full, n_chunks=n_tot, logits_fn=logits)


def _fox_prompt(q, kb, vb, cum_t, n, s, tq, kc):
    hc = q.shape[1]
    nh = cum_t.shape[1]
    nq = s // tq
    body = functools.partial(_fox_body, tq=tq, kc=kc)
    scratch = _pair_scratch(hc // LANES, tq, s)
    one = pl.Buffered(1)
    return pl.pallas_call(
        body, grid=(n, nq),
        in_specs=[pl.BlockSpec((tq, hc), lambda b, i: (b * nq + i, 0)),
                  pl.BlockSpec((s, hc), lambda b, i: (b, 0), pipeline_mode=one),
                  pl.BlockSpec((s, hc), lambda b, i: (b, 0), pipeline_mode=one),
                  pl.BlockSpec((1, nh, s), lambda b, i: (b, 0, 0), pipeline_mode=one)],
        out_specs=pl.BlockSpec((tq, hc), lambda b, i: (b * nq + i, 0)),
        out_shape=jax.ShapeDtypeStruct((n * s, hc), BF16), scratch_shapes=scratch,
        compiler_params=_params(), name="fox_prompt")(
            q, kb, vb, cum_t)


def _topk_to_bias(sc_ref, nkc, kc, n_sel, on=0.0, off=NEG_INF):
    rows = sc_ref.shape[0]
    tiles = kc // LANES
    pos_inf = float("inf")

    rb = min(rows, TOPK_ROW_BLOCK)

    def lane_fold(fn, init):
        parts = []
        for r0 in range(0, rows, rb):
            rs = slice(r0, r0 + rb)

            def body(c, acc, rs=rs):
                base = pl.multiple_of(c * kc, kc)
                for t in range(tiles):
                    acc = fn(acc, sc_ref[rs, pl.ds(base + t * LANES, LANES)], rs)
                return acc
            parts.append(lax.fori_loop(0, nkc, body, jnp.full((rb, LANES), init, F32)))
        return jnp.concatenate(parts, axis=0)

    def count(pred_fn):
        part = lane_fold(lambda acc, x, rs: acc + jnp.where(pred_fn(x, rs), 1.0, 0.0), 0.0)
        return jnp.sum(part, axis=1, keepdims=True)

    mx = jnp.max(lane_fold(lambda acc, x, rs: jnp.maximum(acc, x), NEG_INF), axis=1, keepdims=True)
    mn = jnp.min(lane_fold(lambda acc, x, rs: jnp.minimum(acc, jnp.where(x == NEG_INF, pos_inf, x)), pos_inf),
                 axis=1, keepdims=True)
    n_visible = count(lambda x, rs: x > NEG_INF)

    def live_rows(lo, hi, cnt):
        mid = lo + 0.5 * (hi - lo)
        splittable = jnp.logical_and(mid > lo, mid < hi)
        return jnp.where(jnp.logical_and(cnt > n_sel, splittable), 1.0, 0.0)

    def halve(lo, hi, cnt):
        mid = lo + 0.5 * (hi - lo)
        midb = jnp.broadcast_to(mid, (rows, LANES))
        c_mid = count(lambda x, rs: x >= midb[rs])
        enough = c_mid >= n_sel
        return jnp.where(enough, mid, lo), jnp.where(enough, hi, mid), jnp.where(enough, c_mid, cnt)

    def close_tied(lo, hi, cnt):
        lob = jnp.broadcast_to(lo, (rows, LANES))
        low = jnp.min(lane_fold(lambda acc, x, rs: jnp.minimum(acc, jnp.where(x >= lob[rs], x, pos_inf)), pos_inf),
                      axis=1, keepdims=True)
        lowb = jnp.broadcast_to(low, (rows, LANES))
        tied = count(lambda x, rs: x > lowb[rs]) < n_sel
        return lo, jnp.where(tied, lo, hi), cnt

    def bisect(carry):
        it, _, lo, hi, cnt = carry
        lo, hi, cnt = lax.cond(it % TIE_CHECK_EVERY == TIE_CHECK_EVERY - 1, close_tied, halve, lo, hi, cnt)
        return it + 1, jnp.max(live_rows(lo, hi, cnt)), lo, hi, cnt

    def unfinished(carry):
        return jnp.logical_and(carry[0] < BISECT_ITERS, carry[1] > 0.0)

    _, _, lo, _, cnt = lax.while_loop(
        unfinished, bisect, (jnp.int32(0), jnp.max(live_rows(mn, mx, n_visible)), mn, mx, n_visible))
    lob = jnp.broadcast_to(lo, (rows, LANES))
    has_ties = jnp.max(cnt) > n_sel

    @pl.when(jnp.logical_not(has_ties))
    def _():
        def body(c, _):
            base = pl.multiple_of(c * kc, kc)
            for t in range(tiles):
                col = pl.ds(base + t * LANES, LANES)
                sc_ref[:, col] = jnp.where(sc_ref[:, col] >= lob, on, off)
            return 0
        lax.fori_loop(0, nkc, body, 0)

    @pl.when(has_ties)
    def _():
        thr = jnp.min(lane_fold(lambda acc, x, rs: jnp.minimum(acc, jnp.where(x >= lob[rs], x, pos_inf)), pos_inf),
                      axis=1, keepdims=True)
        thrb = jnp.broadcast_to(thr, (rows, LANES))
        need = n_sel - count(lambda x, rs: x > thrb[rs])
        upper = (lax.broadcasted_iota(jnp.int32, (kc, kc), 0) <= lax.broadcasted_iota(jnp.int32, (kc, kc), 1))
        upper = jnp.where(upper, 1.0, 0.0).astype(BF16)

        def body(c, seen):
            col = pl.ds(pl.multiple_of(c * kc, kc), kc)
            x = sc_ref[:, col]
            eq = jnp.where(x == thr, 1.0, 0.0)
            rank = _dot(eq.astype(BF16), upper) + seen
            keep_eq = jnp.where(rank <= need, eq, 0.0)
            sc_ref[:, col] = jnp.where(x > thr, on, jnp.where(keep_eq > 0.5, on, off))
            return seen + jnp.sum(eq, axis=1, keepdims=True)
        lax.fori_loop(0, nkc, body, jnp.zeros((rows, 1), F32))


def _dsa_body(qi_ref, kw_ref, ki2_ref, q_ref, k_ref, v_ref, out_ref, sc_ref, wb_ref, qis_ref,
              qs_ref, s_buf, stat_ref, lsum_ref, acc_ref, *, tq, kc, ks, kb, n_sel):
    i = pl.program_id(1)
    nkc = ((i + 1) * tq + kc - 1) // kc

    kw = kw_ref[...]
    for h in range(H_IDX):
        wb_ref[h] = jnp.broadcast_to(kw[:, D_IDX + h:D_IDX + h + 1] * (H_IDX ** -0.5), (tq, LANES))
        j, parity = divmod(h, 2)
        qi_pair = qi_ref[:, j * LANES:(j + 1) * LANES]
        qis_ref[h * tq:(h + 1) * tq, :] = jnp.where(_pair_mask(qi_pair.shape, parity), qi_pair, jnp.zeros_like(qi_pair))
    qpos = i * tq + lax.broadcasted_iota(jnp.int32, (tq, LANES), 0)
    kiota = lax.broadcasted_iota(jnp.int32, (tq, LANES), 1)

    def score_chunk(c, _):
        base = pl.multiple_of(c * ks, ks)
        d = _dot_nt(qis_ref[...], ki2_ref[pl.ds(base, ks), :])
        for t in range(ks // LANES):
            acc = None
            for h in range(H_IDX):
                term = jnp.maximum(d[h * tq:(h + 1) * tq, t * LANES:(t + 1) * LANES], 0.0) * wb_ref[h]
                acc = term if acc is None else acc + term
            kpos = base + t * LANES + kiota
            sc_ref[:, pl.ds(base + t * LANES, LANES)] = jnp.where(kpos <= qpos, acc, NEG_INF)
        return 0

    n_big = ((i + 1) * tq + kb - 1) // kb
    n_score = ((i + 1) * tq + ks - 1) // ks
    lax.fori_loop(0, n_score, score_chunk, 0)

    def fill_hidden(c, _):
        sc_ref[:, pl.ds(pl.multiple_of(c * ks, ks), ks)] = jnp.full((tq, ks), NEG_INF, F32)
        return 0

    lax.fori_loop(n_score, n_big * (kb // ks), fill_hidden, 0)
    _topk_to_bias(sc_ref, nkc, kc, n_sel)

    _stack_pair_queries(q_ref, qs_ref, tq)

    def logits(j, c, raw, masked):
        bias = sc_ref[:, pl.ds(pl.multiple_of(c * kb, kb), kb)]
        return raw + jnp.concatenate([bias, bias], axis=0)

    _attend_pairs(qs_ref, k_ref, v_ref, out_ref, s_buf, stat_ref, lsum_ref, acc_ref, tq=tq, kb=kb,
                  n_plain=n_big, n_chunks=n_big, logits_fn=logits)


def _dsa_prompt(qi, kw, ki2, q, kb, vb, n, s, tq, kc, kbig):
    hb = q.shape[1]
    nq = s // tq
    n_sel = min(TOPK_MAX, s // 4)
    ks = min(kc, 2 * LANES)
    body = functools.partial(_dsa_body, tq=tq, kc=kc, ks=ks, kb=kbig, n_sel=n_sel)
    scratch = [pltpu.VMEM((tq, s), F32), pltpu.VMEM((H_IDX, tq, LANES), F32), pltpu.VMEM((H_IDX * tq, LANES), BF16)]
    scratch += _pair_scratch(hb // LANES, tq, s)
    one = pl.Buffered(1)
    qrow = lambda b, i: (b * nq + i, 0)
    return pl.pallas_call(
        body, grid=(n, nq),
        in_specs=[pl.BlockSpec((tq, hb), qrow), pl.BlockSpec((tq, LANES), qrow),
                  pl.BlockSpec((s, LANES), lambda b, i: (b, 0), pipeline_mode=one),
                  pl.BlockSpec((tq, hb), qrow),
                  pl.BlockSpec((s, hb), lambda b, i: (b, 0), pipeline_mode=one),
                  pl.BlockSpec((s, hb), lambda b, i: (b, 0), pipeline_mode=one)],
        out_specs=pl.BlockSpec((tq, hb), qrow),
        out_shape=jax.ShapeDtypeStruct((n * s, hb), BF16),
        scratch_shapes=scratch, compiler_params=_params(), name="dsa_prompt")(qi, kw, ki2, q, kb, vb)


def _page_specs(pool, li, pages_per_step, n_pages):
    def spec(r):
        def index(n, p, pt):
            return (li, pt[n, jnp.minimum(p * pages_per_step + r, n_pages - 1)]) + (0,) * (pool.ndim - 2)
        return pl.BlockSpec((1, 1) + pool.shape[2:], index)
    return [spec(r) for r in range(pages_per_step)]


def _idx_scores_body(pt_ref, qi_ref, w_ref, *rest, pages_per_step, n_main, page, t_new):
    page_refs = rest[:pages_per_step]
    new_ref, out_ref = rest[pages_per_step], rest[pages_per_step + 1]
    p = pl.program_id(1)
    qi = qi_ref[0]
    w = w_ref[0] * (H_IDX ** -0.5)

    def scores(ki_t):
        d = _dot(qi, ki_t.astype(BF16))
        contrib = jnp.maximum(d, 0.0) * w
        return jnp.sum(contrib.reshape(t_new, H_IDX, page), axis=1)

    @pl.when(p < n_main)
    def _():
        for r in range(pages_per_step):
            off = pl.multiple_of((p * pages_per_step + r) * page, page)
            out_ref[0, :, pl.ds(off, page)] = scores(page_refs[r][0, 0])

    @pl.when(p == n_main)
    def _():
        sc = scores(new_ref[0])
        t = lax.broadcasted_iota(jnp.int32, sc.shape, 0)
        j = lax.broadcasted_iota(jnp.int32, sc.shape, 1)
        out_ref[0, :, n_main * pages_per_step * page:] = jnp.where(j <= t, sc, NEG_INF)


def _idx_scores_sample(page_table, qi_rows, w_rows, ki_pool, li, ki_new_pages, pages_per_step):
    nseq, n_pages = page_table.shape
    page = ki_pool.shape[3]
    t_new = qi_rows.shape[1] // H_IDX
    n_main = n_pages // pages_per_step
    length = (n_pages + 1) * page
    body = functools.partial(_idx_scores_body, pages_per_step=pages_per_step, n_main=n_main, page=page, t_new=t_new)
    per_seq = lambda n, p, pt: (n, 0, 0)
    grid_spec = pltpu.PrefetchScalarGridSpec(
        num_scalar_prefetch=1, grid=(nseq, n_main + 1),
        in_specs=[pl.BlockSpec((1,) + qi_rows.shape[1:], per_seq), pl.BlockSpec((1,) + w_rows.shape[1:], per_seq)]
        + _page_specs(ki_pool, li, pages_per_step, n_pages)
        + [pl.BlockSpec((1,) + ki_new_pages.shape[1:], per_seq)],
        out_specs=pl.BlockSpec((1, t_new, length), per_seq))
    return pl.pallas_call(
        body, grid_spec=grid_spec, out_shape=jax.ShapeDtypeStruct((nseq, t_new, length), F32),
        compiler_params=_params(), name="idx_scores_sample")(
            page_table, qi_rows, w_rows, *([ki_pool] * pages_per_step), ki_new_pages)


def _topk_rows_body(sc_ref, out_ref, *, kc, n_sel):
    out_ref[...] = sc_ref[...]
    _topk_to_bias(out_ref, out_ref.shape[1] // kc, kc, n_sel, on=1.0, off=0.0)


def _topk_rows(scores, n_sel, rows, kc):
    r, length = scores.shape
    body = functools.partial(_topk_rows_body, kc=kc, n_sel=n_sel)
    return pl.pallas_call(
        body, grid=(r // rows,), in_specs=[pl.BlockSpec((rows, length), lambda i: (i, 0))],
        out_specs=pl.BlockSpec((rows, length), lambda i: (i, 0)),
        out_shape=jax.ShapeDtypeStruct((r, length), F32), compiler_params=_params(), name="topk_rows")(scores)


def _split3(x):
    hi = x.astype(BF16)
    r1 = x - hi.astype(F32)
    mid = r1.astype(BF16)
    lo = (r1 - mid.astype(F32)).astype(BF16)
    return hi, mid, lo


def _decay_bias_body(pt_ref, *rest, pages_per_step, n_pages, n_heads):
    page_refs = rest[:pages_per_step]
    new_ref, out_ref, lf_ref = rest[pages_per_step:]
    p = pl.program_id(1)
    n_rows, page = lf_ref.shape

    @pl.when(p == 0)
    def _():
        lf_ref[n_pages * n_heads:, :] = new_ref[0]

    for r in range(pages_per_step):
        row0 = pl.multiple_of((p * pages_per_step + r) * n_heads, n_heads)
        lf_ref[pl.ds(row0, n_heads), :] = page_refs[r][0, 0]

    @pl.when(p == pl.num_programs(1) - 1)
    def _():
        upto = jnp.where(lax.broadcasted_iota(jnp.int32, (page, page), 0)
                         <= lax.broadcasted_iota(jnp.int32, (page, page), 1), 1.0, 0.0).astype(BF16)
        src = lax.broadcasted_iota(jnp.int32, (n_rows, n_rows), 1)
        dst = lax.broadcasted_iota(jnp.int32, (n_rows, n_rows), 0)
        earlier_pages = jnp.where((src % n_heads) == (dst % n_heads),
                                  jnp.where(src // n_heads < dst // n_heads, 1.0, 0.0), 0.0).astype(BF16)
        cum = sum(_dot(x, upto) for x in _split3(lf_ref[...]))
        totals = jnp.broadcast_to(cum[:, page - 1:], cum.shape)
        cum = cum + sum(_dot(earlier_pages, x) for x in _split3(totals))
        out_ref[0] = -cum


def _decay_bias_sample(page_table, f_pool_t, li, f_new_t):
    nseq, n_pages = page_table.shape
    n_heads, page = f_pool_t.shape[2:]
    pages_per_step = 8 if n_pages % 8 == 0 else 1
    n_rows = (n_pages + 1) * n_heads
    per_seq = lambda n, p, pt: (n, 0, 0)
    body = functools.partial(_decay_bias_body, pages_per_step=pages_per_step, n_pages=n_pages, n_heads=n_heads)
    grid_spec = pltpu.PrefetchScalarGridSpec(
        num_scalar_prefetch=1, grid=(nseq, n_pages // pages_per_step),
        in_specs=_page_specs(f_pool_t, li, pages_per_step, n_pages) + [pl.BlockSpec((1, n_heads, page), per_seq)],
        out_specs=pl.BlockSpec((1, n_rows, page), per_seq), scratch_shapes=[pltpu.VMEM((n_rows, page), F32)])
    return pl.pallas_call(
        body, grid_spec=grid_spec, out_shape=jax.ShapeDtypeStruct((nseq, n_rows, page), F32),
        compiler_params=_params(), name="decay_bias_sample")(
            page_table, *([f_pool_t] * pages_per_step), f_new_t)


def _paged_attn_body(pt_ref, q_ref, *rest, pages_per_step, n_main, page, t_new, n_heads, decay):
    it = iter(rest)
    k_refs = [next(it) for _ in range(pages_per_step)]
    v_refs = [next(it) for _ in range(pages_per_step)]
    knew_ref, vnew_ref = next(it), next(it)
    side_ref = next(it)
    out_ref = next(it)
    m_ref, l_ref, acc_ref = next(it), next(it), next(it)
    p = pl.program_id(1)
    rows = t_new * n_heads
    q = q_ref[0]

    @pl.when(p == 0)
    def _():
        m_ref[...] = jnp.full(m_ref.shape, M_INIT, F32)
        l_ref[...] = jnp.zeros(l_ref.shape, F32)
        acc_ref[...] = jnp.zeros(acc_ref.shape, F32)

    def page_logits(k_t, page_idx, new_page):
        s = _dot(q, k_t.astype(BF16))
        if decay:
            neg_cum = side_ref[0, pl.ds(pl.multiple_of(page_idx * n_heads, n_heads), n_heads), :]
            s = s + jnp.concatenate([neg_cum] * t_new, axis=0)
            if new_page:
                t = lax.broadcasted_iota(jnp.int32, s.shape, 0) // n_heads
                s = jnp.where(lax.broadcasted_iota(jnp.int32, s.shape, 1) <= t, s, NEG_INF)
            return s
        sel = side_ref[0, :, pl.ds(pl.multiple_of(page_idx * page, page), page)]
        sel_rows = jnp.broadcast_to(sel[:, None, :], (t_new, n_heads, page)).reshape(rows, page)
        return jnp.where(sel_rows > 0.5, s, NEG_INF)

    def update(logits, values_t):
        m = m_ref[...]
        m_new = m
        for s in logits:
            m_new = jnp.maximum(m_new, jnp.max(s, axis=1, keepdims=True))
        alpha = jnp.exp(m - m_new)
        l_new = alpha * l_ref[...]
        acc = alpha * acc_ref[...]
        for s, v_t in zip(logits, values_t):
            pr = jnp.exp(s - m_new)
            l_new = l_new + jnp.sum(pr, axis=1, keepdims=True)
            acc = acc + _dot_nt(pr.astype(BF16), v_t.astype(BF16))
        l_ref[...] = l_new
        acc_ref[...] = acc
        m_ref[...] = m_new

    @pl.when(p < n_main)
    def _():
        update([page_logits(k_refs[r][0, 0], p * pages_per_step + r, False) for r in range(pages_per_step)],
               [v_refs[r][0, 0] for r in range(pages_per_step)])

    @pl.when(p == n_main)
    def _():
        update([page_logits(knew_ref[0], n_main * pages_per_step, True)], [vnew_ref[0]])
        o = acc_ref[...] / l_ref[...]
        head_of_row = lax.broadcasted_iota(jnp.int32, o.shape, 0) % n_heads
        head_of_lane = lax.broadcasted_iota(jnp.int32, o.shape, 1) // HEAD_DIM
        o = jnp.where(head_of_row == head_of_lane, o, 0.0)
        out_ref[0] = jnp.sum(o.reshape(t_new, n_heads, o.shape[1]), axis=1).astype(BF16)


def _paged_attn(page_table, q_rows, k_pool_t, v_pool_t, li, k_new_t, v_new_t, pages_per_step,
                neg_cum=None, selected=None):
    nseq, n_pages = page_table.shape
    width, page = k_pool_t.shape[2:]
    n_heads = width // HEAD_DIM
    rows = q_rows.shape[1]
    t_new = rows // n_heads
    n_main = n_pages // pages_per_step
    decay = neg_cum is not None
    side = neg_cum if decay else selected
    per_seq = lambda n, p, pt: (n, 0, 0)
    body = functools.partial(_paged_attn_body, pages_per_step=pages_per_step, n_main=n_main, page=page,
                             t_new=t_new, n_heads=n_heads, decay=decay)
    args = [page_table, q_rows] + [k_pool_t] * pages_per_step + [v_pool_t] * pages_per_step + [k_new_t, v_new_t, side]
    in_specs = [pl.BlockSpec((1,) + q_rows.shape[1:], per_seq)]
    in_specs += _page_specs(k_pool_t, li, pages_per_step, n_pages) + _page_specs(v_pool_t, li, pages_per_step, n_pages)
    in_specs += [pl.BlockSpec((1, width, page), per_seq), pl.BlockSpec((1, width, page), per_seq),
                 pl.BlockSpec((1,) + side.shape[1:], per_seq)]
    scratch = [pltpu.VMEM((rows, 1), F32), pltpu.VMEM((rows, 1), F32), pltpu.VMEM((rows, width), F32)]
    grid_spec = pltpu.PrefetchScalarGridSpec(
        num_scalar_prefetch=1, grid=(nseq, n_main + 1), in_specs=in_specs,
        out_specs=pl.BlockSpec((1, t_new, width), per_seq), scratch_shapes=scratch)
    return pl.pallas_call(
        body, grid_spec=grid_spec, out_shape=jax.ShapeDtypeStruct((nseq, t_new, width), BF16),
        compiler_params=_params(), name="paged_attn_decay" if decay else "paged_attn_select")(*args)


def _rope_tables(pos):
    half = HEAD_DIM // 2
    inv = ROPE_THETA ** (-jnp.arange(half, dtype=F32) / half)
    ang = pos.astype(F32)[:, None] * inv[None, :]
    cos, sin = jnp.cos(ang), jnp.sin(ang)
    reps = LANES // HEAD_DIM
    return jnp.tile(jnp.concatenate([cos, cos], axis=1), (1, reps)), jnp.tile(jnp.concatenate([-sin, sin], axis=1), (1, reps))


def _block_diag(w):
    g, bw, _ = w.shape
    eye = jnp.eye(g, dtype=w.dtype)
    return (w[:, :, None, :] * eye[:, None, :, None]).reshape(g * bw, g * bw)


def _pad_cols(w, width):
    return jnp.pad(w, ((0, 0), (0, width - w.shape[1])))


def _to_time_major(a):
    return jnp.swapaxes(a, 0, 1).reshape((a.shape[0] * a.shape[1],) + a.shape[2:])


def _from_time_major(a, n):
    return jnp.swapaxes(a.reshape((a.shape[0] // n, n) + a.shape[1:]), 0, 1)


def _expand_heads(a, n, n_heads):
    width = a.shape[1]
    per_seq = _from_time_major(a, n)
    mask = (jnp.arange(width)[None, :] // HEAD_DIM) == jnp.arange(n_heads)[:, None]
    out = jnp.where(mask[None, None], per_seq[:, :, None, :], jnp.zeros((), a.dtype))
    return out.reshape(n, -1, width)


def _new_pages_t(a, n, page):
    return jnp.swapaxes(_new_pages(a, n, page), 1, 2)


def _key_minor(pool):
    moved = jnp.moveaxis(pool, 2, -1)
    return moved.reshape(moved.shape[:2] + (-1, moved.shape[-1]))


def _new_pages(a, n, page):
    per_seq = _from_time_major(a, n)
    return jnp.pad(per_seq, ((0, 0), (0, page - per_seq.shape[1]), (0, 0)))


def _pick_tile(m, pref):
    t = min(m, pref)
    while m % t:
        t //= 2
    return t


def kernel(x_prompt, x_sample, cache_k_b, cache_v_b, cache_kidx_b, state_conv_a, cache_k_c, cache_v_c, cache_logf_c, state_conv_d, state_lru_h, page_table, g_ff1, w_ff1_gate, w_ff1_up, w_ff1_down, g_mix, g_ff2, w_ff2_gate, w_ff2_up, w_ff2_down, w_in_e, conv_a_w, w_out_e, w_in_o, b_f, conv_d_w, conv_d_b, w_rg_a, b_rg_a, w_rg_x, b_rg_x, lam, w_out_o, g_final):
    batch, seq, d_model = x_prompt.shape
    nseq, t_new, _ = x_sample.shape
    depth = g_ff1.shape[0]
    n_pool, page = cache_k_b.shape[1], cache_k_b.shape[2]
    n_pages = page_table.shape[1]
    past = n_pages * page
    wa = conv_a_w.shape[2]
    hb = w_out_e.shape[1] - wa
    wd = conv_d_w.shape[2]
    hc = w_out_o.shape[1] - wd
    n_heads_c = hc // HEAD_DIM
    pages_per_step = 8 if n_pages % 8 == 0 else 1

    mp, ms = batch * seq, nseq * t_new
    tm_p = _pick_tile(seq, 512)
    xp = x_prompt.reshape(mp, d_model)
    xs = _to_time_major(x_sample)
    cos_p, sin_p = _rope_tables(jnp.tile(jnp.arange(seq), batch))
    cos_s, sin_s = _rope_tables(past + jnp.repeat(jnp.arange(t_new), nseq))
    row2 = lambda v: v.reshape(1, -1)
    bf = lambda w: w.astype(BF16)

    outs_p, outs_s = {}, {}
    for l in range(depth):
        ffn1 = (row2(g_ff1[l]), bf(w_ff1_gate[l]), bf(w_ff1_up[l]), bf(w_ff1_down[l]))
        ffn2 = (row2(g_ff2[l]), bf(w_ff2_gate[l]), bf(w_ff2_up[l]), bf(w_ff2_down[l]))
        xp = _ffn(xp, *ffn1, tm=tm_p)
        xs = _ffn(xs, *ffn1, tm=ms)
        g = row2(g_mix[l])
        if l % 2 == 0:
            e = l // 2
            w = w_in_e[e]
            o_q, o_k, o_v, o_qi, o_kw = 3 * wa, 3 * wa + hb, 3 * wa + 2 * hb, 3 * wa + 3 * hb, 3 * wa + 3 * hb + H_IDX * D_IDX
            weights = (bf(w[:, :o_q]), bf(w[:, o_q:o_k]), bf(w[:, o_k:o_v]), bf(w[:, o_v:o_qi]), bf(w[:, o_qi:o_kw]),
                       bf(_pad_cols(w[:, o_kw:], LANES)))
            w_out = (bf(w_out_e[e][:wa]), bf(w_out_e[e][wa:]))

            hist_p = jnp.zeros((batch, conv_a_w.shape[1] - 1, wa), F32)
            ya, q, k32, kb, v32, vb, qi, kw, ki2, cstate = _proj_even(
                xp, g, weights, cos_p, sin_p, conv_a_w[e], hist_p, tm_p, seq // tm_p, 1)
            yb = _dsa_prompt(qi, kw, ki2, q, kb, vb, batch, seq, _pick_tile(seq, 256), _pick_tile(seq, 512),
                             _pick_tile(seq, 1024))
            outs_p.setdefault("k_b", []).append(k32.reshape(batch, seq, hb // HEAD_DIM, HEAD_DIM))
            outs_p.setdefault("v_b", []).append(v32.reshape(batch, seq, hb // HEAD_DIM, HEAD_DIM))
            outs_p.setdefault("kidx_b", []).append(kw[:, :D_IDX].reshape(batch, seq, D_IDX))
            outs_p.setdefault("conv_a", []).append(cstate)
            mix_p = (ya, yb, *w_out)

            hist_s = _to_time_major(state_conv_a[e])[None]
            ya, q, k32, kb, v32, vb, qi, kw, ki2, cstate = _proj_even(
                xs, g, weights, cos_s, sin_s, conv_a_w[e], hist_s, ms, 1, nseq)
            qi_rows = _from_time_major(qi, nseq).reshape(nseq, t_new * H_IDX, D_IDX)
            w_rows = _from_time_major(kw[:, D_IDX:D_IDX + H_IDX], nseq).reshape(nseq, t_new * H_IDX, 1)
            scores = _idx_scores_sample(page_table, qi_rows, w_rows, _key_minor(cache_kidx_b), e,
                                        _new_pages_t(kw[:, :D_IDX], nseq, page), pages_per_step)
            length = scores.shape[2]
            n_sel = min(TOPK_MAX, (past + t_new) // 4)
            kc = page * 3 if (length // page) % 3 == 0 else page
            picked = _topk_rows(scores.reshape(ms, length), n_sel, _pick_tile(ms, 32), kc).reshape(nseq, t_new, length)
            yb = _paged_attn(page_table, _expand_heads(q, nseq, hb // HEAD_DIM), _key_minor(cache_k_b),
                             _key_minor(cache_v_b), e, _new_pages_t(k32, nseq, page), _new_pages_t(v32, nseq, page),
                             pages_per_step, selected=picked)
            yb = _to_time_major(yb)
            outs_s.setdefault("k_b", []).append(_from_time_major(k32, nseq).reshape(nseq, t_new, hb // HEAD_DIM, HEAD_DIM))
            outs_s.setdefault("v_b", []).append(_from_time_major(v32, nseq).reshape(nseq, t_new, hb // HEAD_DIM, HEAD_DIM))
            outs_s.setdefault("kidx_b", []).append(_from_time_major(kw[:, :D_IDX], nseq))
            outs_s.setdefault("conv_a", []).append(_from_time_major(cstate[0], nseq))
            mix_s = (ya, yb, *w_out)
        else:
            o = l // 2
            w = w_in_o[o]
            o_f, o_xd, o_gd = 3 * hc, 3 * hc + n_heads_c, 3 * hc + n_heads_c + wd
            weights = (bf(w[:, :o_f]), bf(_pad_cols(w[:, o_f:o_xd], LANES)), bf(w[:, o_xd:o_gd]), bf(w[:, o_gd:]))
            small = (_pad_cols(row2(b_f[o]), LANES), conv_d_w[o], row2(conv_d_b[o]), bf(_block_diag(w_rg_a[o])),
                     row2(b_rg_a[o]), bf(_block_diag(w_rg_x[o])), row2(b_rg_x[o]), row2(lam[o]))
            w_out = (bf(w_out_o[o][:hc]), bf(w_out_o[o][hc:]))

            hist_p = jnp.zeros((batch, conv_d_w.shape[1] - 1, wd), F32)
            h0_p = jnp.zeros((batch, 1, wd), F32)
            q, k32, kb, v32, vb, logf, yd, cstate, hlast = _proj_odd(
                xp, g, weights, small, hist_p, h0_p, tm_p, seq // tm_p, 1, 0)
            cum_t = _cumsum_heads(logf, batch, seq, _pick_tile(seq, 512))
            yc = _fox_prompt(q, kb, vb, cum_t, batch, seq, _pick_tile(seq, 256), _pick_tile(seq, 1024))
            outs_p.setdefault("k_c", []).append(k32.reshape(batch, seq, n_heads_c, HEAD_DIM))
            outs_p.setdefault("v_c", []).append(v32.reshape(batch, seq, n_heads_c, HEAD_DIM))
            outs_p.setdefault("logf_c", []).append(logf.reshape(batch, seq, n_heads_c))
            outs_p.setdefault("conv_d", []).append(cstate)
            outs_p.setdefault("lru_h", []).append(hlast[:, 0])
            mix_p = (yc, yd, *w_out)

            hist_s = _to_time_major(state_conv_d[o])[None]
            h0_s = state_lru_h[o][None]
            q, k32, kb, v32, vb, logf, yd, cstate, hlast = _proj_odd(
                xs, g, weights, small, hist_s, h0_s, ms, 1, nseq, past)
            neg_cum = _decay_bias_sample(page_table, _key_minor(cache_logf_c), o, _new_pages_t(logf, nseq, page))
            yc = _paged_attn(page_table, _expand_heads(q, nseq, n_heads_c), _key_minor(cache_k_c),
                             _key_minor(cache_v_c), o, _new_pages_t(k32, nseq, page), _new_pages_t(v32, nseq, page),
                             pages_per_step, neg_cum=neg_cum)
            yc = _to_time_major(yc)
            outs_s.setdefault("k_c", []).append(_from_time_major(k32, nseq).reshape(nseq, t_new, n_heads_c, HEAD_DIM))
            outs_s.setdefault("v_c", []).append(_from_time_major(v32, nseq).reshape(nseq, t_new, n_heads_c, HEAD_DIM))
            outs_s.setdefault("logf_c", []).append(_from_time_major(logf, nseq))
            outs_s.setdefault("conv_d", []).append(_from_time_major(cstate[0], nseq))
            outs_s.setdefault("lru_h", []).append(hlast[0])
            mix_s = (yc, yd, *w_out)
        gfin = row2(g_final) if l == depth - 1 else None
        xp = _ffn(xp, *ffn2, tm=tm_p, mix=mix_p, g_final=gfin)
        xs = _ffn(xs, *ffn2, tm=ms, mix=mix_s, g_final=gfin)

    names = ("k_b", "v_b", "kidx_b", "conv_a", "k_c", "v_c", "logf_c", "conv_d", "lru_h")
    st_p = tuple(jnp.stack(outs_p[nm]) for nm in names)
    st_s = tuple(jnp.stack(outs_s[nm]) for nm in names)
    return (xp.reshape(batch, seq, d_model), _from_time_major(xs, nseq)) + st_p + st_s
```

```python
import functools

import jax
import jax.numpy as jnp
import numpy as np
from jax import lax
from jax.experimental import pallas as pl
from jax.experimental.pallas import tpu as pltpu

F32 = jnp.float32
BF16 = jnp.bfloat16

HEAD_DIM = 64
D_IDX = 64
H_IDX = 8
TOPK_MAX = 256
LRU_BLOCKS = 8
RG_C = 8.0
ROPE_THETA = 10000.0
EPS = 1e-6

LANES = 128
VMEM_LIMIT = 56 * 1024 * 1024
NEG_INF = float("-inf")
M_INIT = -1e30
BISECT_ITERS = 400
TIE_CHECK_FIRST = 20
TIE_CHECK_EVERY = 8
TOPK_ROW_BLOCK = 64

_NT = (((1,), (1,)), ((), ()))


def _params():
    return pltpu.CompilerParams(vmem_limit_bytes=VMEM_LIMIT)


def _const_spec(shape):
    nd = len(shape)
    return pl.BlockSpec(shape, lambda *_: (0,) * nd, pipeline_mode=pl.Buffered(1))


def _dot(a, b):
    return jnp.dot(a, b, preferred_element_type=F32)


def _dot_nt(a, b, precision=None):
    return lax.dot_general(a, b, _NT, preferred_element_type=F32, precision=precision)


def _rmsnorm(x, g):
    ms = jnp.mean(x * x, axis=-1, keepdims=True)
    return x * lax.rsqrt(ms + EPS) * g


def _ffn_body(*refs, n_chunks, fc, has_mix, has_final):
    it = iter(refs)
    x_ref = next(it)
    if has_mix:
        ya_ref, yb_ref, woa_ref, wob_ref = next(it), next(it), next(it), next(it)
    g_ref, wg_ref, wu_ref, wd_ref = next(it), next(it), next(it), next(it)
    if has_final:
        gf_ref = next(it)
    out_ref = next(it)

    x = x_ref[...]
    if has_mix:
        x = x + _dot(ya_ref[...], woa_ref[...]) + _dot(yb_ref[...], wob_ref[...])
    h = _rmsnorm(x, g_ref[...]).astype(BF16)
    acc = None
    for c in range(n_chunks):
        gate = _dot(h, wg_ref[:, c * fc:(c + 1) * fc])
        up = _dot(h, wu_ref[:, c * fc:(c + 1) * fc])
        a = (gate * jax.nn.sigmoid(gate) * up).astype(BF16)
        part = _dot(a, wd_ref[c * fc:(c + 1) * fc, :])
        acc = part if acc is None else acc + part
    x = x + 0.5 * acc
    if has_final:
        x = _rmsnorm(x, gf_ref[...])
    out_ref[...] = x


def _ffn(x, g, wg, wu, wd, tm, mix=None, g_final=None):
    m, d = x.shape
    f = wg.shape[1]
    n_chunks = 2 if (f // 2) % LANES == 0 else 1
    fc = f // n_chunks
    row = lambda i: (i, 0)
    args, specs = [x], [pl.BlockSpec((tm, d), row)]
    if mix is not None:
        ya, yb, woa, wob = mix
        args += [ya, yb, woa, wob]
        specs += [pl.BlockSpec((tm, ya.shape[1]), row), pl.BlockSpec((tm, yb.shape[1]), row),
                  _const_spec(woa.shape), _const_spec(wob.shape)]
    args += [g, wg, wu, wd]
    specs += [_const_spec(g.shape), _const_spec(wg.shape), _const_spec(wu.shape), _const_spec(wd.shape)]
    if g_final is not None:
        args.append(g_final)
        specs.append(_const_spec(g_final.shape))
    body = functools.partial(_ffn_body, n_chunks=n_chunks, fc=fc, has_mix=mix is not None,
                             has_final=g_final is not None)
    return pl.pallas_call(
        body, grid=(m // tm,), in_specs=specs, out_specs=pl.BlockSpec((tm, d), row),
        out_shape=jax.ShapeDtypeStruct((m, d), F32), compiler_params=_params(), name="ffn")(*args)


def _rope_tile(xt, cos, sin_signed):
    lane = lax.broadcasted_iota(jnp.int32, xt.shape, 1)
    first_half = (lane % HEAD_DIM) < (HEAD_DIM // 2)
    partner = jnp.where(first_half, pltpu.roll(xt, LANES - HEAD_DIM // 2, 1), pltpu.roll(xt, HEAD_DIM // 2, 1))
    return xt * cos + partner * sin_signed


def _store_f32(ref, cols, value, key_minor):
    if key_minor:
        ref[0, cols, :] = value.T
    else:
        ref[:, cols] = value


def _kv_out(m, width, tm, tiles_per_seq, key_minor):
    if not key_minor:
        return jax.ShapeDtypeStruct((m, width), F32), pl.BlockSpec((tm, width), lambda i: (i, 0))
    shape = (m // (tm * tiles_per_seq), width, tm * tiles_per_seq)
    return (jax.ShapeDtypeStruct(shape, F32),
            pl.BlockSpec((1, width, tm), lambda i: (i // tiles_per_seq, 0, i % tiles_per_seq)))


def _hist_rows(width, stride):
    need = (width - 1) * stride
    pad = -(-need // 8) * 8
    return need, pad


def _causal_conv(u, w_ref, ubuf, hist_ref, state_ref, first_tile, width, stride):
    tm = u.shape[0]
    need, pad = _hist_rows(width, stride)

    @pl.when(first_tile)
    def _():
        ubuf[pad - need:pad, :] = hist_ref[0]

    @pl.when(jnp.logical_not(first_tile))
    def _():
        ubuf[0:pad, :] = ubuf[tm:tm + pad, :]

    ubuf[pad:pad + tm, :] = u
    y = None
    for j in range(width):
        start = pad - (width - 1 - j) * stride
        term = w_ref[j:j + 1, :] * ubuf[start:start + tm, :]
        y = term if y is None else y + term
    state_ref[0] = ubuf[pad + tm - need:pad + tm, :]
    return y


def _proj_even_body(x_ref, g_ref, wc_ref, wq_ref, wk_ref, wv_ref, wqi_ref, wkw_ref, cos_ref, sin_ref,
                    cw_ref, hist_ref,
                    ya_ref, q_ref, k32_ref, kb_ref, v32_ref, vb_ref, qi_ref, kw_ref, ki2_ref, state_ref,
                    ubuf, *, tiles_per_seq, stride, wa, kv_key_minor):
    first_tile = (pl.program_id(0) % tiles_per_seq) == 0
    h = _rmsnorm(x_ref[...], g_ref[...]).astype(BF16)
    cos = cos_ref[...]
    sin = sin_ref[...]
    scale = HEAD_DIM ** -0.5

    zc = _dot(h, wc_ref[...])
    xin, gb, gc = zc[:, :wa], zc[:, wa:2 * wa], zc[:, 2 * wa:]
    yconv = _causal_conv(gc * xin, cw_ref, ubuf, hist_ref, state_ref, first_tile, cw_ref.shape[0], stride)
    ya_ref[...] = (gb * yconv).astype(BF16)

    n_tiles = wq_ref.shape[1] // LANES
    q = _dot(h, wq_ref[...])
    k = _dot(h, wk_ref[...])
    qi = _dot(h, wqi_ref[...])
    for t in range(n_tiles):
        sl = slice(t * LANES, (t + 1) * LANES)
        q_ref[:, sl] = (_rope_tile(q[:, sl], cos, sin) * scale).astype(BF16)
        kt = _rope_tile(k[:, sl], cos, sin)
        _store_f32(k32_ref, sl, kt, kv_key_minor)
        kb_ref[:, sl] = kt.astype(BF16)
        qi_ref[:, sl] = (_rope_tile(qi[:, sl], cos, sin) * (D_IDX ** -0.5)).astype(BF16)
    v = _dot(h, wv_ref[...])
    _store_f32(v32_ref, slice(None), v, kv_key_minor)
    vb_ref[...] = v.astype(BF16)

    kw = _dot(h, wkw_ref[...])
    lane = lax.broadcasted_iota(jnp.int32, kw.shape, 1)
    is_ki = lane < D_IDX
    kw = jnp.where(is_ki, _rope_tile(kw, cos, sin), kw)
    kw_ref[...] = kw
    ki2_ref[...] = jnp.where(is_ki, kw, pltpu.roll(kw, D_IDX, 1)).astype(BF16)


def _proj_even(x, g, w, cos, sin, conv_w, hist, tm, tiles_per_seq, stride, kv_key_minor):
    m, d = x.shape
    wc, wq, wk, wv, wqi, wkw = w
    wa = wc.shape[1] // 3
    hb = wq.shape[1]
    width = conv_w.shape[0]
    need, pad = _hist_rows(width, stride)
    nb = hist.shape[0]
    row = lambda i: (i, 0)
    seq = lambda i: (i // tiles_per_seq, 0, 0)
    in_specs = [pl.BlockSpec((tm, d), row), _const_spec(g.shape)]
    in_specs += [_const_spec(a.shape) for a in (wc, wq, wk, wv, wqi, wkw)]
    in_specs += [pl.BlockSpec((tm, LANES), row), pl.BlockSpec((tm, LANES), row), _const_spec(conv_w.shape),
                 pl.BlockSpec((1, need, wa), seq)]
    out_shape = [jax.ShapeDtypeStruct((m, wa), BF16), jax.ShapeDtypeStruct((m, hb), BF16),
                 jax.ShapeDtypeStruct((m, hb), F32), jax.ShapeDtypeStruct((m, hb), BF16),
                 jax.ShapeDtypeStruct((m, hb), F32), jax.ShapeDtypeStruct((m, hb), BF16),
                 jax.ShapeDtypeStruct((m, hb), BF16), jax.ShapeDtypeStruct((m, LANES), F32),
                 jax.ShapeDtypeStruct((m, LANES), BF16), jax.ShapeDtypeStruct((nb, need, wa), F32)]
    out_specs = [pl.BlockSpec((tm, wa), row)] + [pl.BlockSpec((tm, hb), row)] * 6
    out_specs += [pl.BlockSpec((tm, LANES), row), pl.BlockSpec((tm, LANES), row), pl.BlockSpec((1, need, wa), seq)]
    for pos in (2, 4):
        out_shape[pos], out_specs[pos] = _kv_out(m, hb, tm, tiles_per_seq, kv_key_minor)
    body = functools.partial(_proj_even_body, tiles_per_seq=tiles_per_seq, stride=stride, wa=wa,
                             kv_key_minor=kv_key_minor)
    return pl.pallas_call(
        body, grid=(m // tm,), in_specs=in_specs, out_specs=out_specs, out_shape=out_shape,
        scratch_shapes=[pltpu.VMEM((pad + tm, wa), F32)], compiler_params=_params(), name="proj_even")(
            x, g, wc, wq, wk, wv, wqi, wkw, cos, sin, conv_w, hist)


def _shift_rows(x, d, fill):
    row = lax.broadcasted_iota(jnp.int32, x.shape, 0)
    return jnp.where(row >= d, pltpu.roll(x, d, 0), fill)


def _proj_odd_body(x_ref, g_ref, wqkv_ref, wf_ref, wxd_ref, wgd_ref, bf_ref, cw_ref, cb_ref, wra_ref, bra_ref,
                   wrx_ref, brx_ref, lam_ref, hist_ref, h0_ref,
                   q_ref, k32_ref, kb_ref, v32_ref, vb_ref, logf_ref, yd_ref, state_ref, hlast_ref,
                   ubuf, hcar, *, tiles_per_seq, stride, pos_base, hc, kv_key_minor):
    tile_in_seq = pl.program_id(0) % tiles_per_seq
    first_tile = tile_in_seq == 0
    tm = x_ref.shape[0]
    h = _rmsnorm(x_ref[...], g_ref[...]).astype(BF16)

    z = _dot(h, wqkv_ref[...])
    q_ref[...] = (z[:, :hc] * (HEAD_DIM ** -0.5)).astype(BF16)
    k = z[:, hc:2 * hc]
    v = z[:, 2 * hc:]
    _store_f32(k32_ref, slice(None), k, kv_key_minor)
    kb_ref[...] = k.astype(BF16)
    _store_f32(v32_ref, slice(None), v, kv_key_minor)
    vb_ref[...] = v.astype(BF16)

    zf = _dot(h, wf_ref[...]) + bf_ref[...]
    logf = jnp.minimum(zf, 0.0) - jnp.log1p(jnp.exp(-jnp.abs(zf)))
    logf_ref[...] = logf[:, :logf_ref.shape[1]]

    xd = _dot(h, wxd_ref[...])
    gd = _dot(h, wgd_ref[...])
    xc = _causal_conv(xd, cw_ref, ubuf, hist_ref, state_ref, first_tile, cw_ref.shape[0], stride) + cb_ref[...]
    xcb = xc.astype(BF16)
    r = jax.nn.sigmoid(_dot(xcb, wra_ref[...]) + bra_ref[...])
    ig = jax.nn.sigmoid(_dot(xcb, wrx_ref[...]) + brx_ref[...])
    lam = lam_ref[...]
    softplus_neg_lam = jnp.maximum(-lam, 0.0) + jnp.log1p(jnp.exp(-jnp.abs(lam)))
    log_a = -RG_C * r * softplus_neg_lam
    a = jnp.exp(log_a)
    row = lax.broadcasted_iota(jnp.int32, (tm, 1), 0)
    pos = pos_base + tile_in_seq * (tm // stride) + row // stride
    mult = jnp.where(pos == 0, 1.0, jnp.sqrt(1.0 - jnp.exp(2.0 * log_a)))
    b = mult * ig * xc

    d = stride
    while d < tm:
        a_prev = _shift_rows(a, d, 1.0)
        b_prev = _shift_rows(b, d, 0.0)
        b = a * b_prev + b
        a = a * a_prev
        d *= 2

    @pl.when(first_tile)
    def _():
        hcar[...] = h0_ref[0]

    carry = hcar[...]
    if stride == 1:
        carry_rows = carry
    else:
        carry_rows = jnp.concatenate([carry] * (tm // stride), axis=0)
    hs = a * carry_rows + b
    hcar[...] = hs[tm - stride:, :]
    hlast_ref[0] = hs[tm - stride:, :]
    yd_ref[...] = (hs * jax.nn.gelu(gd)).astype(BF16)


def _proj_odd(x, g, w, small, hist, h0, tm, tiles_per_seq, stride, pos_base, kv_key_minor):
    m, d = x.shape
    wqkv, wf, wxd, wgd = w
    bf, conv_w, conv_b, wra, bra, wrx, brx, lam = small
    hc = wqkv.shape[1] // 3
    wd_ = wxd.shape[1]
    n_heads = hc // HEAD_DIM
    width = conv_w.shape[0]
    need, pad = _hist_rows(width, stride)
    nb = hist.shape[0]
    row = lambda i: (i, 0)
    seq = lambda i: (i // tiles_per_seq, 0, 0)
    ins = [x, g, wqkv, wf, wxd, wgd, bf, conv_w, conv_b, wra, bra, wrx, brx, lam, hist, h0]
    in_specs = [pl.BlockSpec((tm, d), row)] + [_const_spec(a.shape) for a in ins[1:14]]
    in_specs += [pl.BlockSpec((1, need, wd_), seq), pl.BlockSpec((1, stride, wd_), seq)]
    out_shape = [jax.ShapeDtypeStruct((m, hc), BF16), jax.ShapeDtypeStruct((m, hc), F32),
                 jax.ShapeDtypeStruct((m, hc), BF16), jax.ShapeDtypeStruct((m, hc), F32),
                 jax.ShapeDtypeStruct((m, hc), BF16), jax.ShapeDtypeStruct((m, n_heads), F32),
                 jax.ShapeDtypeStruct((m, wd_), BF16), jax.ShapeDtypeStruct((nb, need, wd_), F32),
                 jax.ShapeDtypeStruct((nb, stride, wd_), F32)]
    out_specs = [pl.BlockSpec((tm, hc), row)] * 5 + [pl.BlockSpec((tm, n_heads), row), pl.BlockSpec((tm, wd_), row),
                                                      pl.BlockSpec((1, need, wd_), seq),
                                                      pl.BlockSpec((1, stride, wd_), seq)]
    for pos in (1, 3):
        out_shape[pos], out_specs[pos] = _kv_out(m, hc, tm, tiles_per_seq, kv_key_minor)
    body = functools.partial(_proj_odd_body, tiles_per_seq=tiles_per_seq, stride=stride, pos_base=pos_base, hc=hc,
                             kv_key_minor=kv_key_minor)
    return pl.pallas_call(
        body, grid=(m // tm,), in_specs=in_specs, out_specs=out_specs, out_shape=out_shape,
        scratch_shapes=[pltpu.VMEM((pad + tm, wd_), F32), pltpu.VMEM((stride, wd_), F32)],
        compiler_params=_params(), name="proj_odd")(*ins)


def _cumsum_lanes(lf, carry):
    t, nh = lf.shape
    eye = (lax.broadcasted_iota(jnp.int32, (nh, nh), 0) == lax.broadcasted_iota(jnp.int32, (nh, nh), 1)).astype(F32)
    lf_t = _dot_nt(eye, lf, precision=lax.Precision.HIGHEST)
    tri = (lax.broadcasted_iota(jnp.int32, (t, t), 1) <= lax.broadcasted_iota(jnp.int32, (t, t), 0)).astype(F32)
    return _dot_nt(lf_t, tri, precision=lax.Precision.HIGHEST) + carry


def _cumsum_body(lf_ref, out_ref, carry):
    @pl.when(pl.program_id(1) == 0)
    def _():
        carry[...] = jnp.zeros_like(carry)

    cum = _cumsum_lanes(lf_ref[...], carry[:, 0:1])
    out_ref[0] = cum
    carry[...] = jnp.broadcast_to(cum[:, cum.shape[1] - 1:], carry.shape)


def _cumsum_heads(logf, n, s, tc):
    nh = logf.shape[1]
    tiles = s // tc
    return pl.pallas_call(
        _cumsum_body, grid=(n, tiles),
        in_specs=[pl.BlockSpec((tc, nh), lambda b, i: (b * tiles + i, 0))],
        out_specs=pl.BlockSpec((1, nh, tc), lambda b, i: (b, 0, i)),
        out_shape=jax.ShapeDtypeStruct((n, nh, s), F32),
        scratch_shapes=[pltpu.VMEM((nh, LANES), F32)], compiler_params=_params(), name="cumsum")(logf)


def _pair_mask(shape, parity):
    lane = lax.broadcasted_iota(jnp.int32, shape, 1)
    return (lane >= HEAD_DIM) == (parity == 1)


def _stack_pair_queries(q_ref, qs_ref, tq):
    for j in range(q_ref.shape[1] // LANES):
        q_pair = q_ref[:, j * LANES:(j + 1) * LANES]
        zero = jnp.zeros_like(q_pair)
        qs_ref[j, 0:tq, :] = jnp.where(_pair_mask(q_pair.shape, 0), q_pair, zero)
        qs_ref[j, tq:2 * tq, :] = jnp.where(_pair_mask(q_pair.shape, 1), q_pair, zero)


def _attend_pairs(qs_ref, k_ref, v_ref, out_ref, s_buf, stat_ref, lsum_ref, acc_ref, *, tq, kb, n_plain, n_chunks,
                  logits_fn):
    tiles = kb // LANES
    n_pairs = qs_ref.shape[0]

    def col_of(j):
        start = j * LANES
        return pl.ds(start if isinstance(j, int) else pl.multiple_of(start, LANES), LANES)

    def pass1(j, c, masked):
        rows = pl.ds(pl.multiple_of(c * kb, kb), kb)
        s = logits_fn(j, c, _dot_nt(qs_ref[j], k_ref[rows, col_of(j)]), masked)
        s_buf[:, rows] = s
        mt = s[:, 0:LANES]
        for t in range(1, tiles):
            mt = jnp.maximum(mt, s[:, t * LANES:(t + 1) * LANES])
        stat_ref[j % 2] = jnp.maximum(stat_ref[j % 2], mt)

    def pass2(j, c):
        rows = pl.ds(pl.multiple_of(c * kb, kb), kb)
        m_rows = jnp.concatenate([stat_ref[j % 2]] * tiles, axis=1)
        p = jnp.exp(s_buf[:, rows] - m_rows)
        part = p[:, 0:LANES]
        for t in range(1, tiles):
            part = part + p[:, t * LANES:(t + 1) * LANES]
        lsum_ref[...] += part
        acc_ref[...] += _dot(p.astype(BF16), v_ref[rows, col_of(j)])

    def sweep(first, second):
        def body(c, _, masked):
            if second is not None:
                pass2(second, c)
            if first is not None:
                pass1(first, c, masked)
            return 0
        if first is None:
            lax.fori_loop(0, n_chunks, functools.partial(body, masked=False), 0)
        else:
            lax.fori_loop(0, n_plain, functools.partial(body, masked=False), 0)
            lax.fori_loop(n_plain, n_chunks, functools.partial(body, masked=True), 0)

    def open_pair(j):
        stat_ref[j % 2] = jnp.full(stat_ref.shape[1:], NEG_INF, F32)

    def close_max(j):
        m = jnp.max(stat_ref[j % 2], axis=1, keepdims=True)
        stat_ref[j % 2] = jnp.broadcast_to(m, stat_ref.shape[1:])
        lsum_ref[...] = jnp.zeros(lsum_ref.shape, F32)
        acc_ref[...] = jnp.zeros(acc_ref.shape, F32)

    def finish(j):
        o = acc_ref[...] / jnp.sum(lsum_ref[...], axis=1, keepdims=True)
        even, odd = o[0:tq], o[tq:2 * tq]
        out_ref[:, col_of(j)] = jnp.where(_pair_mask(even.shape, 1), odd, even).astype(BF16)

    open_pair(0)
    sweep(0, None)
    close_max(0)

    def stage(j, _):
        open_pair(j)
        sweep(j, j - 1)
        finish(j - 1)
        close_max(j)
        return 0

    lax.fori_loop(1, n_pairs, stage, 0)
    sweep(None, n_pairs - 1)
    finish(n_pairs - 1)


def _pair_scratch(n_pairs, tq, s):
    return [pltpu.VMEM((n_pairs, 2 * tq, LANES), BF16), pltpu.VMEM((2 * tq, s), F32),
            pltpu.VMEM((2, 2 * tq, LANES), F32), pltpu.VMEM((2 * tq, LANES), F32), pltpu.VMEM((2 * tq, LANES), F32)]


def _fox_body(q_ref, k_ref, v_ref, cum_ref, out_ref, qs_ref, s_buf, stat_ref, lsum_ref, acc_ref, *, tq, kc):
    i = pl.program_id(1)
    n_full = (i * tq) // kc
    n_tot = ((i + 1) * tq + kc - 1) // kc
    _stack_pair_queries(q_ref, qs_ref, tq)

    def logits(j, c, raw, masked):
        rows = pl.ds(pl.multiple_of(c * kc, kc), kc)
        halves = []
        for parity in (0, 1):
            sh = raw[parity * tq:(parity + 1) * tq] - cum_ref[0, pl.ds(2 * j + parity, 1), rows]
            if masked:
                qpos = i * tq + lax.broadcasted_iota(jnp.int32, (tq, kc), 0)
                kpos = c * kc + lax.broadcasted_iota(jnp.int32, (tq, kc), 1)
                sh = jnp.where(kpos <= qpos, sh, NEG_INF)
            halves.append(sh)
        return jnp.concatenate(halves, axis=0)

    _attend_pairs(qs_ref, k_ref, v_ref, out_ref, s_buf, stat_ref, lsum_ref, acc_ref, tq=tq, kb=kc,
                  n_plain=n_full, n_chunks=n_tot, logits_fn=logits)


def _fox_prompt(q, kb, vb, cum_t, n, s, tq, kc):
    hc = q.shape[1]
    nh = cum_t.shape[1]
    nq = s // tq
    body = functools.partial(_fox_body, tq=tq, kc=kc)
    scratch = _pair_scratch(hc // LANES, tq, s)
    one = pl.Buffered(1)
    return pl.pallas_call(
        body, grid=(n, nq),
        in_specs=[pl.BlockSpec((tq, hc), lambda b, i: (b * nq + i, 0)),
                  pl.BlockSpec((s, hc), lambda b, i: (b, 0), pipeline_mode=one),
                  pl.BlockSpec((s, hc), lambda b, i: (b, 0), pipeline_mode=one),
                  pl.BlockSpec((1, nh, s), lambda b, i: (b, 0, 0), pipeline_mode=one)],
        out_specs=pl.BlockSpec((tq, hc), lambda b, i: (b * nq + i, 0)),
        out_shape=jax.ShapeDtypeStruct((n * s, hc), BF16), scratch_shapes=scratch,
        compiler_params=_params(), name="fox_prompt")(
            q, kb, vb, cum_t)


def _topk_to_bias(sc_ref, nkc, kc, n_sel, on=0.0, off=NEG_INF):
    rows = sc_ref.shape[0]
    tiles = kc // LANES
    pos_inf = float("inf")

    rb = min(rows, TOPK_ROW_BLOCK)

    def lane_fold(fn, init):
        parts = []
        for r0 in range(0, rows, rb):
            rs = slice(r0, r0 + rb)

            def body(c, acc, rs=rs):
                base = pl.multiple_of(c * kc, kc)
                for t in range(tiles):
                    acc = fn(acc, sc_ref[rs, pl.ds(base + t * LANES, LANES)], rs)
                return acc
            parts.append(lax.fori_loop(0, nkc, body, jnp.full((rb, LANES), init, F32)))
        return jnp.concatenate(parts, axis=0)

    def count(pred_fn):
        part = lane_fold(lambda acc, x, rs: acc + jnp.where(pred_fn(x, rs), 1.0, 0.0), 0.0)
        return jnp.sum(part, axis=1, keepdims=True)

    def range_and_count():
        parts = []
        for r0 in range(0, rows, rb):
            def body(c, acc, r0=r0):
                hi_, lo_, n_ = acc
                base = pl.multiple_of(c * kc, kc)
                for t in range(tiles):
                    x = sc_ref[r0:r0 + rb, pl.ds(base + t * LANES, LANES)]
                    visible = x > NEG_INF
                    hi_ = jnp.maximum(hi_, x)
                    lo_ = jnp.minimum(lo_, jnp.where(visible, x, pos_inf))
                    n_ = n_ + jnp.where(visible, 1.0, 0.0)
                return hi_, lo_, n_
            seed = tuple(jnp.full((rb, LANES), v, F32) for v in (NEG_INF, pos_inf, 0.0))
            parts.append(lax.fori_loop(0, nkc, body, seed))
        hi_, lo_, n_ = (jnp.concatenate([p[i] for p in parts], axis=0) for i in range(3))
        return (jnp.max(hi_, axis=1, keepdims=True), jnp.min(lo_, axis=1, keepdims=True),
                jnp.sum(n_, axis=1, keepdims=True))

    mx, mn, n_visible = range_and_count()

    def live_rows(lo, hi, cnt):
        mid = lo + 0.5 * (hi - lo)
        splittable = jnp.logical_and(mid > lo, mid < hi)
        return jnp.where(jnp.logical_and(cnt > n_sel, splittable), 1.0, 0.0)

    def halve(lo, hi, cnt):
        mid = lo + 0.5 * (hi - lo)
        midb = jnp.broadcast_to(mid, (rows, LANES))
        c_mid = count(lambda x, rs: x >= midb[rs])
        enough = c_mid >= n_sel
        return jnp.where(enough, mid, lo), jnp.where(enough, hi, mid), jnp.where(enough, c_mid, cnt)

    def close_tied(lo, hi, cnt):
        lob = jnp.broadcast_to(lo, (rows, LANES))
        low = jnp.min(lane_fold(lambda acc, x, rs: jnp.minimum(acc, jnp.where(x >= lob[rs], x, pos_inf)), pos_inf),
                      axis=1, keepdims=True)
        lowb = jnp.broadcast_to(low, (rows, LANES))
        tied = count(lambda x, rs: x > lowb[rs]) < n_sel
        return lo, jnp.where(tied, lo, hi), cnt

    def bisect(carry):
        it, _, lo, hi, cnt = carry
        check = jnp.logical_and(it >= TIE_CHECK_FIRST, (it - TIE_CHECK_FIRST) % TIE_CHECK_EVERY == 0)
        lo, hi, cnt = lax.cond(check, close_tied, halve, lo, hi, cnt)
        return it + 1, jnp.max(live_rows(lo, hi, cnt)), lo, hi, cnt

    def unfinished(carry):
        return jnp.logical_and(carry[0] < BISECT_ITERS, carry[1] > 0.0)

    _, _, lo, _, cnt = lax.while_loop(
        unfinished, bisect, (jnp.int32(0), jnp.max(live_rows(mn, mx, n_visible)), mn, mx, n_visible))
    lob = jnp.broadcast_to(lo, (rows, LANES))
    has_ties = jnp.max(cnt) > n_sel

    @pl.when(jnp.logical_not(has_ties))
    def _():
        def body(c, _):
            base = pl.multiple_of(c * kc, kc)
            for t in range(tiles):
                col = pl.ds(base + t * LANES, LANES)
                sc_ref[:, col] = jnp.where(sc_ref[:, col] >= lob, on, off)
            return 0
        lax.fori_loop(0, nkc, body, 0)

    @pl.when(has_ties)
    def _():
        thr = jnp.min(lane_fold(lambda acc, x, rs: jnp.minimum(acc, jnp.where(x >= lob[rs], x, pos_inf)), pos_inf),
                      axis=1, keepdims=True)
        thrb = jnp.broadcast_to(thr, (rows, LANES))
        need = n_sel - count(lambda x, rs: x > thrb[rs])
        upper = (lax.broadcasted_iota(jnp.int32, (kc, kc), 0) <= lax.broadcasted_iota(jnp.int32, (kc, kc), 1))
        upper = jnp.where(upper, 1.0, 0.0).astype(BF16)

        def body(c, seen):
            col = pl.ds(pl.multiple_of(c * kc, kc), kc)
            x = sc_ref[:, col]
            eq = jnp.where(x == thr, 1.0, 0.0)
            rank = _dot(eq.astype(BF16), upper) + seen
            keep_eq = jnp.where(rank <= need, eq, 0.0)
            sc_ref[:, col] = jnp.where(x > thr, on, jnp.where(keep_eq > 0.5, on, off))
            return seen + jnp.sum(eq, axis=1, keepdims=True)
        lax.fori_loop(0, nkc, body, jnp.zeros((rows, 1), F32))


def _dsa_body(qi_ref, kw_ref, ki2_ref, q_ref, k_ref, v_ref, out_ref, sc_ref, wb_ref, qis_ref,
              qs_ref, s_buf, stat_ref, lsum_ref, acc_ref, *, tq, kc, ks, kb, n_sel):
    i = pl.program_id(1)
    nkc = ((i + 1) * tq + kc - 1) // kc

    kw = kw_ref[...]
    for h in range(H_IDX):
        wb_ref[h] = jnp.broadcast_to(kw[:, D_IDX + h:D_IDX + h + 1] * (H_IDX ** -0.5), (tq, LANES))
        j, parity = divmod(h, 2)
        qi_pair = qi_ref[:, j * LANES:(j + 1) * LANES]
        qis_ref[h * tq:(h + 1) * tq, :] = jnp.where(_pair_mask(qi_pair.shape, parity), qi_pair, jnp.zeros_like(qi_pair))
    qpos = i * tq + lax.broadcasted_iota(jnp.int32, (tq, LANES), 0)
    kiota = lax.broadcasted_iota(jnp.int32, (tq, LANES), 1)

    def score_chunk(c, _):
        base = pl.multiple_of(c * ks, ks)
        d = _dot_nt(qis_ref[...], ki2_ref[pl.ds(base, ks), :])
        for t in range(ks // LANES):
            acc = None
            for h in range(H_IDX):
                term = jnp.maximum(d[h * tq:(h + 1) * tq, t * LANES:(t + 1) * LANES], 0.0) * wb_ref[h]
                acc = term if acc is None else acc + term
            kpos = base + t * LANES + kiota
            sc_ref[:, pl.ds(base + t * LANES, LANES)] = jnp.where(kpos <= qpos, acc, NEG_INF)
        return 0

    n_big = ((i + 1) * tq + kb - 1) // kb
    n_score = ((i + 1) * tq + ks - 1) // ks
    lax.fori_loop(0, n_score, score_chunk, 0)

    def fill_hidden(c, _):
        sc_ref[:, pl.ds(pl.multiple_of(c * ks, ks), ks)] = jnp.full((tq, ks), NEG_INF, F32)
        return 0

    lax.fori_loop(n_score, n_big * (kb // ks), fill_hidden, 0)
    _topk_to_bias(sc_ref, nkc, kc, n_sel)

    _stack_pair_queries(q_ref, qs_ref, tq)

    def logits(j, c, raw, masked):
        bias = sc_ref[:, pl.ds(pl.multiple_of(c * kb, kb), kb)]
        return raw + jnp.concatenate([bias, bias], axis=0)

    _attend_pairs(qs_ref, k_ref, v_ref, out_ref, s_buf, stat_ref, lsum_ref, acc_ref, tq=tq, kb=kb,
                  n_plain=n_big, n_chunks=n_big, logits_fn=logits)


def _dsa_prompt(qi, kw, ki2, q, kb, vb, n, s, tq, kc, kbig):
    hb = q.shape[1]
    nq = s // tq
    n_sel = min(TOPK_MAX, s // 4)
    ks = min(kc, 2 * LANES)
    body = functools.partial(_dsa_body, tq=tq, kc=kc, ks=ks, kb=kbig, n_sel=n_sel)
    scratch = [pltpu.VMEM((tq, s), F32), pltpu.VMEM((H_IDX, tq, LANES), F32), pltpu.VMEM((H_IDX * tq, LANES), BF16)]
    scratch += _pair_scratch(hb // LANES, tq, s)
    one = pl.Buffered(1)
    qrow = lambda b, i: (b * nq + i, 0)
    return pl.pallas_call(
        body, grid=(n, nq),
        in_specs=[pl.BlockSpec((tq, hb), qrow), pl.BlockSpec((tq, LANES), qrow),
                  pl.BlockSpec((s, LANES), lambda b, i: (b, 0), pipeline_mode=one),
                  pl.BlockSpec((tq, hb), qrow),
                  pl.BlockSpec((s, hb), lambda b, i: (b, 0), pipeline_mode=one),
                  pl.BlockSpec((s, hb), lambda b, i: (b, 0), pipeline_mode=one)],
        out_specs=pl.BlockSpec((tq, hb), qrow),
        out_shape=jax.ShapeDtypeStruct((n * s, hb), BF16),
        scratch_shapes=scratch, compiler_params=_params(), name="dsa_prompt")(qi, kw, ki2, q, kb, vb)


def _page_specs(pool, li, pages_per_step, n_pages):
    def spec(r):
        def index(n, p, pt):
            return (li, pt[n, jnp.minimum(p * pages_per_step + r, n_pages - 1)]) + (0,) * (pool.ndim - 2)
        return pl.BlockSpec((1, 1) + pool.shape[2:], index)
    return [spec(r) for r in range(pages_per_step)]


def _idx_scores_body(pt_ref, qi_ref, w_ref, *rest, pages_per_step, n_main, page, t_new):
    page_refs = rest[:pages_per_step]
    new_ref, out_ref = rest[pages_per_step], rest[pages_per_step + 1]
    p = pl.program_id(1)
    qi = qi_ref[0]
    w = w_ref[0] * (H_IDX ** -0.5)

    def scores(ki_t):
        d = _dot(qi, ki_t.astype(BF16))
        contrib = jnp.maximum(d, 0.0) * w
        return jnp.sum(contrib.reshape(t_new, H_IDX, page), axis=1)

    @pl.when(p < n_main)
    def _():
        for r in range(pages_per_step):
            off = pl.multiple_of((p * pages_per_step + r) * page, page)
            out_ref[0, :, pl.ds(off, page)] = scores(page_refs[r][0, 0])

    @pl.when(p == n_main)
    def _():
        sc = scores(new_ref[0])
        t = lax.broadcasted_iota(jnp.int32, sc.shape, 0)
        j = lax.broadcasted_iota(jnp.int32, sc.shape, 1)
        out_ref[0, :, n_main * pages_per_step * page:] = jnp.where(j <= t, sc, NEG_INF)


def _idx_scores_sample(page_table, qi_rows, w_rows, ki_pool, li, ki_new_pages, pages_per_step):
    nseq, n_pages = page_table.shape
    page = ki_pool.shape[3]
    t_new = qi_rows.shape[1] // H_IDX
    n_main = n_pages // pages_per_step
    length = (n_pages + 1) * page
    body = functools.partial(_idx_scores_body, pages_per_step=pages_per_step, n_main=n_main, page=page, t_new=t_new)
    per_seq = lambda n, p, pt: (n, 0, 0)
    grid_spec = pltpu.PrefetchScalarGridSpec(
        num_scalar_prefetch=1, grid=(nseq, n_main + 1),
        in_specs=[pl.BlockSpec((1,) + qi_rows.shape[1:], per_seq), pl.BlockSpec((1,) + w_rows.shape[1:], per_seq)]
        + _page_specs(ki_pool, li, pages_per_step, n_pages)
        + [pl.BlockSpec((1,) + ki_new_pages.shape[1:], per_seq)],
        out_specs=pl.BlockSpec((1, t_new, length), per_seq))
    return pl.pallas_call(
        body, grid_spec=grid_spec, out_shape=jax.ShapeDtypeStruct((nseq, t_new, length), F32),
        compiler_params=_params(), name="idx_scores_sample")(
            page_table, qi_rows, w_rows, *([ki_pool] * pages_per_step), ki_new_pages)


def _topk_rows_body(sc_ref, out_ref, *, kc, n_sel):
    out_ref[...] = sc_ref[...]
    _topk_to_bias(out_ref, out_ref.shape[1] // kc, kc, n_sel, on=1.0, off=0.0)


def _topk_rows(scores, n_sel, rows, kc):
    r, length = scores.shape
    body = functools.partial(_topk_rows_body, kc=kc, n_sel=n_sel)
    return pl.pallas_call(
        body, grid=(r // rows,), in_specs=[pl.BlockSpec((rows, length), lambda i: (i, 0))],
        out_specs=pl.BlockSpec((rows, length), lambda i: (i, 0)),
        out_shape=jax.ShapeDtypeStruct((r, length), F32), compiler_params=_params(), name="topk_rows")(scores)


def _split3(x):
    hi = x.astype(BF16)
    r1 = x - hi.astype(F32)
    mid = r1.astype(BF16)
    lo = (r1 - mid.astype(F32)).astype(BF16)
    return hi, mid, lo


def _decay_bias_body(pt_ref, *rest, pages_per_step, n_pages, n_heads):
    page_refs = rest[:pages_per_step]
    new_ref, out_ref, lf_ref = rest[pages_per_step:]
    p = pl.program_id(1)
    n_rows, page = lf_ref.shape

    @pl.when(p == 0)
    def _():
        lf_ref[n_pages * n_heads:, :] = new_ref[0]

    for r in range(pages_per_step):
        row0 = pl.multiple_of((p * pages_per_step + r) * n_heads, n_heads)
        lf_ref[pl.ds(row0, n_heads), :] = page_refs[r][0, 0]

    @pl.when(p == pl.num_programs(1) - 1)
    def _():
        upto = jnp.where(lax.broadcasted_iota(jnp.int32, (page, page), 0)
                         <= lax.broadcasted_iota(jnp.int32, (page, page), 1), 1.0, 0.0).astype(BF16)
        src = lax.broadcasted_iota(jnp.int32, (n_rows, n_rows), 1)
        dst = lax.broadcasted_iota(jnp.int32, (n_rows, n_rows), 0)
        earlier_pages = jnp.where((src % n_heads) == (dst % n_heads),
                                  jnp.where(src // n_heads < dst // n_heads, 1.0, 0.0), 0.0).astype(BF16)
        cum = sum(_dot(x, upto) for x in _split3(lf_ref[...]))
        totals = jnp.broadcast_to(cum[:, page - 1:], cum.shape)
        cum = cum + sum(_dot(earlier_pages, x) for x in _split3(totals))
        out_ref[0] = -cum


def _decay_bias_sample(page_table, f_pool_t, li, f_new_t):
    nseq, n_pages = page_table.shape
    n_heads, page = f_pool_t.shape[2:]
    pages_per_step = _pages_per_step(n_pages, 32)
    n_rows = (n_pages + 1) * n_heads
    per_seq = lambda n, p, pt: (n, 0, 0)
    body = functools.partial(_decay_bias_body, pages_per_step=pages_per_step, n_pages=n_pages, n_heads=n_heads)
    grid_spec = pltpu.PrefetchScalarGridSpec(
        num_scalar_prefetch=1, grid=(nseq, n_pages // pages_per_step),
        in_specs=_page_specs(f_pool_t, li, pages_per_step, n_pages) + [pl.BlockSpec((1, n_heads, page), per_seq)],
        out_specs=pl.BlockSpec((1, n_rows, page), per_seq), scratch_shapes=[pltpu.VMEM((n_rows, page), F32)])
    return pl.pallas_call(
        body, grid_spec=grid_spec, out_shape=jax.ShapeDtypeStruct((nseq, n_rows, page), F32),
        compiler_params=_params(), name="decay_bias_sample")(
            page_table, *([f_pool_t] * pages_per_step), f_new_t)


def _paged_attn_body(pt_ref, q_ref, *rest, pages_per_step, n_main, page, t_new, n_heads, decay):
    it = iter(rest)
    k_refs = [next(it) for _ in range(pages_per_step)]
    v_refs = [next(it) for _ in range(pages_per_step)]
    knew_ref, vnew_ref = next(it), next(it)
    side_ref = next(it)
    out_ref = next(it)
    m_ref, l_ref, acc_ref = next(it), next(it), next(it)
    p = pl.program_id(1)
    rows = t_new * n_heads
    q = q_ref[0]

    @pl.when(p == 0)
    def _():
        m_ref[...] = jnp.full(m_ref.shape, M_INIT, F32)
        l_ref[...] = jnp.zeros(l_ref.shape, F32)
        acc_ref[...] = jnp.zeros(acc_ref.shape, F32)

    def page_logits(k_t, page_idx, new_page):
        s = _dot(q, k_t.astype(BF16))
        if decay:
            neg_cum = side_ref[0, pl.ds(pl.multiple_of(page_idx * n_heads, n_heads), n_heads), :]
            s = s + jnp.concatenate([neg_cum] * t_new, axis=0)
            if new_page:
                t = lax.broadcasted_iota(jnp.int32, s.shape, 0) // n_heads
                s = jnp.where(lax.broadcasted_iota(jnp.int32, s.shape, 1) <= t, s, NEG_INF)
            return s
        sel = side_ref[0, :, pl.ds(pl.multiple_of(page_idx * page, page), page)]
        sel_rows = jnp.broadcast_to(sel[:, None, :], (t_new, n_heads, page)).reshape(rows, page)
        return jnp.where(sel_rows > 0.5, s, NEG_INF)

    def update(logits, values_t):
        m = m_ref[...]
        m_new = m
        for s in logits:
            m_new = jnp.maximum(m_new, jnp.max(s, axis=1, keepdims=True))
        alpha = jnp.exp(m - m_new)
        l_new = alpha * l_ref[...]
        acc = alpha * acc_ref[...]
        for s, v_t in zip(logits, values_t):
            pr = jnp.exp(s - m_new)
            l_new = l_new + jnp.sum(pr, axis=1, keepdims=True)
            acc = acc + _dot_nt(pr.astype(BF16), v_t.astype(BF16))
        l_ref[...] = l_new
        acc_ref[...] = acc
        m_ref[...] = m_new

    @pl.when(p < n_main)
    def _():
        update([page_logits(k_refs[r][0, 0], p * pages_per_step + r, False) for r in range(pages_per_step)],
               [v_refs[r][0, 0] for r in range(pages_per_step)])

    @pl.when(p == n_main)
    def _():
        update([page_logits(knew_ref[0], n_main * pages_per_step, True)], [vnew_ref[0]])
        o = acc_ref[...] / l_ref[...]
        head_of_row = lax.broadcasted_iota(jnp.int32, o.shape, 0) % n_heads
        head_of_lane = lax.broadcasted_iota(jnp.int32, o.shape, 1) // HEAD_DIM
        o = jnp.where(head_of_row == head_of_lane, o, 0.0)
        out_ref[0] = jnp.sum(o.reshape(t_new, n_heads, o.shape[1]), axis=1).astype(BF16)


def _paged_attn(page_table, q_rows, k_pool_t, v_pool_t, li, k_new_t, v_new_t, pages_per_step,
                neg_cum=None, selected=None):
    nseq, n_pages = page_table.shape
    width, page = k_pool_t.shape[2:]
    n_heads = width // HEAD_DIM
    rows = q_rows.shape[1]
    t_new = rows // n_heads
    n_main = n_pages // pages_per_step
    decay = neg_cum is not None
    side = neg_cum if decay else selected
    per_seq = lambda n, p, pt: (n, 0, 0)
    body = functools.partial(_paged_attn_body, pages_per_step=pages_per_step, n_main=n_main, page=page,
                             t_new=t_new, n_heads=n_heads, decay=decay)
    args = [page_table, q_rows] + [k_pool_t] * pages_per_step + [v_pool_t] * pages_per_step + [k_new_t, v_new_t, side]
    in_specs = [pl.BlockSpec((1,) + q_rows.shape[1:], per_seq)]
    in_specs += _page_specs(k_pool_t, li, pages_per_step, n_pages) + _page_specs(v_pool_t, li, pages_per_step, n_pages)
    in_specs += [pl.BlockSpec((1, width, page), per_seq), pl.BlockSpec((1, width, page), per_seq),
                 pl.BlockSpec((1,) + side.shape[1:], per_seq)]
    scratch = [pltpu.VMEM((rows, 1), F32), pltpu.VMEM((rows, 1), F32), pltpu.VMEM((rows, width), F32)]
    grid_spec = pltpu.PrefetchScalarGridSpec(
        num_scalar_prefetch=1, grid=(nseq, n_main + 1), in_specs=in_specs,
        out_specs=pl.BlockSpec((1, t_new, width), per_seq), scratch_shapes=scratch)
    return pl.pallas_call(
        body, grid_spec=grid_spec, out_shape=jax.ShapeDtypeStruct((nseq, t_new, width), BF16),
        compiler_params=_params(), name="paged_attn_decay" if decay else "paged_attn_select")(*args)


def _rope_tables(pos):
    half = HEAD_DIM // 2
    inv = ROPE_THETA ** (-jnp.arange(half, dtype=F32) / half)
    ang = pos.astype(F32)[:, None] * inv[None, :]
    cos, sin = jnp.cos(ang), jnp.sin(ang)
    reps = LANES // HEAD_DIM
    return jnp.tile(jnp.concatenate([cos, cos], axis=1), (1, reps)), jnp.tile(jnp.concatenate([-sin, sin], axis=1), (1, reps))


def _block_diag(w):
    g, bw, _ = w.shape
    eye = jnp.eye(g, dtype=w.dtype)
    return (w[:, :, None, :] * eye[:, None, :, None]).reshape(g * bw, g * bw)


def _pad_cols(w, width):
    return jnp.pad(w, ((0, 0), (0, width - w.shape[1])))


def _to_time_major(a):
    return jnp.swapaxes(a, 0, 1).reshape((a.shape[0] * a.shape[1],) + a.shape[2:])


def _from_time_major(a, n):
    return jnp.swapaxes(a.reshape((a.shape[0] // n, n) + a.shape[1:]), 0, 1)


def _expand_heads(a, n, n_heads):
    width = a.shape[1]
    per_seq = _from_time_major(a, n)
    mask = (jnp.arange(width)[None, :] // HEAD_DIM) == jnp.arange(n_heads)[:, None]
    out = jnp.where(mask[None, None], per_seq[:, :, None, :], jnp.zeros((), a.dtype))
    return out.reshape(n, -1, width)


def _new_pages_t(a, n, page):
    return jnp.swapaxes(_new_pages(a, n, page), 1, 2)


def _key_minor(pool):
    moved = jnp.moveaxis(pool, 2, -1)
    return moved.reshape(moved.shape[:2] + (-1, moved.shape[-1]))


def _new_pages(a, n, page):
    per_seq = _from_time_major(a, n)
    return jnp.pad(per_seq, ((0, 0), (0, page - per_seq.shape[1]), (0, 0)))


def _heads_from_key_minor(a):
    n, width, s = a.shape
    return jnp.transpose(a.reshape(n, width // HEAD_DIM, HEAD_DIM, s), (0, 3, 1, 2))


def _pages_per_step(n_pages, pref):
    while n_pages % pref:
        pref //= 2
    return pref


def _pick_tile(m, pref):
    t = min(m, pref)
    while m % t:
        t //= 2
    return t


def kernel(x_prompt, x_sample, cache_k_b, cache_v_b, cache_kidx_b, state_conv_a, cache_k_c, cache_v_c, cache_logf_c, state_conv_d, state_lru_h, page_table, g_ff1, w_ff1_gate, w_ff1_up, w_ff1_down, g_mix, g_ff2, w_ff2_gate, w_ff2_up, w_ff2_down, w_in_e, conv_a_w, w_out_e, w_in_o, b_f, conv_d_w, conv_d_b, w_rg_a, b_rg_a, w_rg_x, b_rg_x, lam, w_out_o, g_final):
    batch, seq, d_model = x_prompt.shape
    nseq, t_new, _ = x_sample.shape
    depth = g_ff1.shape[0]
    n_pool, page = cache_k_b.shape[1], cache_k_b.shape[2]
    n_pages = page_table.shape[1]
    past = n_pages * page
    wa = conv_a_w.shape[2]
    hb = w_out_e.shape[1] - wa
    wd = conv_d_w.shape[2]
    hc = w_out_o.shape[1] - wd
    n_heads_c = hc // HEAD_DIM
    pages_per_step = _pages_per_step(n_pages, 16)
    idx_pages_per_step = _pages_per_step(n_pages, 32)

    mp, ms = batch * seq, nseq * t_new
    tm_p = _pick_tile(seq, 512)
    xp = x_prompt.reshape(mp, d_model)
    xs = _to_time_major(x_sample)
    cos_p, sin_p = _rope_tables(jnp.tile(jnp.arange(seq), batch))
    cos_s, sin_s = _rope_tables(past + jnp.repeat(jnp.arange(t_new), nseq))
    row2 = lambda v: v.reshape(1, -1)
    bf = lambda w: w.astype(BF16)

    outs_p, outs_s = {}, {}
    for l in range(depth):
        ffn1 = (row2(g_ff1[l]), bf(w_ff1_gate[l]), bf(w_ff1_up[l]), bf(w_ff1_down[l]))
        ffn2 = (row2(g_ff2[l]), bf(w_ff2_gate[l]), bf(w_ff2_up[l]), bf(w_ff2_down[l]))
        xp = _ffn(xp, *ffn1, tm=tm_p)
        xs = _ffn(xs, *ffn1, tm=ms)
        g = row2(g_mix[l])
        if l % 2 == 0:
            e = l // 2
            w = w_in_e[e]
            o_q, o_k, o_v, o_qi, o_kw = 3 * wa, 3 * wa + hb, 3 * wa + 2 * hb, 3 * wa + 3 * hb, 3 * wa + 3 * hb + H_IDX * D_IDX
            weights = (bf(w[:, :o_q]), bf(w[:, o_q:o_k]), bf(w[:, o_k:o_v]), bf(w[:, o_v:o_qi]), bf(w[:, o_qi:o_kw]),
                       bf(_pad_cols(w[:, o_kw:], LANES)))
            w_out = (bf(w_out_e[e][:wa]), bf(w_out_e[e][wa:]))

            hist_p = jnp.zeros((batch, conv_a_w.shape[1] - 1, wa), F32)
            ya, q, k32, kb, v32, vb, qi, kw, ki2, cstate = _proj_even(
                xp, g, weights, cos_p, sin_p, conv_a_w[e], hist_p, tm_p, seq // tm_p, 1, True)
            yb = _dsa_prompt(qi, kw, ki2, q, kb, vb, batch, seq, _pick_tile(seq, 256), _pick_tile(seq, 512),
                             _pick_tile(seq, 1024))
            outs_p.setdefault("k_b", []).append(_heads_from_key_minor(k32))
            outs_p.setdefault("v_b", []).append(_heads_from_key_minor(v32))
            outs_p.setdefault("kidx_b", []).append(kw[:, :D_IDX].reshape(batch, seq, D_IDX))
            outs_p.setdefault("conv_a", []).append(cstate)
            mix_p = (ya, yb, *w_out)

            hist_s = _to_time_major(state_conv_a[e])[None]
            ya, q, k32, kb, v32, vb, qi, kw, ki2, cstate = _proj_even(
                xs, g, weights, cos_s, sin_s, conv_a_w[e], hist_s, ms, 1, nseq, False)
            qi_rows = _from_time_major(qi, nseq).reshape(nseq, t_new * H_IDX, D_IDX)
            w_rows = _from_time_major(kw[:, D_IDX:D_IDX + H_IDX], nseq).reshape(nseq, t_new * H_IDX, 1)
            scores = _idx_scores_sample(page_table, qi_rows, w_rows, _key_minor(cache_kidx_b), e,
                                        _new_pages_t(kw[:, :D_IDX], nseq, page), idx_pages_per_step)
            length = scores.shape[2]
            n_sel = min(TOPK_MAX, (past + t_new) // 4)
            kc = page * 3 if (length // page) % 3 == 0 else page
            picked = _topk_rows(scores.reshape(ms, length), n_sel, _pick_tile(ms, 32), kc).reshape(nseq, t_new, length)
            yb = _paged_attn(page_table, _expand_heads(q, nseq, hb // HEAD_DIM), _key_minor(cache_k_b),
                             _key_minor(cache_v_b), e, _new_pages_t(k32, nseq, page), _new_pages_t(v32, nseq, page),
                             pages_per_step, selected=picked)
            yb = _to_time_major(yb)
            outs_s.setdefault("k_b", []).append(_from_time_major(k32, nseq).reshape(nseq, t_new, hb // HEAD_DIM, HEAD_DIM))
            outs_s.setdefault("v_b", []).append(_from_time_major(v32, nseq).reshape(nseq, t_new, hb // HEAD_DIM, HEAD_DIM))
            outs_s.setdefault("kidx_b", []).append(_from_time_major(kw[:, :D_IDX], nseq))
            outs_s.setdefault("conv_a", []).append(_from_time_major(cstate[0], nseq))
            mix_s = (ya, yb, *w_out)
        else:
            o = l // 2
            w = w_in_o[o]
            o_f, o_xd, o_gd = 3 * hc, 3 * hc + n_heads_c, 3 * hc + n_heads_c + wd
            weights = (bf(w[:, :o_f]), bf(_pad_cols(w[:, o_f:o_xd], LANES)), bf(w[:, o_xd:o_gd]), bf(w[:, o_gd:]))
            small = (_pad_cols(row2(b_f[o]), LANES), conv_d_w[o], row2(conv_d_b[o]), bf(_block_diag(w_rg_a[o])),
                     row2(b_rg_a[o]), bf(_block_diag(w_rg_x[o])), row2(b_rg_x[o]), row2(lam[o]))
            w_out = (bf(w_out_o[o][:hc]), bf(w_out_o[o][hc:]))

            hist_p = jnp.zeros((batch, conv_d_w.shape[1] - 1, wd), F32)
            h0_p = jnp.zeros((batch, 1, wd), F32)
            q, k32, kb, v32, vb, logf, yd, cstate, hlast = _proj_odd(
                xp, g, weights, small, hist_p, h0_p, tm_p, seq // tm_p, 1, 0, True)
            cum_t = _cumsum_heads(logf, batch, seq, _pick_tile(seq, 512))
            yc = _fox_prompt(q, kb, vb, cum_t, batch, seq, _pick_tile(seq, 256), _pick_tile(seq, 1024))
            outs_p.setdefault("k_c", []).append(_heads_from_key_minor(k32))
            outs_p.setdefault("v_c", []).append(_heads_from_key_minor(v32))
            outs_p.setdefault("logf_c", []).append(logf.reshape(batch, seq, n_heads_c))
            outs_p.setdefault("conv_d", []).append(cstate)
            outs_p.setdefault("lru_h", []).append(hlast[:, 0])
            mix_p = (yc, yd, *w_out)

            hist_s = _to_time_major(state_conv_d[o])[None]
            h0_s = state_lru_h[o][None]
            q, k32, kb, v32, vb, logf, yd, cstate, hlast = _proj_odd(
                xs, g, weights, small, hist_s, h0_s, ms, 1, nseq, past, False)
            neg_cum = _decay_bias_sample(page_table, _key_minor(cache_logf_c), o, _new_pages_t(logf, nseq, page))
            yc = _paged_attn(page_table, _expand_heads(q, nseq, n_heads_c), _key_minor(cache_k_c),
                             _key_minor(cache_v_c), o, _new_pages_t(k32, nseq, page), _new_pages_t(v32, nseq, page),
                             pages_per_step, neg_cum=neg_cum)
            yc = _to_time_major(yc)
            outs_s.setdefault("k_c", []).append(_from_time_major(k32, nseq).reshape(nseq, t_new, n_heads_c, HEAD_DIM))
            outs_s.setdefault("v_c", []).append(_from_time_major(v32, nseq).reshape(nseq, t_new, n_heads_c, HEAD_DIM))
            outs_s.setdefault("logf_c", []).append(_from_time_major(logf, nseq))
            outs_s.setdefault("conv_d", []).append(_from_time_major(cstate[0], nseq))
            outs_s.setdefault("lru_h", []).append(hlast[0])
            mix_s = (yc, yd, *w_out)
        gfin = row2(g_final) if l == depth - 1 else None
        xp = _ffn(xp, *ffn2, tm=tm_p, mix=mix_p, g_final=gfin)
        xs = _ffn(xs, *ffn2, tm=ms, mix=mix_s, g_final=gfin)

    names = ("k_b", "v_b", "kidx_b", "conv_a", "k_c", "v_c", "logf_c", "conv_d", "lru_h")
    st_p = tuple(jnp.stack(outs_p[nm]) for nm in names)
    st_s = tuple(jnp.stack(outs_s[nm]) for nm in names)
    return (xp.reshape(batch, seq, d_model), _from_time_major(xs, nseq)) + st_p + st_s
```

```python
import functools

import jax
import jax.numpy as jnp
import numpy as np
from jax import lax
from jax.experimental import pallas as pl
from jax.experimental.pallas import tpu as pltpu

F32 = jnp.float32
BF16 = jnp.bfloat16

HEAD_DIM = 64
D_IDX = 64
H_IDX = 8
TOPK_MAX = 256
LRU_BLOCKS = 8
RG_C = 8.0
ROPE_THETA = 10000.0
EPS = 1e-6

LANES = 128
SUBLANES = 8
VMEM_LIMIT = 56 * 1024 * 1024
NEG_INF = float("-inf")
M_INIT = -1e30
BISECT_ITERS = 400
TIE_CHECK_FIRST = 20
TIE_CHECK_EVERY = 8
TOPK_ROW_BLOCK = 128

_NT = (((1,), (1,)), ((), ()))


def _params():
    return pltpu.CompilerParams(vmem_limit_bytes=VMEM_LIMIT)


def _const_spec(shape):
    nd = len(shape)
    return pl.BlockSpec(shape, lambda *_: (0,) * nd, pipeline_mode=pl.Buffered(1))


def _dot(a, b):
    return jnp.dot(a, b, preferred_element_type=F32)


def _dot_nt(a, b, precision=None):
    return lax.dot_general(a, b, _NT, preferred_element_type=F32, precision=precision)


def _rmsnorm(x, g):
    ms = jnp.mean(x * x, axis=-1, keepdims=True)
    return x * lax.rsqrt(ms + EPS) * g


def _ffn_body(*refs, n_chunks, fc, has_mix, has_final):
    it = iter(refs)
    x_ref = next(it)
    if has_mix:
        ya_ref, yb_ref, woa_ref, wob_ref = next(it), next(it), next(it), next(it)
    g_ref, wg_ref, wu_ref, wd_ref = next(it), next(it), next(it), next(it)
    if has_final:
        gf_ref = next(it)
    out_ref = next(it)

    x = x_ref[...]
    if has_mix:
        x = x + _dot(ya_ref[...], woa_ref[...]) + _dot(yb_ref[...], wob_ref[...])
    h = _rmsnorm(x, g_ref[...]).astype(BF16)
    acc = None
    for c in range(n_chunks):
        gate = _dot(h, wg_ref[:, c * fc:(c + 1) * fc])
        up = _dot(h, wu_ref[:, c * fc:(c + 1) * fc])
        a = (gate * jax.nn.sigmoid(gate) * up).astype(BF16)
        part = _dot(a, wd_ref[c * fc:(c + 1) * fc, :])
        acc = part if acc is None else acc + part
    x = x + 0.5 * acc
    if has_final:
        x = _rmsnorm(x, gf_ref[...])
    out_ref[...] = x


def _ffn(x, g, wg, wu, wd, tm, mix=None, g_final=None):
    m, d = x.shape
    f = wg.shape[1]
    n_chunks = 2 if (f // 2) % LANES == 0 else 1
    fc = f // n_chunks
    row = lambda i: (i, 0)
    args, specs = [x], [pl.BlockSpec((tm, d), row)]
    if mix is not None:
        ya, yb, woa, wob = mix
        args += [ya, yb, woa, wob]
        specs += [pl.BlockSpec((tm, ya.shape[1]), row), pl.BlockSpec((tm, yb.shape[1]), row),
                  _const_spec(woa.shape), _const_spec(wob.shape)]
    args += [g, wg, wu, wd]
    specs += [_const_spec(g.shape), _const_spec(wg.shape), _const_spec(wu.shape), _const_spec(wd.shape)]
    if g_final is not None:
        args.append(g_final)
        specs.append(_const_spec(g_final.shape))
    body = functools.partial(_ffn_body, n_chunks=n_chunks, fc=fc, has_mix=mix is not None,
                             has_final=g_final is not None)
    return pl.pallas_call(
        body, grid=(m // tm,), in_specs=specs, out_specs=pl.BlockSpec((tm, d), row),
        out_shape=jax.ShapeDtypeStruct((m, d), F32), compiler_params=_params(), name="ffn")(*args)


def _rope_tile(xt, cos, sin_signed):
    lane = lax.broadcasted_iota(jnp.int32, xt.shape, 1)
    first_half = (lane % HEAD_DIM) < (HEAD_DIM // 2)
    partner = jnp.where(first_half, pltpu.roll(xt, LANES - HEAD_DIM // 2, 1), pltpu.roll(xt, HEAD_DIM // 2, 1))
    return xt * cos + partner * sin_signed


def _store_f32(ref, cols, value, key_minor):
    if key_minor:
        ref[0, cols, :] = value.T
    else:
        ref[:, cols] = value


def _kv_out(m, width, tm, tiles_per_seq, key_minor):
    if not key_minor:
        return jax.ShapeDtypeStruct((m, width), F32), pl.BlockSpec((tm, width), lambda i: (i, 0))
    shape = (m // (tm * tiles_per_seq), width, tm * tiles_per_seq)
    return (jax.ShapeDtypeStruct(shape, F32),
            pl.BlockSpec((1, width, tm), lambda i: (i // tiles_per_seq, 0, i % tiles_per_seq)))


def _hist_rows(width, stride):
    need = (width - 1) * stride
    pad = -(-need // SUBLANES) * SUBLANES
    return need, pad


def _causal_conv(u, w_ref, ubuf, hist_ref, state_ref, first_tile, width, stride):
    tm = u.shape[0]
    need, pad = _hist_rows(width, stride)

    @pl.when(first_tile)
    def _():
        ubuf[pad - need:pad, :] = hist_ref[0]

    @pl.when(jnp.logical_not(first_tile))
    def _():
        ubuf[0:pad, :] = ubuf[tm:tm + pad, :]

    ubuf[pad:pad + tm, :] = u
    y = None
    for j in range(width):
        start = pad - (width - 1 - j) * stride
        term = w_ref[j:j + 1, :] * ubuf[start:start + tm, :]
        y = term if y is None else y + term
    state_ref[0] = ubuf[pad + tm - need:pad + tm, :]
    return y


def _proj_even_body(x_ref, g_ref, wc_ref, wq_ref, wk_ref, wv_ref, wqi_ref, wkw_ref, cos_ref, sin_ref,
                    cw_ref, hist_ref,
                    ya_ref, q_ref, k32_ref, kb_ref, v32_ref, vb_ref, qi_ref, kw_ref, ki2_ref, state_ref,
                    ubuf, *, tiles_per_seq, stride, wa, kv_key_minor):
    first_tile = (pl.program_id(0) % tiles_per_seq) == 0
    h = _rmsnorm(x_ref[...], g_ref[...]).astype(BF16)
    cos = cos_ref[...]
    sin = sin_ref[...]
    scale = HEAD_DIM ** -0.5

    zc = _dot(h, wc_ref[...])
    xin, gb, gc = zc[:, :wa], zc[:, wa:2 * wa], zc[:, 2 * wa:]
    yconv = _causal_conv(gc * xin, cw_ref, ubuf, hist_ref, state_ref, first_tile, cw_ref.shape[0], stride)
    ya_ref[...] = (gb * yconv).astype(BF16)

    n_tiles = wq_ref.shape[1] // LANES
    q = _dot(h, wq_ref[...])
    k = _dot(h, wk_ref[...])
    qi = _dot(h, wqi_ref[...])
    for t in range(n_tiles):
        sl = slice(t * LANES, (t + 1) * LANES)
        q_ref[:, sl] = (_rope_tile(q[:, sl], cos, sin) * scale).astype(BF16)
        kt = _rope_tile(k[:, sl], cos, sin)
        _store_f32(k32_ref, sl, kt, kv_key_minor)
        kb_ref[:, sl] = kt.astype(BF16)
        qi_ref[:, sl] = (_rope_tile(qi[:, sl], cos, sin) * (D_IDX ** -0.5)).astype(BF16)
    v = _dot(h, wv_ref[...])
    _store_f32(v32_ref, slice(None), v, kv_key_minor)
    vb_ref[...] = v.astype(BF16)

    kw = _dot(h, wkw_ref[...])
    lane = lax.broadcasted_iota(jnp.int32, kw.shape, 1)
    is_ki = lane < D_IDX
    kw = jnp.where(is_ki, _rope_tile(kw, cos, sin), kw)
    kw_ref[...] = kw
    ki2_ref[...] = jnp.where(is_ki, kw, pltpu.roll(kw, D_IDX, 1)).astype(BF16)


def _proj_even(x, g, w, cos, sin, conv_w, hist, tm, tiles_per_seq, stride, kv_key_minor):
    m, d = x.shape
    wc, wq, wk, wv, wqi, wkw = w
    wa = wc.shape[1] // 3
    hb = wq.shape[1]
    width = conv_w.shape[0]
    need, pad = _hist_rows(width, stride)
    nb = hist.shape[0]
    row = lambda i: (i, 0)
    seq = lambda i: (i // tiles_per_seq, 0, 0)
    in_specs = [pl.BlockSpec((tm, d), row), _const_spec(g.shape)]
    in_specs += [_const_spec(a.shape) for a in (wc, wq, wk, wv, wqi, wkw)]
    in_specs += [pl.BlockSpec((tm, LANES), row), pl.BlockSpec((tm, LANES), row), _const_spec(conv_w.shape),
                 pl.BlockSpec((1, need, wa), seq)]
    out_shape = [jax.ShapeDtypeStruct((m, wa), BF16), jax.ShapeDtypeStruct((m, hb), BF16),
                 jax.ShapeDtypeStruct((m, hb), F32), jax.ShapeDtypeStruct((m, hb), BF16),
                 jax.ShapeDtypeStruct((m, hb), F32), jax.ShapeDtypeStruct((m, hb), BF16),
                 jax.ShapeDtypeStruct((m, hb), BF16), jax.ShapeDtypeStruct((m, LANES), F32),
                 jax.ShapeDtypeStruct((m, LANES), BF16), jax.ShapeDtypeStruct((nb, need, wa), F32)]
    out_specs = [pl.BlockSpec((tm, wa), row)] + [pl.BlockSpec((tm, hb), row)] * 6
    out_specs += [pl.BlockSpec((tm, LANES), row), pl.BlockSpec((tm, LANES), row), pl.BlockSpec((1, need, wa), seq)]
    for pos in (2, 4):
        out_shape[pos], out_specs[pos] = _kv_out(m, hb, tm, tiles_per_seq, kv_key_minor)
    body = functools.partial(_proj_even_body, tiles_per_seq=tiles_per_seq, stride=stride, wa=wa,
                             kv_key_minor=kv_key_minor)
    return pl.pallas_call(
        body, grid=(m // tm,), in_specs=in_specs, out_specs=out_specs, out_shape=out_shape,
        scratch_shapes=[pltpu.VMEM((pad + tm, wa), F32)], compiler_params=_params(), name="proj_even")(
            x, g, wc, wq, wk, wv, wqi, wkw, cos, sin, conv_w, hist)


def _shift_rows(x, d, fill):
    if d % SUBLANES == 0:
        return jnp.concatenate([jnp.full((d, x.shape[1]), fill, x.dtype), x[:x.shape[0] - d]], axis=0)
    row = lax.broadcasted_iota(jnp.int32, x.shape, 0)
    return jnp.where(row >= d, pltpu.roll(x, d, 0), fill)


def _proj_odd_body(x_ref, g_ref, wqkv_ref, wf_ref, wxd_ref, wgd_ref, bf_ref, cw_ref, cb_ref, wra_ref, bra_ref,
                   wrx_ref, brx_ref, lam_ref, hist_ref, h0_ref,
                   q_ref, k32_ref, kb_ref, v32_ref, vb_ref, logf_ref, yd_ref, state_ref, hlast_ref,
                   ubuf, hcar, *, tiles_per_seq, stride, pos_base, hc, kv_key_minor):
    tile_in_seq = pl.program_id(0) % tiles_per_seq
    first_tile = tile_in_seq == 0
    tm = x_ref.shape[0]
    h = _rmsnorm(x_ref[...], g_ref[...]).astype(BF16)

    z = _dot(h, wqkv_ref[...])
    q_ref[...] = (z[:, :hc] * (HEAD_DIM ** -0.5)).astype(BF16)
    k = z[:, hc:2 * hc]
    v = z[:, 2 * hc:]
    _store_f32(k32_ref, slice(None), k, kv_key_minor)
    kb_ref[...] = k.astype(BF16)
    _store_f32(v32_ref, slice(None), v, kv_key_minor)
    vb_ref[...] = v.astype(BF16)

    zf = _dot(h, wf_ref[...]) + bf_ref[...]
    logf = jnp.minimum(zf, 0.0) - jnp.log1p(jnp.exp(-jnp.abs(zf)))
    logf_ref[...] = logf[:, :logf_ref.shape[1]]

    xd = _dot(h, wxd_ref[...])
    gd = _dot(h, wgd_ref[...])
    xc = _causal_conv(xd, cw_ref, ubuf, hist_ref, state_ref, first_tile, cw_ref.shape[0], stride) + cb_ref[...]
    xcb = xc.astype(BF16)
    r = jax.nn.sigmoid(_dot(xcb, wra_ref[...]) + bra_ref[...])
    ig = jax.nn.sigmoid(_dot(xcb, wrx_ref[...]) + brx_ref[...])
    lam = lam_ref[...]
    softplus_neg_lam = jnp.maximum(-lam, 0.0) + jnp.log1p(jnp.exp(-jnp.abs(lam)))
    log_a = -RG_C * r * softplus_neg_lam
    a = jnp.exp(log_a)
    row = lax.broadcasted_iota(jnp.int32, (tm, 1), 0)
    pos = pos_base + tile_in_seq * (tm // stride) + row // stride
    mult = jnp.where(pos == 0, 1.0, jnp.sqrt(1.0 - jnp.exp(2.0 * log_a)))
    b = mult * ig * xc

    d = stride
    while d < tm:
        a_prev = _shift_rows(a, d, 1.0)
        b_prev = _shift_rows(b, d, 0.0)
        b = a * b_prev + b
        a = a * a_prev
        d *= 2

    @pl.when(first_tile)
    def _():
        hcar[...] = h0_ref[0]

    carry = hcar[...]
    if stride == 1:
        carry_rows = carry
    else:
        carry_rows = jnp.concatenate([carry] * (tm // stride), axis=0)
    hs = a * carry_rows + b
    hcar[...] = hs[tm - stride:, :]
    hlast_ref[0] = hs[tm - stride:, :]
    yd_ref[...] = (hs * jax.nn.gelu(gd)).astype(BF16)


def _proj_odd(x, g, w, small, hist, h0, tm, tiles_per_seq, stride, pos_base, kv_key_minor):
    m, d = x.shape
    wqkv, wf, wxd, wgd = w
    bf, conv_w, conv_b, wra, bra, wrx, brx, lam = small
    hc = wqkv.shape[1] // 3
    wd_ = wxd.shape[1]
    n_heads = hc // HEAD_DIM
    width = conv_w.shape[0]
    need, pad = _hist_rows(width, stride)
    nb = hist.shape[0]
    row = lambda i: (i, 0)
    seq = lambda i: (i // tiles_per_seq, 0, 0)
    ins = [x, g, wqkv, wf, wxd, wgd, bf, conv_w, conv_b, wra, bra, wrx, brx, lam, hist, h0]
    in_specs = [pl.BlockSpec((tm, d), row)] + [_const_spec(a.shape) for a in ins[1:14]]
    in_specs += [pl.BlockSpec((1, need, wd_), seq), pl.BlockSpec((1, stride, wd_), seq)]
    out_shape = [jax.ShapeDtypeStruct((m, hc), BF16), jax.ShapeDtypeStruct((m, hc), F32),
                 jax.ShapeDtypeStruct((m, hc), BF16), jax.ShapeDtypeStruct((m, hc), F32),
                 jax.ShapeDtypeStruct((m, hc), BF16), jax.ShapeDtypeStruct((m, n_heads), F32),
                 jax.ShapeDtypeStruct((m, wd_), BF16), jax.ShapeDtypeStruct((nb, need, wd_), F32),
                 jax.ShapeDtypeStruct((nb, stride, wd_), F32)]
    out_specs = [pl.BlockSpec((tm, hc), row)] * 5 + [pl.BlockSpec((tm, n_heads), row), pl.BlockSpec((tm, wd_), row),
                                                      pl.BlockSpec((1, need, wd_), seq),
                                                      pl.BlockSpec((1, stride, wd_), seq)]
    for pos in (1, 3):
        out_shape[pos], out_specs[pos] = _kv_out(m, hc, tm, tiles_per_seq, kv_key_minor)
    body = functools.partial(_proj_odd_body, tiles_per_seq=tiles_per_seq, stride=stride, pos_base=pos_base, hc=hc,
                             kv_key_minor=kv_key_minor)
    return pl.pallas_call(
        body, grid=(m // tm,), in_specs=in_specs, out_specs=out_specs, out_shape=out_shape,
        scratch_shapes=[pltpu.VMEM((pad + tm, wd_), F32), pltpu.VMEM((stride, wd_), F32)],
        compiler_params=_params(), name="proj_odd")(*ins)


def _cumsum_lanes(lf, carry):
    t, nh = lf.shape
    eye = (lax.broadcasted_iota(jnp.int32, (nh, nh), 0) == lax.broadcasted_iota(jnp.int32, (nh, nh), 1)).astype(F32)
    lf_t = _dot_nt(eye, lf, precision=lax.Precision.HIGHEST)
    tri = (lax.broadcasted_iota(jnp.int32, (t, t), 1) <= lax.broadcasted_iota(jnp.int32, (t, t), 0)).astype(F32)
    return _dot_nt(lf_t, tri, precision=lax.Precision.HIGHEST) + carry


def _cumsum_body(lf_ref, out_ref, carry):
    @pl.when(pl.program_id(1) == 0)
    def _():
        carry[...] = jnp.zeros_like(carry)

    cum = _cumsum_lanes(lf_ref[...], carry[:, 0:1])
    out_ref[0] = cum
    carry[...] = jnp.broadcast_to(cum[:, cum.shape[1] - 1:], carry.shape)


def _cumsum_heads(logf, n, s, tc):
    nh = logf.shape[1]
    tiles = s // tc
    return pl.pallas_call(
        _cumsum_body, grid=(n, tiles),
        in_specs=[pl.BlockSpec((tc, nh), lambda b, i: (b * tiles + i, 0))],
        out_specs=pl.BlockSpec((1, nh, tc), lambda b, i: (b, 0, i)),
        out_shape=jax.ShapeDtypeStruct((n, nh, s), F32),
        scratch_shapes=[pltpu.VMEM((nh, LANES), F32)], compiler_params=_params(), name="cumsum")(logf)


def _pair_mask(shape, parity):
    lane = lax.broadcasted_iota(jnp.int32, shape, 1)
    return (lane >= HEAD_DIM) == (parity == 1)


def _stack_pair_queries(q_ref, qs_ref, tq):
    for j in range(q_ref.shape[1] // LANES):
        q_pair = q_ref[:, j * LANES:(j + 1) * LANES]
        zero = jnp.zeros_like(q_pair)
        qs_ref[j, 0:tq, :] = jnp.where(_pair_mask(q_pair.shape, 0), q_pair, zero)
        qs_ref[j, tq:2 * tq, :] = jnp.where(_pair_mask(q_pair.shape, 1), q_pair, zero)


def _attend_pairs(qs_ref, k_ref, v_ref, out_ref, s_buf, stat_ref, lsum_ref, acc_ref, *, tq, kb, n_plain, n_chunks,
                  logits_fn):
    tiles = kb // LANES
    n_pairs = qs_ref.shape[0]

    def col_of(j):
        start = j * LANES
        return pl.ds(start if isinstance(j, int) else pl.multiple_of(start, LANES), LANES)

    def pass1(j, c, masked):
        rows = pl.ds(pl.multiple_of(c * kb, kb), kb)
        s = logits_fn(j, c, _dot_nt(qs_ref[j], k_ref[rows, col_of(j)]), masked)
        s_buf[:, rows] = s
        mt = s[:, 0:LANES]
        for t in range(1, tiles):
            mt = jnp.maximum(mt, s[:, t * LANES:(t + 1) * LANES])
        stat_ref[j % 2] = jnp.maximum(stat_ref[j % 2], mt)

    def pass2(j, c):
        rows = pl.ds(pl.multiple_of(c * kb, kb), kb)
        m_rows = jnp.concatenate([stat_ref[j % 2]] * tiles, axis=1)
        p = jnp.exp(s_buf[:, rows] - m_rows)
        part = p[:, 0:LANES]
        for t in range(1, tiles):
            part = part + p[:, t * LANES:(t + 1) * LANES]
        lsum_ref[...] += part
        acc_ref[...] += _dot(p.astype(BF16), v_ref[rows, col_of(j)])

    def sweep(first, second):
        def body(c, _, masked):
            if second is not None:
                pass2(second, c)
            if first is not None:
                pass1(first, c, masked)
            return 0
        if first is None:
            lax.fori_loop(0, n_chunks, functools.partial(body, masked=False), 0)
        else:
            lax.fori_loop(0, n_plain, functools.partial(body, masked=False), 0)
            lax.fori_loop(n_plain, n_chunks, functools.partial(body, masked=True), 0)

    def open_pair(j):
        stat_ref[j % 2] = jnp.full(stat_ref.shape[1:], NEG_INF, F32)

    def close_max(j):
        m = jnp.max(stat_ref[j % 2], axis=1, keepdims=True)
        stat_ref[j % 2] = jnp.broadcast_to(m, stat_ref.shape[1:])
        lsum_ref[...] = jnp.zeros(lsum_ref.shape, F32)
        acc_ref[...] = jnp.zeros(acc_ref.shape, F32)

    def finish(j):
        o = acc_ref[...] / jnp.sum(lsum_ref[...], axis=1, keepdims=True)
        even, odd = o[0:tq], o[tq:2 * tq]
        out_ref[:, col_of(j)] = jnp.where(_pair_mask(even.shape, 1), odd, even).astype(BF16)

    open_pair(0)
    sweep(0, None)
    close_max(0)

    def stage(j, _):
        open_pair(j)
        sweep(j, j - 1)
        finish(j - 1)
        close_max(j)
        return 0

    lax.fori_loop(1, n_pairs, stage, 0)
    sweep(None, n_pairs - 1)
    finish(n_pairs - 1)


def _pair_scratch(n_pairs, tq, s):
    return [pltpu.VMEM((n_pairs, 2 * tq, LANES), BF16), pltpu.VMEM((2 * tq, s), F32),
            pltpu.VMEM((2, 2 * tq, LANES), F32), pltpu.VMEM((2 * tq, LANES), F32), pltpu.VMEM((2 * tq, LANES), F32)]


def _fox_body(q_ref, k_ref, v_ref, cum_ref, out_ref, qs_ref, s_buf, stat_ref, lsum_ref, acc_ref, *, tq, kc):
    i = pl.program_id(1)
    n_full = (i * tq) // kc
    n_tot = ((i + 1) * tq + kc - 1) // kc
    _stack_pair_queries(q_ref, qs_ref, tq)

    def logits(j, c, raw, masked):
        rows = pl.ds(pl.multiple_of(c * kc, kc), kc)
        halves = []
        for parity in (0, 1):
            sh = raw[parity * tq:(parity + 1) * tq] - cum_ref[0, pl.ds(2 * j + parity, 1), rows]
            if masked:
                qpos = i * tq + lax.broadcasted_iota(jnp.int32, (tq, kc), 0)
                kpos = c * kc + lax.broadcasted_iota(jnp.int32, (tq, kc), 1)
                sh = jnp.where(kpos <= qpos, sh, NEG_INF)
            halves.append(sh)
        return jnp.concatenate(halves, axis=0)

    _attend_pairs(qs_ref, k_ref, v_ref, out_ref, s_buf, stat_ref, lsum_ref, acc_ref, tq=tq, kb=kc,
                  n_plain=n_full, n_chunks=n_tot, logits_fn=logits)


def _fox_prompt(q, kb, vb, cum_t, n, s, tq, kc):
    hc = q.shape[1]
    nh = cum_t.shape[1]
    nq = s // tq
    body = functools.partial(_fox_body, tq=tq, kc=kc)
    scratch = _pair_scratch(hc // LANES, tq, s)
    one = pl.Buffered(1)
    return pl.pallas_call(
        body, grid=(n, nq),
        in_specs=[pl.BlockSpec((tq, hc), lambda b, i: (b * nq + i, 0)),
                  pl.BlockSpec((s, hc), lambda b, i: (b, 0), pipeline_mode=one),
                  pl.BlockSpec((s, hc), lambda b, i: (b, 0), pipeline_mode=one),
                  pl.BlockSpec((1, nh, s), lambda b, i: (b, 0, 0), pipeline_mode=one)],
        out_specs=pl.BlockSpec((tq, hc), lambda b, i: (b * nq + i, 0)),
        out_shape=jax.ShapeDtypeStruct((n * s, hc), BF16), scratch_shapes=scratch,
        compiler_params=_params(), name="fox_prompt")(
            q, kb, vb, cum_t)


def _topk_to_bias(sc_ref, nkc, kc, n_sel, on=0.0, off=NEG_INF):
    rows = sc_ref.shape[0]
    n_walk, k_walk = nkc, kc
    tiles = k_walk // LANES
    pos_inf = float("inf")

    rb = min(rows, TOPK_ROW_BLOCK)

    def lane_fold(fn, init):
        parts = []
        for r0 in range(0, rows, rb):
            rs = slice(r0, r0 + rb)

            def body(c, acc, rs=rs):
                base = pl.multiple_of(c * k_walk, k_walk)
                for t in range(tiles):
                    acc = fn(acc, sc_ref[rs, pl.ds(base + t * LANES, LANES)], rs)
                return acc
            parts.append(lax.fori_loop(0, n_walk, body, jnp.full((rb, LANES), init, F32)))
        return jnp.concatenate(parts, axis=0)

    def count(pred_fn):
        part = lane_fold(lambda acc, x, rs: acc + jnp.where(pred_fn(x, rs), 1.0, 0.0), 0.0)
        return jnp.sum(part, axis=1, keepdims=True)

    def range_and_count():
        parts = []
        for r0 in range(0, rows, rb):
            def body(c, acc, r0=r0):
                hi_, lo_, n_ = acc
                base = pl.multiple_of(c * k_walk, k_walk)
                for t in range(tiles):
                    x = sc_ref[r0:r0 + rb, pl.ds(base + t * LANES, LANES)]
                    visible = x > NEG_INF
                    hi_ = jnp.maximum(hi_, x)
                    lo_ = jnp.minimum(lo_, jnp.where(visible, x, pos_inf))
                    n_ = n_ + jnp.where(visible, 1.0, 0.0)
                return hi_, lo_, n_
            seed = tuple(jnp.full((rb, LANES), v, F32) for v in (NEG_INF, pos_inf, 0.0))
            parts.append(lax.fori_loop(0, n_walk, body, seed))
        hi_, lo_, n_ = (jnp.concatenate([p[i] for p in parts], axis=0) for i in range(3))
        return (jnp.max(hi_, axis=1, keepdims=True), jnp.min(lo_, axis=1, keepdims=True),
                jnp.sum(n_, axis=1, keepdims=True))

    mx, mn, n_visible = range_and_count()

    def live_rows(lo, hi, cnt):
        mid = lo + 0.5 * (hi - lo)
        splittable = jnp.logical_and(mid > lo, mid < hi)
        return jnp.where(jnp.logical_and(cnt > n_sel, splittable), 1.0, 0.0)

    def halve(lo, hi, cnt):
        mid = lo + 0.5 * (hi - lo)
        midb = jnp.broadcast_to(mid, (rows, LANES))
        c_mid = count(lambda x, rs: x >= midb[rs])
        enough = c_mid >= n_sel
        return jnp.where(enough, mid, lo), jnp.where(enough, hi, mid), jnp.where(enough, c_mid, cnt)

    def close_tied(lo, hi, cnt):
        lob = jnp.broadcast_to(lo, (rows, LANES))
        low = jnp.min(lane_fold(lambda acc, x, rs: jnp.minimum(acc, jnp.where(x >= lob[rs], x, pos_inf)), pos_inf),
                      axis=1, keepdims=True)
        lowb = jnp.broadcast_to(low, (rows, LANES))
        tied = count(lambda x, rs: x > lowb[rs]) < n_sel
        return lo, jnp.where(tied, lo, hi), cnt

    def bisect(carry):
        it, _, lo, hi, cnt = carry
        check = jnp.logical_and(it >= TIE_CHECK_FIRST, (it - TIE_CHECK_FIRST) % TIE_CHECK_EVERY == 0)
        lo, hi, cnt = lax.cond(check, close_tied, halve, lo, hi, cnt)
        return it + 1, jnp.max(live_rows(lo, hi, cnt)), lo, hi, cnt

    def unfinished(carry):
        return jnp.logical_and(carry[0] < BISECT_ITERS, carry[1] > 0.0)

    _, _, lo, _, cnt = lax.while_loop(
        unfinished, bisect, (jnp.int32(0), jnp.max(live_rows(mn, mx, n_visible)), mn, mx, n_visible))
    lob = jnp.broadcast_to(lo, (rows, LANES))
    has_ties = jnp.max(cnt) > n_sel

    @pl.when(jnp.logical_not(has_ties))
    def _():
        def body(c, _):
            base = pl.multiple_of(c * k_walk, k_walk)
            for t in range(tiles):
                col = pl.ds(base + t * LANES, LANES)
                sc_ref[:, col] = jnp.where(sc_ref[:, col] >= lob, on, off)
            return 0
        lax.fori_loop(0, n_walk, body, 0)

    @pl.when(has_ties)
    def _():
        thr = jnp.min(lane_fold(lambda acc, x, rs: jnp.minimum(acc, jnp.where(x >= lob[rs], x, pos_inf)), pos_inf),
                      axis=1, keepdims=True)
        thrb = jnp.broadcast_to(thr, (rows, LANES))
        need = n_sel - count(lambda x, rs: x > thrb[rs])
        upper = (lax.broadcasted_iota(jnp.int32, (kc, kc), 0) <= lax.broadcasted_iota(jnp.int32, (kc, kc), 1))
        upper = jnp.where(upper, 1.0, 0.0).astype(BF16)

        def body(c, seen):
            col = pl.ds(pl.multiple_of(c * kc, kc), kc)
            x = sc_ref[:, col]
            eq = jnp.where(x == thr, 1.0, 0.0)
            rank = _dot(eq.astype(BF16), upper) + seen
            keep_eq = jnp.where(rank <= need, eq, 0.0)
            sc_ref[:, col] = jnp.where(x > thr, on, jnp.where(keep_eq > 0.5, on, off))
            return seen + jnp.sum(eq, axis=1, keepdims=True)
        lax.fori_loop(0, nkc, body, jnp.zeros((rows, 1), F32))


def _dsa_body(qi_ref, kw_ref, ki2_ref, q_ref, k_ref, v_ref, out_ref, sc_ref, wb_ref, qis_ref,
              qs_ref, s_buf, stat_ref, lsum_ref, acc_ref, *, tq, kc, ks, kb, n_sel):
    i = pl.program_id(1)
    nkc = ((i + 1) * tq + kc - 1) // kc

    kw = kw_ref[...]
    for h in range(H_IDX):
        wb_ref[h] = jnp.broadcast_to(kw[:, D_IDX + h:D_IDX + h + 1] * (H_IDX ** -0.5), (tq, LANES))
        j, parity = divmod(h, 2)
        qi_pair = qi_ref[:, j * LANES:(j + 1) * LANES]
        qis_ref[h * tq:(h + 1) * tq, :] = jnp.where(_pair_mask(qi_pair.shape, parity), qi_pair, jnp.zeros_like(qi_pair))
    qpos = i * tq + lax.broadcasted_iota(jnp.int32, (tq, LANES), 0)
    kiota = lax.broadcasted_iota(jnp.int32, (tq, LANES), 1)

    def score_chunk(c, _):
        base = pl.multiple_of(c * ks, ks)
        d = _dot_nt(qis_ref[...], ki2_ref[pl.ds(base, ks), :])
        for t in range(ks // LANES):
            acc = None
            for h in range(H_IDX):
                term = jnp.maximum(d[h * tq:(h + 1) * tq, t * LANES:(t + 1) * LANES], 0.0) * wb_ref[h]
                acc = term if acc is None else acc + term
            kpos = base + t * LANES + kiota
            sc_ref[:, pl.ds(base + t * LANES, LANES)] = jnp.where(kpos <= qpos, acc, NEG_INF)
        return 0

    n_big = ((i + 1) * tq + kb - 1) // kb
    n_score = ((i + 1) * tq + ks - 1) // ks
    lax.fori_loop(0, n_score, score_chunk, 0)

    def fill_hidden(c, _):
        sc_ref[:, pl.ds(pl.multiple_of(c * ks, ks), ks)] = jnp.full((tq, ks), NEG_INF, F32)
        return 0

    lax.fori_loop(n_score, n_big * (kb // ks), fill_hidden, 0)
    _topk_to_bias(sc_ref, nkc, kc, n_sel)

    _stack_pair_queries(q_ref, qs_ref, tq)

    def logits(j, c, raw, masked):
        bias = sc_ref[:, pl.ds(pl.multiple_of(c * kb, kb), kb)]
        return raw + jnp.concatenate([bias, bias], axis=0)

    _attend_pairs(qs_ref, k_ref, v_ref, out_ref, s_buf, stat_ref, lsum_ref, acc_ref, tq=tq, kb=kb,
                  n_plain=n_big, n_chunks=n_big, logits_fn=logits)


def _dsa_prompt(qi, kw, ki2, q, kb, vb, n, s, tq, kc, kbig):
    hb = q.shape[1]
    nq = s // tq
    n_sel = min(TOPK_MAX, s // 4)
    ks = min(kc, 2 * LANES)
    body = functools.partial(_dsa_body, tq=tq, kc=kc, ks=ks, kb=kbig, n_sel=n_sel)
    scratch = [pltpu.VMEM((tq, s), F32), pltpu.VMEM((H_IDX, tq, LANES), F32), pltpu.VMEM((H_IDX * tq, LANES), BF16)]
    scratch += _pair_scratch(hb // LANES, tq, s)
    one = pl.Buffered(1)
    qrow = lambda b, i: (b * nq + i, 0)
    return pl.pallas_call(
        body, grid=(n, nq),
        in_specs=[pl.BlockSpec((tq, hb), qrow), pl.BlockSpec((tq, LANES), qrow),
                  pl.BlockSpec((s, LANES), lambda b, i: (b, 0), pipeline_mode=one),
                  pl.BlockSpec((tq, hb), qrow),
                  pl.BlockSpec((s, hb), lambda b, i: (b, 0), pipeline_mode=one),
                  pl.BlockSpec((s, hb), lambda b, i: (b, 0), pipeline_mode=one)],
        out_specs=pl.BlockSpec((tq, hb), qrow),
        out_shape=jax.ShapeDtypeStruct((n * s, hb), BF16),
        scratch_shapes=scratch, compiler_params=_params(), name="dsa_prompt")(qi, kw, ki2, q, kb, vb)


def _page_specs(pool, li, pages_per_step, n_pages):
    def spec(r):
        def index(n, p, pt):
            return (li, pt[n, jnp.minimum(p * pages_per_step + r, n_pages - 1)]) + (0,) * (pool.ndim - 2)
        return pl.BlockSpec((1, 1) + pool.shape[2:], index)
    return [spec(r) for r in range(pages_per_step)]


def _idx_scores_body(pt_ref, qi_ref, w_ref, *rest, pages_per_step, n_main, page, t_new):
    page_refs = rest[:pages_per_step]
    new_ref, out_ref = rest[pages_per_step], rest[pages_per_step + 1]
    p = pl.program_id(1)
    qi = qi_ref[0]
    w = w_ref[0] * (H_IDX ** -0.5)

    def scores(ki_t):
        d = _dot(qi, ki_t.astype(BF16))
        contrib = jnp.maximum(d, 0.0) * w
        return jnp.sum(contrib.reshape(t_new, H_IDX, page), axis=1)

    @pl.when(p < n_main)
    def _():
        for r in range(pages_per_step):
            off = pl.multiple_of((p * pages_per_step + r) * page, page)
            out_ref[0, :, pl.ds(off, page)] = scores(page_refs[r][0, 0])

    @pl.when(p == n_main)
    def _():
        sc = scores(new_ref[0])
        t = lax.broadcasted_iota(jnp.int32, sc.shape, 0)
        j = lax.broadcasted_iota(jnp.int32, sc.shape, 1)
        out_ref[0, :, n_main * pages_per_step * page:] = jnp.where(j <= t, sc, NEG_INF)


def _idx_scores_sample(page_table, qi_rows, w_rows, ki_pool, li, ki_new_pages, pages_per_step):
    nseq, n_pages = page_table.shape
    page = ki_pool.shape[3]
    t_new = qi_rows.shape[1] // H_IDX
    n_main = n_pages // pages_per_step
    length = (n_pages + 1) * page
    body = functools.partial(_idx_scores_body, pages_per_step=pages_per_step, n_main=n_main, page=page, t_new=t_new)
    per_seq = lambda n, p, pt: (n, 0, 0)
    grid_spec = pltpu.PrefetchScalarGridSpec(
        num_scalar_prefetch=1, grid=(nseq, n_main + 1),
        in_specs=[pl.BlockSpec((1,) + qi_rows.shape[1:], per_seq), pl.BlockSpec((1,) + w_rows.shape[1:], per_seq)]
        + _page_specs(ki_pool, li, pages_per_step, n_pages)
        + [pl.BlockSpec((1,) + ki_new_pages.shape[1:], per_seq)],
        out_specs=pl.BlockSpec((1, t_new, length), per_seq))
    return pl.pallas_call(
        body, grid_spec=grid_spec, out_shape=jax.ShapeDtypeStruct((nseq, t_new, length), F32),
        compiler_params=_params(), name="idx_scores_sample")(
            page_table, qi_rows, w_rows, *([ki_pool] * pages_per_step), ki_new_pages)


def _topk_rows_body(sc_ref, out_ref, *, kc, n_sel):
    out_ref[...] = sc_ref[...]
    _topk_to_bias(out_ref, out_ref.shape[1] // kc, kc, n_sel, on=1.0, off=0.0)


def _topk_rows(scores, n_sel, rows, kc):
    r, length = scores.shape
    body = functools.partial(_topk_rows_body, kc=kc, n_sel=n_sel)
    return pl.pallas_call(
        body, grid=(r // rows,), in_specs=[pl.BlockSpec((rows, length), lambda i: (i, 0))],
        out_specs=pl.BlockSpec((rows, length), lambda i: (i, 0)),
        out_shape=jax.ShapeDtypeStruct((r, length), F32), compiler_params=_params(), name="topk_rows")(scores)


def _split3(x):
    hi = x.astype(BF16)
    r1 = x - hi.astype(F32)
    mid = r1.astype(BF16)
    lo = (r1 - mid.astype(F32)).astype(BF16)
    return hi, mid, lo


def _decay_bias_body(pt_ref, *rest, pages_per_step, n_pages, n_heads):
    page_refs = rest[:pages_per_step]
    new_ref, out_ref, lf_ref = rest[pages_per_step:]
    p = pl.program_id(1)
    n_rows, page = lf_ref.shape

    @pl.when(p == 0)
    def _():
        lf_ref[n_pages * n_heads:, :] = new_ref[0]

    for r in range(pages_per_step):
        row0 = pl.multiple_of((p * pages_per_step + r) * n_heads, n_heads)
        lf_ref[pl.ds(row0, n_heads), :] = page_refs[r][0, 0]

    @pl.when(p == pl.num_programs(1) - 1)
    def _():
        upto = jnp.where(lax.broadcasted_iota(jnp.int32, (page, page), 0)
                         <= lax.broadcasted_iota(jnp.int32, (page, page), 1), 1.0, 0.0).astype(BF16)
        src = lax.broadcasted_iota(jnp.int32, (n_rows, n_rows), 1)
        dst = lax.broadcasted_iota(jnp.int32, (n_rows, n_rows), 0)
        earlier_pages = jnp.where((src % n_heads) == (dst % n_heads),
                                  jnp.where(src // n_heads < dst // n_heads, 1.0, 0.0), 0.0).astype(BF16)
        cum = sum(_dot(x, upto) for x in _split3(lf_ref[...]))
        totals = jnp.broadcast_to(cum[:, page - 1:], cum.shape)
        cum = cum + sum(_dot(earlier_pages, x) for x in _split3(totals))
        out_ref[0] = -cum


def _decay_bias_sample(page_table, f_pool_t, li, f_new_t):
    nseq, n_pages = page_table.shape
    n_heads, page = f_pool_t.shape[2:]
    pages_per_step = _pages_per_step(n_pages, 32)
    n_rows = (n_pages + 1) * n_heads
    per_seq = lambda n, p, pt: (n, 0, 0)
    body = functools.partial(_decay_bias_body, pages_per_step=pages_per_step, n_pages=n_pages, n_heads=n_heads)
    grid_spec = pltpu.PrefetchScalarGridSpec(
        num_scalar_prefetch=1, grid=(nseq, n_pages // pages_per_step),
        in_specs=_page_specs(f_pool_t, li, pages_per_step, n_pages) + [pl.BlockSpec((1, n_heads, page), per_seq)],
        out_specs=pl.BlockSpec((1, n_rows, page), per_seq), scratch_shapes=[pltpu.VMEM((n_rows, page), F32)])
    return pl.pallas_call(
        body, grid_spec=grid_spec, out_shape=jax.ShapeDtypeStruct((nseq, n_rows, page), F32),
        compiler_params=_params(), name="decay_bias_sample")(
            page_table, *([f_pool_t] * pages_per_step), f_new_t)


def _paged_attn_body(pt_ref, q_ref, *rest, pages_per_step, n_main, page, t_new, n_heads, decay):
    it = iter(rest)
    k_refs = [next(it) for _ in range(pages_per_step)]
    v_refs = [next(it) for _ in range(pages_per_step)]
    knew_ref, vnew_ref = next(it), next(it)
    side_ref = next(it)
    out_ref = next(it)
    m_ref, l_ref, acc_ref = next(it), next(it), next(it)
    p = pl.program_id(1)
    rows = t_new * n_heads
    q = q_ref[0]

    @pl.when(p == 0)
    def _():
        m_ref[...] = jnp.full(m_ref.shape, M_INIT, F32)
        l_ref[...] = jnp.zeros(l_ref.shape, F32)
        acc_ref[...] = jnp.zeros(acc_ref.shape, F32)

    def page_logits(k_t, page_idx, new_page):
        s = _dot(q, k_t.astype(BF16))
        if decay:
            neg_cum = side_ref[0, pl.ds(pl.multiple_of(page_idx * n_heads, n_heads), n_heads), :]
            s = s + jnp.concatenate([neg_cum] * t_new, axis=0)
            if new_page:
                t = lax.broadcasted_iota(jnp.int32, s.shape, 0) // n_heads
                s = jnp.where(lax.broadcasted_iota(jnp.int32, s.shape, 1) <= t, s, NEG_INF)
            return s
        sel = side_ref[0, :, pl.ds(pl.multiple_of(page_idx * page, page), page)]
        sel_rows = jnp.broadcast_to(sel[:, None, :], (t_new, n_heads, page)).reshape(rows, page)
        return jnp.where(sel_rows > 0.5, s, NEG_INF)

    def update(logits, values_t):
        m = m_ref[...]
        m_new = m
        for s in logits:
            m_new = jnp.maximum(m_new, jnp.max(s, axis=1, keepdims=True))
        alpha = jnp.exp(m - m_new)
        l_new = alpha * l_ref[...]
        acc = alpha * acc_ref[...]
        for s, v_t in zip(logits, values_t):
            pr = jnp.exp(s - m_new)
            l_new = l_new + jnp.sum(pr, axis=1, keepdims=True)
            acc = acc + _dot_nt(pr.astype(BF16), v_t.astype(BF16))
        l_ref[...] = l_new
        acc_ref[...] = acc
        m_ref[...] = m_new

    @pl.when(p < n_main)
    def _():
        update([page_logits(k_refs[r][0, 0], p * pages_per_step + r, False) for r in range(pages_per_step)],
               [v_refs[r][0, 0] for r in range(pages_per_step)])

    @pl.when(p == n_main)
    def _():
        update([page_logits(knew_ref[0], n_main * pages_per_step, True)], [vnew_ref[0]])
        o = acc_ref[...] / l_ref[...]
        head_of_row = lax.broadcasted_iota(jnp.int32, o.shape, 0) % n_heads
        head_of_lane = lax.broadcasted_iota(jnp.int32, o.shape, 1) // HEAD_DIM
        o = jnp.where(head_of_row == head_of_lane, o, 0.0)
        out_ref[0] = jnp.sum(o.reshape(t_new, n_heads, o.shape[1]), axis=1).astype(BF16)


def _paged_attn(page_table, q_rows, k_pool_t, v_pool_t, li, k_new_t, v_new_t, pages_per_step,
                neg_cum=None, selected=None):
    nseq, n_pages = page_table.shape
    width, page = k_pool_t.shape[2:]
    n_heads = width // HEAD_DIM
    rows = q_rows.shape[1]
    t_new = rows // n_heads
    n_main = n_pages // pages_per_step
    decay = neg_cum is not None
    side = neg_cum if decay else selected
    per_seq = lambda n, p, pt: (n, 0, 0)
    body = functools.partial(_paged_attn_body, pages_per_step=pages_per_step, n_main=n_main, page=page,
                             t_new=t_new, n_heads=n_heads, decay=decay)
    args = [page_table, q_rows] + [k_pool_t] * pages_per_step + [v_pool_t] * pages_per_step + [k_new_t, v_new_t, side]
    in_specs = [pl.BlockSpec((1,) + q_rows.shape[1:], per_seq)]
    in_specs += _page_specs(k_pool_t, li, pages_per_step, n_pages) + _page_specs(v_pool_t, li, pages_per_step, n_pages)
    in_specs += [pl.BlockSpec((1, width, page), per_seq), pl.BlockSpec((1, width, page), per_seq),
                 pl.BlockSpec((1,) + side.shape[1:], per_seq)]
    scratch = [pltpu.VMEM((rows, 1), F32), pltpu.VMEM((rows, 1), F32), pltpu.VMEM((rows, width), F32)]
    grid_spec = pltpu.PrefetchScalarGridSpec(
        num_scalar_prefetch=1, grid=(nseq, n_main + 1), in_specs=in_specs,
        out_specs=pl.BlockSpec((1, t_new, width), per_seq), scratch_shapes=scratch)
    return pl.pallas_call(
        body, grid_spec=grid_spec, out_shape=jax.ShapeDtypeStruct((nseq, t_new, width), BF16),
        compiler_params=_params(), name="paged_attn_decay" if decay else "paged_attn_select")(*args)


def _rope_tables(pos):
    half = HEAD_DIM // 2
    inv = ROPE_THETA ** (-jnp.arange(half, dtype=F32) / half)
    ang = pos.astype(F32)[:, None] * inv[None, :]
    cos, sin = jnp.cos(ang), jnp.sin(ang)
    reps = LANES // HEAD_DIM
    return jnp.tile(jnp.concatenate([cos, cos], axis=1), (1, reps)), jnp.tile(jnp.concatenate([-sin, sin], axis=1), (1, reps))


def _block_diag(w):
    g, bw, _ = w.shape
    eye = jnp.eye(g, dtype=w.dtype)
    return (w[:, :, None, :] * eye[:, None, :, None]).reshape(g * bw, g * bw)


def _pad_cols(w, width):
    return jnp.pad(w, ((0, 0), (0, width - w.shape[1])))


def _to_time_major(a):
    return jnp.swapaxes(a, 0, 1).reshape((a.shape[0] * a.shape[1],) + a.shape[2:])


def _from_time_major(a, n):
    return jnp.swapaxes(a.reshape((a.shape[0] // n, n) + a.shape[1:]), 0, 1)


def _expand_heads(a, n, n_heads):
    width = a.shape[1]
    per_seq = _from_time_major(a, n)
    mask = (jnp.arange(width)[None, :] // HEAD_DIM) == jnp.arange(n_heads)[:, None]
    out = jnp.where(mask[None, None], per_seq[:, :, None, :], jnp.zeros((), a.dtype))
    return out.reshape(n, -1, width)


def _new_pages_t(a, n, page):
    return jnp.swapaxes(_new_pages(a, n, page), 1, 2)


def _key_minor(pool):
    moved = jnp.moveaxis(pool, 2, -1)
    return moved.reshape(moved.shape[:2] + (-1, moved.shape[-1]))


def _new_pages(a, n, page):
    per_seq = _from_time_major(a, n)
    return jnp.pad(per_seq, ((0, 0), (0, page - per_seq.shape[1]), (0, 0)))


def _heads_from_key_minor(a):
    n, width, s = a.shape
    return jnp.transpose(a.reshape(n, width // HEAD_DIM, HEAD_DIM, s), (0, 3, 1, 2))


def _pages_per_step(n_pages, pref):
    while n_pages % pref:
        pref //= 2
    return pref


def _pick_tile(m, pref):
    t = min(m, pref)
    while m % t:
        t //= 2
    return t


def kernel(x_prompt, x_sample, cache_k_b, cache_v_b, cache_kidx_b, state_conv_a, cache_k_c, cache_v_c, cache_logf_c, state_conv_d, state_lru_h, page_table, g_ff1, w_ff1_gate, w_ff1_up, w_ff1_down, g_mix, g_ff2, w_ff2_gate, w_ff2_up, w_ff2_down, w_in_e, conv_a_w, w_out_e, w_in_o, b_f, conv_d_w, conv_d_b, w_rg_a, b_rg_a, w_rg_x, b_rg_x, lam, w_out_o, g_final):
    batch, seq, d_model = x_prompt.shape
    nseq, t_new, _ = x_sample.shape
    depth = g_ff1.shape[0]
    n_pool, page = cache_k_b.shape[1], cache_k_b.shape[2]
    n_pages = page_table.shape[1]
    past = n_pages * page
    wa = conv_a_w.shape[2]
    hb = w_out_e.shape[1] - wa
    wd = conv_d_w.shape[2]
    hc = w_out_o.shape[1] - wd
    n_heads_c = hc // HEAD_DIM
    pages_per_step = _pages_per_step(n_pages, 16)
    idx_pages_per_step = _pages_per_step(n_pages, 32)

    mp, ms = batch * seq, nseq * t_new
    tm_p = _pick_tile(seq, 512)
    xp = x_prompt.reshape(mp, d_model)
    xs = _to_time_major(x_sample)
    cos_p, sin_p = _rope_tables(jnp.tile(jnp.arange(seq), batch))
    cos_s, sin_s = _rope_tables(past + jnp.repeat(jnp.arange(t_new), nseq))
    row2 = lambda v: v.reshape(1, -1)
    bf = lambda w: w.astype(BF16)

    outs_p, outs_s = {}, {}
    for l in range(depth):
        ffn1 = (row2(g_ff1[l]), bf(w_ff1_gate[l]), bf(w_ff1_up[l]), bf(w_ff1_down[l]))
        ffn2 = (row2(g_ff2[l]), bf(w_ff2_gate[l]), bf(w_ff2_up[l]), bf(w_ff2_down[l]))
        xp = _ffn(xp, *ffn1, tm=tm_p)
        xs = _ffn(xs, *ffn1, tm=ms)
        g = row2(g_mix[l])
        if l % 2 == 0:
            e = l // 2
            w = w_in_e[e]
            o_q, o_k, o_v, o_qi, o_kw = 3 * wa, 3 * wa + hb, 3 * wa + 2 * hb, 3 * wa + 3 * hb, 3 * wa + 3 * hb + H_IDX * D_IDX
            weights = (bf(w[:, :o_q]), bf(w[:, o_q:o_k]), bf(w[:, o_k:o_v]), bf(w[:, o_v:o_qi]), bf(w[:, o_qi:o_kw]),
                       bf(_pad_cols(w[:, o_kw:], LANES)))
            w_out = (bf(w_out_e[e][:wa]), bf(w_out_e[e][wa:]))

            hist_p = jnp.zeros((batch, conv_a_w.shape[1] - 1, wa), F32)
            ya, q, k32, kb, v32, vb, qi, kw, ki2, cstate = _proj_even(
                xp, g, weights, cos_p, sin_p, conv_a_w[e], hist_p, tm_p, seq // tm_p, 1, True)
            yb = _dsa_prompt(qi, kw, ki2, q, kb, vb, batch, seq, _pick_tile(seq, 256), _pick_tile(seq, 512),
                             _pick_tile(seq, 1024))
            outs_p.setdefault("k_b", []).append(_heads_from_key_minor(k32))
            outs_p.setdefault("v_b", []).append(_heads_from_key_minor(v32))
            outs_p.setdefault("kidx_b", []).append(kw[:, :D_IDX].reshape(batch, seq, D_IDX))
            outs_p.setdefault("conv_a", []).append(cstate)
            mix_p = (ya, yb, *w_out)

            hist_s = _to_time_major(state_conv_a[e])[None]
            ya, q, k32, kb, v32, vb, qi, kw, ki2, cstate = _proj_even(
                xs, g, weights, cos_s, sin_s, conv_a_w[e], hist_s, ms, 1, nseq, False)
            qi_rows = _from_time_major(qi, nseq).reshape(nseq, t_new * H_IDX, D_IDX)
            w_rows = _from_time_major(kw[:, D_IDX:D_IDX + H_IDX], nseq).reshape(nseq, t_new * H_IDX, 1)
            scores = _idx_scores_sample(page_table, qi_rows, w_rows, _key_minor(cache_kidx_b), e,
                                        _new_pages_t(kw[:, :D_IDX], nseq, page), idx_pages_per_step)
            length = scores.shape[2]
            n_sel = min(TOPK_MAX, (past + t_new) // 4)
            kc = page * 3 if (length // page) % 3 == 0 else page
            picked = _topk_rows(scores.reshape(ms, length), n_sel, _pick_tile(ms, 32), kc).reshape(nseq, t_new, length)
            yb = _paged_attn(page_table, _expand_heads(q, nseq, hb // HEAD_DIM), _key_minor(cache_k_b),
                             _key_minor(cache_v_b), e, _new_pages_t(k32, nseq, page), _new_pages_t(v32, nseq, page),
                             pages_per_step, selected=picked)
            yb = _to_time_major(yb)
            outs_s.setdefault("k_b", []).append(_from_time_major(k32, nseq).reshape(nseq, t_new, hb // HEAD_DIM, HEAD_DIM))
            outs_s.setdefault("v_b", []).append(_from_time_major(v32, nseq).reshape(nseq, t_new, hb // HEAD_DIM, HEAD_DIM))
            outs_s.setdefault("kidx_b", []).append(_from_time_major(kw[:, :D_IDX], nseq))
            outs_s.setdefault("conv_a", []).append(_from_time_major(cstate[0], nseq))
            mix_s = (ya, yb, *w_out)
        else:
            o = l // 2
            w = w_in_o[o]
            o_f, o_xd, o_gd = 3 * hc, 3 * hc + n_heads_c, 3 * hc + n_heads_c + wd
            weights = (bf(w[:, :o_f]), bf(_pad_cols(w[:, o_f:o_xd], LANES)), bf(w[:, o_xd:o_gd]), bf(w[:, o_gd:]))
            small = (_pad_cols(row2(b_f[o]), LANES), conv_d_w[o], row2(conv_d_b[o]), bf(_block_diag(w_rg_a[o])),
                     row2(b_rg_a[o]), bf(_block_diag(w_rg_x[o])), row2(b_rg_x[o]), row2(lam[o]))
            w_out = (bf(w_out_o[o][:hc]), bf(w_out_o[o][hc:]))

            hist_p = jnp.zeros((batch, conv_d_w.shape[1] - 1, wd), F32)
            h0_p = jnp.zeros((batch, 1, wd), F32)
            q, k32, kb, v32, vb, logf, yd, cstate, hlast = _proj_odd(
                xp, g, weights, small, hist_p, h0_p, tm_p, seq // tm_p, 1, 0, True)
            cum_t = _cumsum_heads(logf, batch, seq, _pick_tile(seq, 512))
            yc = _fox_prompt(q, kb, vb, cum_t, batch, seq, _pick_tile(seq, 512), _pick_tile(seq, 1024))
            outs_p.setdefault("k_c", []).append(_heads_from_key_minor(k32))
            outs_p.setdefault("v_c", []).append(_heads_from_key_minor(v32))
            outs_p.setdefault("logf_c", []).append(logf.reshape(batch, seq, n_heads_c))
            outs_p.setdefault("conv_d", []).append(cstate)
            outs_p.setdefault("lru_h", []).append(hlast[:, 0])
            mix_p = (yc, yd, *w_out)

            hist_s = _to_time_major(state_conv_d[o])[None]
            h0_s = state_lru_h[o][None]
            q, k32, kb, v32, vb, logf, yd, cstate, hlast = _proj_odd(
                xs, g, weights, small, hist_s, h0_s, ms, 1, nseq, past, False)
            neg_cum = _decay_bias_sample(page_table, _key_minor(cache_logf_c), o, _new_pages_t(logf, nseq, page))
            yc = _paged_attn(page_table, _expand_heads(q, nseq, n_heads_c), _key_minor(cache_k_c),
                             _key_minor(cache_v_c), o, _new_pages_t(k32, nseq, page), _new_pages_t(v32, nseq, page),
                             pages_per_step, neg_cum=neg_cum)
            yc = _to_time_major(yc)
            outs_s.setdefault("k_c", []).append(_from_time_major(k32, nseq).reshape(nseq, t_new, n_heads_c, HEAD_DIM))
            outs_s.setdefault("v_c", []).append(_from_time_major(v32, nseq).reshape(nseq, t_new, n_heads_c, HEAD_DIM))
            outs_s.setdefault("logf_c", []).append(_from_time_major(logf, nseq))
            outs_s.setdefault("conv_d", []).append(_from_time_major(cstate[0], nseq))
            outs_s.setdefault("lru_h", []).append(hlast[0])
            mix_s = (yc, yd, *w_out)
        gfin = row2(g_final) if l == depth - 1 else None
        xp = _ffn(xp, *ffn2, tm=tm_p, mix=mix_p, g_final=gfin)
        xs = _ffn(xs, *ffn2, tm=ms, mix=mix_s, g_final=gfin)

    names = ("k_b", "v_b", "kidx_b", "conv_a", "k_c", "v_c", "logf_c", "conv_d", "lru_h")
    st_p = tuple(jnp.stack(outs_p[nm]) for nm in names)
    st_s = tuple(jnp.stack(outs_s[nm]) for nm in names)
    return (xp.reshape(batch, seq, d_model), _from_time_major(xs, nseq)) + st_p + st_s
```

```python
import functools

import jax
import jax.numpy as jnp
import numpy as np
from jax import lax
from jax.experimental import pallas as pl
from jax.experimental.pallas import tpu as pltpu

F32 = jnp.float32
BF16 = jnp.bfloat16

HEAD_DIM = 64
D_IDX = 64
H_IDX = 8
TOPK_MAX = 256
LRU_BLOCKS = 8
RG_C = 8.0
ROPE_THETA = 10000.0
EPS = 1e-6

LANES = 128
SUBLANES = 8
VMEM_LIMIT = 56 * 1024 * 1024
NEG_INF = float("-inf")
M_INIT = -1e30
BISECT_ITERS = 400
TIE_CHECK_FIRST = 20
TIE_CHECK_EVERY = 8
TOPK_ROW_BLOCK = 128

_NT = (((1,), (1,)), ((), ()))


def _params():
    return pltpu.CompilerParams(vmem_limit_bytes=VMEM_LIMIT)


def _const_spec(shape):
    nd = len(shape)
    return pl.BlockSpec(shape, lambda *_: (0,) * nd, pipeline_mode=pl.Buffered(1))


def _dot(a, b):
    return jnp.dot(a, b, preferred_element_type=F32)


def _dot_nt(a, b, precision=None):
    return lax.dot_general(a, b, _NT, preferred_element_type=F32, precision=precision)


def _rmsnorm(x, g):
    ms = jnp.mean(x * x, axis=-1, keepdims=True)
    return x * lax.rsqrt(ms + EPS) * g


def _ffn_body(*refs, n_chunks, fc, has_mix, has_final):
    it = iter(refs)
    x_ref = next(it)
    if has_mix:
        ya_ref, yb_ref, woa_ref, wob_ref = next(it), next(it), next(it), next(it)
    g_ref, wg_ref, wu_ref, wd_ref = next(it), next(it), next(it), next(it)
    if has_final:
        gf_ref = next(it)
    out_ref = next(it)

    x = x_ref[...]
    if has_mix:
        x = x + _dot(ya_ref[...], woa_ref[...]) + _dot(yb_ref[...], wob_ref[...])
    h = _rmsnorm(x, g_ref[...]).astype(BF16)
    acc = None
    for c in range(n_chunks):
        gate = _dot(h, wg_ref[:, c * fc:(c + 1) * fc])
        up = _dot(h, wu_ref[:, c * fc:(c + 1) * fc])
        a = (gate * jax.nn.sigmoid(gate) * up).astype(BF16)
        part = _dot(a, wd_ref[c * fc:(c + 1) * fc, :])
        acc = part if acc is None else acc + part
    x = x + 0.5 * acc
    if has_final:
        x = _rmsnorm(x, gf_ref[...])
    out_ref[...] = x


def _ffn(x, g, wg, wu, wd, tm, mix=None, g_final=None):
    m, d = x.shape
    f = wg.shape[1]
    n_chunks = 2 if (f // 2) % LANES == 0 else 1
    fc = f // n_chunks
    row = lambda i: (i, 0)
    args, specs = [x], [pl.BlockSpec((tm, d), row)]
    if mix is not None:
        ya, yb, woa, wob = mix
        args += [ya, yb, woa, wob]
        specs += [pl.BlockSpec((tm, ya.shape[1]), row), pl.BlockSpec((tm, yb.shape[1]), row),
                  _const_spec(woa.shape), _const_spec(wob.shape)]
    args += [g, wg, wu, wd]
    specs += [_const_spec(g.shape), _const_spec(wg.shape), _const_spec(wu.shape), _const_spec(wd.shape)]
    if g_final is not None:
        args.append(g_final)
        specs.append(_const_spec(g_final.shape))
    body = functools.partial(_ffn_body, n_chunks=n_chunks, fc=fc, has_mix=mix is not None,
                             has_final=g_final is not None)
    return pl.pallas_call(
        body, grid=(m // tm,), in_specs=specs, out_specs=pl.BlockSpec((tm, d), row),
        out_shape=jax.ShapeDtypeStruct((m, d), F32), compiler_params=_params(), name="ffn")(*args)


def _rope_tile(xt, cos, sin_signed):
    lane = lax.broadcasted_iota(jnp.int32, xt.shape, 1)
    first_half = (lane % HEAD_DIM) < (HEAD_DIM // 2)
    partner = jnp.where(first_half, pltpu.roll(xt, LANES - HEAD_DIM // 2, 1), pltpu.roll(xt, HEAD_DIM // 2, 1))
    return xt * cos + partner * sin_signed


def _store_f32(ref, cols, value, key_minor):
    if key_minor:
        ref[0, cols, :] = value.T
    else:
        ref[:, cols] = value


def _kv_out(m, width, tm, tiles_per_seq, key_minor):
    if not key_minor:
        return jax.ShapeDtypeStruct((m, width), F32), pl.BlockSpec((tm, width), lambda i: (i, 0))
    shape = (m // (tm * tiles_per_seq), width, tm * tiles_per_seq)
    return (jax.ShapeDtypeStruct(shape, F32),
            pl.BlockSpec((1, width, tm), lambda i: (i // tiles_per_seq, 0, i % tiles_per_seq)))


def _hist_rows(width, stride):
    need = (width - 1) * stride
    pad = -(-need // SUBLANES) * SUBLANES
    return need, pad


def _causal_conv(u, w_ref, ubuf, hist_ref, state_ref, first_tile, width, stride):
    tm = u.shape[0]
    need, pad = _hist_rows(width, stride)

    @pl.when(first_tile)
    def _():
        ubuf[pad - need:pad, :] = hist_ref[0]

    @pl.when(jnp.logical_not(first_tile))
    def _():
        ubuf[0:pad, :] = ubuf[tm:tm + pad, :]

    ubuf[pad:pad + tm, :] = u
    y = None
    for j in range(width):
        start = pad - (width - 1 - j) * stride
        term = w_ref[j:j + 1, :] * ubuf[start:start + tm, :]
        y = term if y is None else y + term
    state_ref[0] = ubuf[pad + tm - need:pad + tm, :]
    return y


def _proj_even_body(x_ref, g_ref, wc_ref, wq_ref, wk_ref, wv_ref, wqi_ref, wkw_ref, cos_ref, sin_ref,
                    cw_ref, hist_ref,
                    ya_ref, q_ref, k32_ref, kb_ref, v32_ref, vb_ref, qi_ref, kw_ref, ki2_ref, state_ref,
                    ubuf, *, tiles_per_seq, stride, wa, kv_key_minor):
    first_tile = (pl.program_id(0) % tiles_per_seq) == 0
    h = _rmsnorm(x_ref[...], g_ref[...]).astype(BF16)
    cos = cos_ref[...]
    sin = sin_ref[...]
    scale = HEAD_DIM ** -0.5

    zc = _dot(h, wc_ref[...])
    xin, gb, gc = zc[:, :wa], zc[:, wa:2 * wa], zc[:, 2 * wa:]
    yconv = _causal_conv(gc * xin, cw_ref, ubuf, hist_ref, state_ref, first_tile, cw_ref.shape[0], stride)
    ya_ref[...] = (gb * yconv).astype(BF16)

    n_tiles = wq_ref.shape[1] // LANES
    q = _dot(h, wq_ref[...])
    k = _dot(h, wk_ref[...])
    qi = _dot(h, wqi_ref[...])
    for t in range(n_tiles):
        sl = slice(t * LANES, (t + 1) * LANES)
        q_ref[:, sl] = (_rope_tile(q[:, sl], cos, sin) * scale).astype(BF16)
        kt = _rope_tile(k[:, sl], cos, sin)
        _store_f32(k32_ref, sl, kt, kv_key_minor)
        kb_ref[:, sl] = kt.astype(BF16)
        qi_ref[:, sl] = (_rope_tile(qi[:, sl], cos, sin) * (D_IDX ** -0.5)).astype(BF16)
    v = _dot(h, wv_ref[...])
    _store_f32(v32_ref, slice(None), v, kv_key_minor)
    vb_ref[...] = v.astype(BF16)

    kw = _dot(h, wkw_ref[...])
    lane = lax.broadcasted_iota(jnp.int32, kw.shape, 1)
    is_ki = lane < D_IDX
    kw = jnp.where(is_ki, _rope_tile(kw, cos, sin), kw)
    kw_ref[...] = kw
    ki2_ref[...] = jnp.where(is_ki, kw, pltpu.roll(kw, D_IDX, 1)).astype(BF16)


def _proj_even(x, g, w, cos, sin, conv_w, hist, tm, tiles_per_seq, stride, kv_key_minor):
    m, d = x.shape
    wc, wq, wk, wv, wqi, wkw = w
    wa = wc.shape[1] // 3
    hb = wq.shape[1]
    width = conv_w.shape[0]
    need, pad = _hist_rows(width, stride)
    nb = hist.shape[0]
    row = lambda i: (i, 0)
    seq = lambda i: (i // tiles_per_seq, 0, 0)
    in_specs = [pl.BlockSpec((tm, d), row), _const_spec(g.shape)]
    in_specs += [_const_spec(a.shape) for a in (wc, wq, wk, wv, wqi, wkw)]
    in_specs += [pl.BlockSpec((tm, LANES), row), pl.BlockSpec((tm, LANES), row), _const_spec(conv_w.shape),
                 pl.BlockSpec((1, need, wa), seq)]
    out_shape = [jax.ShapeDtypeStruct((m, wa), BF16), jax.ShapeDtypeStruct((m, hb), BF16),
                 jax.ShapeDtypeStruct((m, hb), F32), jax.ShapeDtypeStruct((m, hb), BF16),
                 jax.ShapeDtypeStruct((m, hb), F32), jax.ShapeDtypeStruct((m, hb), BF16),
                 jax.ShapeDtypeStruct((m, hb), BF16), jax.ShapeDtypeStruct((m, LANES), F32),
                 jax.ShapeDtypeStruct((m, LANES), BF16), jax.ShapeDtypeStruct((nb, need, wa), F32)]
    out_specs = [pl.BlockSpec((tm, wa), row)] + [pl.BlockSpec((tm, hb), row)] * 6
    out_specs += [pl.BlockSpec((tm, LANES), row), pl.BlockSpec((tm, LANES), row), pl.BlockSpec((1, need, wa), seq)]
    for pos in (2, 4):
        out_shape[pos], out_specs[pos] = _kv_out(m, hb, tm, tiles_per_seq, kv_key_minor)
    body = functools.partial(_proj_even_body, tiles_per_seq=tiles_per_seq, stride=stride, wa=wa,
                             kv_key_minor=kv_key_minor)
    return pl.pallas_call(
        body, grid=(m // tm,), in_specs=in_specs, out_specs=out_specs, out_shape=out_shape,
        scratch_shapes=[pltpu.VMEM((pad + tm, wa), F32)], compiler_params=_params(), name="proj_even")(
            x, g, wc, wq, wk, wv, wqi, wkw, cos, sin, conv_w, hist)


def _shift_rows(x, d, fill):
    if d % SUBLANES == 0:
        return jnp.concatenate([jnp.full((d, x.shape[1]), fill, x.dtype), x[:x.shape[0] - d]], axis=0)
    row = lax.broadcasted_iota(jnp.int32, x.shape, 0)
    return jnp.where(row >= d, pltpu.roll(x, d, 0), fill)


def _proj_odd_body(x_ref, g_ref, wqkv_ref, wf_ref, wxd_ref, wgd_ref, bf_ref, cw_ref, cb_ref, wra_ref, bra_ref,
                   wrx_ref, brx_ref, lam_ref, hist_ref, h0_ref,
                   q_ref, k32_ref, kb_ref, v32_ref, vb_ref, logf_ref, yd_ref, state_ref, hlast_ref,
                   ubuf, hcar, *, tiles_per_seq, stride, pos_base, hc, kv_key_minor):
    tile_in_seq = pl.program_id(0) % tiles_per_seq
    first_tile = tile_in_seq == 0
    tm = x_ref.shape[0]
    h = _rmsnorm(x_ref[...], g_ref[...]).astype(BF16)

    z = _dot(h, wqkv_ref[...])
    q_ref[...] = (z[:, :hc] * (HEAD_DIM ** -0.5)).astype(BF16)
    k = z[:, hc:2 * hc]
    v = z[:, 2 * hc:]
    _store_f32(k32_ref, slice(None), k, kv_key_minor)
    kb_ref[...] = k.astype(BF16)
    _store_f32(v32_ref, slice(None), v, kv_key_minor)
    vb_ref[...] = v.astype(BF16)

    zf = _dot(h, wf_ref[...]) + bf_ref[...]
    logf = jnp.minimum(zf, 0.0) - jnp.log1p(jnp.exp(-jnp.abs(zf)))
    logf_ref[...] = logf[:, :logf_ref.shape[1]]

    xd = _dot(h, wxd_ref[...])
    gd = _dot(h, wgd_ref[...])
    xc = _causal_conv(xd, cw_ref, ubuf, hist_ref, state_ref, first_tile, cw_ref.shape[0], stride) + cb_ref[...]
    xcb = xc.astype(BF16)
    r = jax.nn.sigmoid(_dot(xcb, wra_ref[...]) + bra_ref[...])
    ig = jax.nn.sigmoid(_dot(xcb, wrx_ref[...]) + brx_ref[...])
    lam = lam_ref[...]
    softplus_neg_lam = jnp.maximum(-lam, 0.0) + jnp.log1p(jnp.exp(-jnp.abs(lam)))
    log_a = -RG_C * r * softplus_neg_lam
    a = jnp.exp(log_a)
    row = lax.broadcasted_iota(jnp.int32, (tm, 1), 0)
    pos = pos_base + tile_in_seq * (tm // stride) + row // stride
    mult = jnp.where(pos == 0, 1.0, jnp.sqrt(1.0 - jnp.exp(2.0 * log_a)))
    b = mult * ig * xc

    d = stride
    while d < tm:
        a_prev = _shift_rows(a, d, 1.0)
        b_prev = _shift_rows(b, d, 0.0)
        b = a * b_prev + b
        a = a * a_prev
        d *= 2

    @pl.when(first_tile)
    def _():
        hcar[...] = h0_ref[0]

    carry = hcar[...]
    if stride == 1:
        carry_rows = carry
    else:
        carry_rows = jnp.concatenate([carry] * (tm // stride), axis=0)
    hs = a * carry_rows + b
    hcar[...] = hs[tm - stride:, :]
    hlast_ref[0] = hs[tm - stride:, :]
    yd_ref[...] = (hs * jax.nn.gelu(gd)).astype(BF16)


def _proj_odd(x, g, w, small, hist, h0, tm, tiles_per_seq, stride, pos_base, kv_key_minor):
    m, d = x.shape
    wqkv, wf, wxd, wgd = w
    bf, conv_w, conv_b, wra, bra, wrx, brx, lam = small
    hc = wqkv.shape[1] // 3
    wd_ = wxd.shape[1]
    n_heads = hc // HEAD_DIM
    width = conv_w.shape[0]
    need, pad = _hist_rows(width, stride)
    nb = hist.shape[0]
    row = lambda i: (i, 0)
    seq = lambda i: (i // tiles_per_seq, 0, 0)
    ins = [x, g, wqkv, wf, wxd, wgd, bf, conv_w, conv_b, wra, bra, wrx, brx, lam, hist, h0]
    in_specs = [pl.BlockSpec((tm, d), row)] + [_const_spec(a.shape) for a in ins[1:14]]
    in_specs += [pl.BlockSpec((1, need, wd_), seq), pl.BlockSpec((1, stride, wd_), seq)]
    out_shape = [jax.ShapeDtypeStruct((m, hc), BF16), jax.ShapeDtypeStruct((m, hc), F32),
                 jax.ShapeDtypeStruct((m, hc), BF16), jax.ShapeDtypeStruct((m, hc), F32),
                 jax.ShapeDtypeStruct((m, hc), BF16), jax.ShapeDtypeStruct((m, n_heads), F32),
                 jax.ShapeDtypeStruct((m, wd_), BF16), jax.ShapeDtypeStruct((nb, need, wd_), F32),
                 jax.ShapeDtypeStruct((nb, stride, wd_), F32)]
    out_specs = [pl.BlockSpec((tm, hc), row)] * 5 + [pl.BlockSpec((tm, n_heads), row), pl.BlockSpec((tm, wd_), row),
                                                      pl.BlockSpec((1, need, wd_), seq),
                                                      pl.BlockSpec((1, stride, wd_), seq)]
    for pos in (1, 3):
        out_shape[pos], out_specs[pos] = _kv_out(m, hc, tm, tiles_per_seq, kv_key_minor)
    body = functools.partial(_proj_odd_body, tiles_per_seq=tiles_per_seq, stride=stride, pos_base=pos_base, hc=hc,
                             kv_key_minor=kv_key_minor)
    return pl.pallas_call(
        body, grid=(m // tm,), in_specs=in_specs, out_specs=out_specs, out_shape=out_shape,
        scratch_shapes=[pltpu.VMEM((pad + tm, wd_), F32), pltpu.VMEM((stride, wd_), F32)],
        compiler_params=_params(), name="proj_odd")(*ins)


def _cumsum_lanes(lf, carry):
    t, nh = lf.shape
    eye = (lax.broadcasted_iota(jnp.int32, (nh, nh), 0) == lax.broadcasted_iota(jnp.int32, (nh, nh), 1)).astype(F32)
    lf_t = _dot_nt(eye, lf, precision=lax.Precision.HIGHEST)
    tri = (lax.broadcasted_iota(jnp.int32, (t, t), 1) <= lax.broadcasted_iota(jnp.int32, (t, t), 0)).astype(F32)
    return _dot_nt(lf_t, tri, precision=lax.Precision.HIGHEST) + carry


def _cumsum_body(lf_ref, out_ref, carry):
    @pl.when(pl.program_id(1) == 0)
    def _():
        carry[...] = jnp.zeros_like(carry)

    cum = _cumsum_lanes(lf_ref[...], carry[:, 0:1])
    out_ref[0] = cum
    carry[...] = jnp.broadcast_to(cum[:, cum.shape[1] - 1:], carry.shape)


def _cumsum_heads(logf, n, s, tc):
    nh = logf.shape[1]
    tiles = s // tc
    return pl.pallas_call(
        _cumsum_body, grid=(n, tiles),
        in_specs=[pl.BlockSpec((tc, nh), lambda b, i: (b * tiles + i, 0))],
        out_specs=pl.BlockSpec((1, nh, tc), lambda b, i: (b, 0, i)),
        out_shape=jax.ShapeDtypeStruct((n, nh, s), F32),
        scratch_shapes=[pltpu.VMEM((nh, LANES), F32)], compiler_params=_params(), name="cumsum")(logf)


def _pair_mask(shape, parity):
    lane = lax.broadcasted_iota(jnp.int32, shape, 1)
    return (lane >= HEAD_DIM) == (parity == 1)


def _stack_pair_queries(q_ref, qs_ref, tq):
    for j in range(q_ref.shape[1] // LANES):
        q_pair = q_ref[:, j * LANES:(j + 1) * LANES]
        zero = jnp.zeros_like(q_pair)
        qs_ref[j, 0:tq, :] = jnp.where(_pair_mask(q_pair.shape, 0), q_pair, zero)
        qs_ref[j, tq:2 * tq, :] = jnp.where(_pair_mask(q_pair.shape, 1), q_pair, zero)


def _attend_pairs(qs_ref, k_ref, v_ref, out_ref, s_buf, stat_ref, lsum_ref, acc_ref, *, tq, kb, n_plain, n_chunks,
                  logits_fn):
    tiles = kb // LANES
    n_pairs = qs_ref.shape[0]

    def col_of(j):
        start = j * LANES
        return pl.ds(start if isinstance(j, int) else pl.multiple_of(start, LANES), LANES)

    def pass1(j, c, masked):
        rows = pl.ds(pl.multiple_of(c * kb, kb), kb)
        s = logits_fn(j, c, _dot_nt(qs_ref[j], k_ref[rows, col_of(j)]), masked)
        s_buf[:, rows] = s
        mt = s[:, 0:LANES]
        for t in range(1, tiles):
            mt = jnp.maximum(mt, s[:, t * LANES:(t + 1) * LANES])
        stat_ref[j % 2] = jnp.maximum(stat_ref[j % 2], mt)

    def pass2(j, c):
        rows = pl.ds(pl.multiple_of(c * kb, kb), kb)
        m_rows = jnp.concatenate([stat_ref[j % 2]] * tiles, axis=1)
        p = jnp.exp(s_buf[:, rows] - m_rows)
        part = p[:, 0:LANES]
        for t in range(1, tiles):
            part = part + p[:, t * LANES:(t + 1) * LANES]
        lsum_ref[...] += part
        acc_ref[...] += _dot(p.astype(BF16), v_ref[rows, col_of(j)])

    def sweep(first, second):
        def body(c, _, masked):
            if second is not None:
                pass2(second, c)
            if first is not None:
                pass1(first, c, masked)
            return 0
        if first is None:
            lax.fori_loop(0, n_chunks, functools.partial(body, masked=False), 0)
        else:
            lax.fori_loop(0, n_plain, functools.partial(body, masked=False), 0)
            lax.fori_loop(n_plain, n_chunks, functools.partial(body, masked=True), 0)

    def open_pair(j):
        stat_ref[j % 2] = jnp.full(stat_ref.shape[1:], NEG_INF, F32)

    def close_max(j):
        m = jnp.max(stat_ref[j % 2], axis=1, keepdims=True)
        stat_ref[j % 2] = jnp.broadcast_to(m, stat_ref.shape[1:])
        lsum_ref[...] = jnp.zeros(lsum_ref.shape, F32)
        acc_ref[...] = jnp.zeros(acc_ref.shape, F32)

    def finish(j):
        o = acc_ref[...] / jnp.sum(lsum_ref[...], axis=1, keepdims=True)
        even, odd = o[0:tq], o[tq:2 * tq]
        out_ref[:, col_of(j)] = jnp.where(_pair_mask(even.shape, 1), odd, even).astype(BF16)

    open_pair(0)
    sweep(0, None)
    close_max(0)

    def stage(j, _):
        open_pair(j)
        sweep(j, j - 1)
        finish(j - 1)
        close_max(j)
        return 0

    lax.fori_loop(1, n_pairs, stage, 0)
    sweep(None, n_pairs - 1)
    finish(n_pairs - 1)


def _pair_scratch(n_pairs, tq, s):
    return [pltpu.VMEM((n_pairs, 2 * tq, LANES), BF16), pltpu.VMEM((2 * tq, s), F32),
            pltpu.VMEM((2, 2 * tq, LANES), F32), pltpu.VMEM((2 * tq, LANES), F32), pltpu.VMEM((2 * tq, LANES), F32)]


def _fox_body(q_ref, k_ref, v_ref, cum_ref, out_ref, qs_ref, s_buf, stat_ref, lsum_ref, acc_ref, *, tq, kc):
    i = pl.program_id(1)
    n_full = (i * tq) // kc
    n_tot = ((i + 1) * tq + kc - 1) // kc
    _stack_pair_queries(q_ref, qs_ref, tq)

    def logits(j, c, raw, masked):
        rows = pl.ds(pl.multiple_of(c * kc, kc), kc)
        halves = []
        for parity in (0, 1):
            sh = raw[parity * tq:(parity + 1) * tq] - cum_ref[0, pl.ds(2 * j + parity, 1), rows]
            if masked:
                qpos = i * tq + lax.broadcasted_iota(jnp.int32, (tq, kc), 0)
                kpos = c * kc + lax.broadcasted_iota(jnp.int32, (tq, kc), 1)
                sh = jnp.where(kpos <= qpos, sh, NEG_INF)
            halves.append(sh)
        return jnp.concatenate(halves, axis=0)

    _attend_pairs(qs_ref, k_ref, v_ref, out_ref, s_buf, stat_ref, lsum_ref, acc_ref, tq=tq, kb=kc,
                  n_plain=n_full, n_chunks=n_tot, logits_fn=logits)


def _fox_prompt(q, kb, vb, cum_t, n, s, tq, kc):
    hc = q.shape[1]
    nh = cum_t.shape[1]
    nq = s // tq
    body = functools.partial(_fox_body, tq=tq, kc=kc)
    scratch = _pair_scratch(hc // LANES, tq, s)
    one = pl.Buffered(1)
    return pl.pallas_call(
        body, grid=(n, nq),
        in_specs=[pl.BlockSpec((tq, hc), lambda b, i: (b * nq + i, 0)),
                  pl.BlockSpec((s, hc), lambda b, i: (b, 0), pipeline_mode=one),
                  pl.BlockSpec((s, hc), lambda b, i: (b, 0), pipeline_mode=one),
                  pl.BlockSpec((1, nh, s), lambda b, i: (b, 0, 0), pipeline_mode=one)],
        out_specs=pl.BlockSpec((tq, hc), lambda b, i: (b * nq + i, 0)),
        out_shape=jax.ShapeDtypeStruct((n * s, hc), BF16), scratch_shapes=scratch,
        compiler_params=_params(), name="fox_prompt")(
            q, kb, vb, cum_t)


def _topk_to_bias(sc_ref, nkc, kc, n_sel, on=0.0, off=NEG_INF):
    rows = sc_ref.shape[0]
    n_walk, k_walk = nkc, kc
    tiles = k_walk // LANES
    pos_inf = float("inf")

    rb = min(rows, TOPK_ROW_BLOCK)

    def lane_fold(fn, init):
        parts = []
        for r0 in range(0, rows, rb):
            rs = slice(r0, r0 + rb)

            def body(c, acc, rs=rs):
                base = pl.multiple_of(c * k_walk, k_walk)
                for t in range(tiles):
                    acc = fn(acc, sc_ref[rs, pl.ds(base + t * LANES, LANES)], rs)
                return acc
            parts.append(lax.fori_loop(0, n_walk, body, jnp.full((rb, LANES), init, F32)))
        return jnp.concatenate(parts, axis=0)

    def count(pred_fn):
        part = lane_fold(lambda acc, x, rs: acc + jnp.where(pred_fn(x, rs), 1.0, 0.0), 0.0)
        return jnp.sum(part, axis=1, keepdims=True)

    def range_and_count():
        parts = []
        for r0 in range(0, rows, rb):
            def body(c, acc, r0=r0):
                hi_, lo_, n_ = acc
                base = pl.multiple_of(c * k_walk, k_walk)
                for t in range(tiles):
                    x = sc_ref[r0:r0 + rb, pl.ds(base + t * LANES, LANES)]
                    visible = x > NEG_INF
                    hi_ = jnp.maximum(hi_, x)
                    lo_ = jnp.minimum(lo_, jnp.where(visible, x, pos_inf))
                    n_ = n_ + jnp.where(visible, 1.0, 0.0)
                return hi_, lo_, n_
            seed = tuple(jnp.full((rb, LANES), v, F32) for v in (NEG_INF, pos_inf, 0.0))
            parts.append(lax.fori_loop(0, n_walk, body, seed))
        hi_, lo_, n_ = (jnp.concatenate([p[i] for p in parts], axis=0) for i in range(3))
        return (jnp.max(hi_, axis=1, keepdims=True), jnp.min(lo_, axis=1, keepdims=True),
                jnp.sum(n_, axis=1, keepdims=True))

    mx, mn, n_visible = range_and_count()

    def live_rows(lo, hi, cnt):
        mid = lo + 0.5 * (hi - lo)
        splittable = jnp.logical_and(mid > lo, mid < hi)
        return jnp.where(jnp.logical_and(cnt > n_sel, splittable), 1.0, 0.0)

    def halve(lo, hi, cnt):
        mid = lo + 0.5 * (hi - lo)
        midb = jnp.broadcast_to(mid, (rows, LANES))
        c_mid = count(lambda x, rs: x >= midb[rs])
        enough = c_mid >= n_sel
        return jnp.where(enough, mid, lo), jnp.where(enough, hi, mid), jnp.where(enough, c_mid, cnt)

    def close_tied(lo, hi, cnt):
        lob = jnp.broadcast_to(lo, (rows, LANES))
        low = jnp.min(lane_fold(lambda acc, x, rs: jnp.minimum(acc, jnp.where(x >= lob[rs], x, pos_inf)), pos_inf),
                      axis=1, keepdims=True)
        lowb = jnp.broadcast_to(low, (rows, LANES))
        tied = count(lambda x, rs: x > lowb[rs]) < n_sel
        return lo, jnp.where(tied, lo, hi), cnt

    def bisect(carry):
        it, _, lo, hi, cnt = carry
        check = jnp.logical_and(it >= TIE_CHECK_FIRST, (it - TIE_CHECK_FIRST) % TIE_CHECK_EVERY == 0)
        lo, hi, cnt = lax.cond(check, close_tied, halve, lo, hi, cnt)
        return it + 1, jnp.max(live_rows(lo, hi, cnt)), lo, hi, cnt

    def unfinished(carry):
        return jnp.logical_and(carry[0] < BISECT_ITERS, carry[1] > 0.0)

    _, _, lo, _, cnt = lax.while_loop(
        unfinished, bisect, (jnp.int32(0), jnp.max(live_rows(mn, mx, n_visible)), mn, mx, n_visible))
    lob = jnp.broadcast_to(lo, (rows, LANES))
    has_ties = jnp.max(cnt) > n_sel

    @pl.when(jnp.logical_not(has_ties))
    def _():
        def body(c, _):
            base = pl.multiple_of(c * k_walk, k_walk)
            for t in range(tiles):
                col = pl.ds(base + t * LANES, LANES)
                sc_ref[:, col] = jnp.where(sc_ref[:, col] >= lob, on, off)
            return 0
        lax.fori_loop(0, n_walk, body, 0)

    @pl.when(has_ties)
    def _():
        thr = jnp.min(lane_fold(lambda acc, x, rs: jnp.minimum(acc, jnp.where(x >= lob[rs], x, pos_inf)), pos_inf),
                      axis=1, keepdims=True)
        thrb = jnp.broadcast_to(thr, (rows, LANES))
        need = n_sel - count(lambda x, rs: x > thrb[rs])
        upper = (lax.broadcasted_iota(jnp.int32, (kc, kc), 0) <= lax.broadcasted_iota(jnp.int32, (kc, kc), 1))
        upper = jnp.where(upper, 1.0, 0.0).astype(BF16)

        def body(c, seen):
            col = pl.ds(pl.multiple_of(c * kc, kc), kc)
            x = sc_ref[:, col]
            eq = jnp.where(x == thr, 1.0, 0.0)
            rank = _dot(eq.astype(BF16), upper) + seen
            keep_eq = jnp.where(rank <= need, eq, 0.0)
            sc_ref[:, col] = jnp.where(x > thr, on, jnp.where(keep_eq > 0.5, on, off))
            return seen + jnp.sum(eq, axis=1, keepdims=True)
        lax.fori_loop(0, nkc, body, jnp.zeros((rows, 1), F32))


def _dsa_body(qi_ref, kw_ref, ki2_ref, q_ref, k_ref, v_ref, out_ref, sc_ref, wb_ref, qis_ref,
              qs_ref, s_buf, stat_ref, lsum_ref, acc_ref, *, tq, kc, ks, kb, n_sel):
    i = pl.program_id(1)
    nkc = ((i + 1) * tq + kc - 1) // kc

    kw = kw_ref[...]
    for h in range(H_IDX):
        wb_ref[h] = jnp.broadcast_to(kw[:, D_IDX + h:D_IDX + h + 1] * (H_IDX ** -0.5), (tq, LANES))
        j, parity = divmod(h, 2)
        qi_pair = qi_ref[:, j * LANES:(j + 1) * LANES]
        qis_ref[h * tq:(h + 1) * tq, :] = jnp.where(_pair_mask(qi_pair.shape, parity), qi_pair, jnp.zeros_like(qi_pair))
    qpos = i * tq + lax.broadcasted_iota(jnp.int32, (tq, LANES), 0)
    kiota = lax.broadcasted_iota(jnp.int32, (tq, LANES), 1)

    def score_chunk(c, _):
        base = pl.multiple_of(c * ks, ks)
        d = _dot_nt(qis_ref[...], ki2_ref[pl.ds(base, ks), :])
        for t in range(ks // LANES):
            acc = None
            for h in range(H_IDX):
                term = jnp.maximum(d[h * tq:(h + 1) * tq, t * LANES:(t + 1) * LANES], 0.0) * wb_ref[h]
                acc = term if acc is None else acc + term
            kpos = base + t * LANES + kiota
            sc_ref[:, pl.ds(base + t * LANES, LANES)] = jnp.where(kpos <= qpos, acc, NEG_INF)
        return 0

    n_big = ((i + 1) * tq + kb - 1) // kb
    n_score = ((i + 1) * tq + ks - 1) // ks
    lax.fori_loop(0, n_score, score_chunk, 0)

    def fill_hidden(c, _):
        sc_ref[:, pl.ds(pl.multiple_of(c * ks, ks), ks)] = jnp.full((tq, ks), NEG_INF, F32)
        return 0

    lax.fori_loop(n_score, n_big * (kb // ks), fill_hidden, 0)
    _topk_to_bias(sc_ref, nkc, kc, n_sel)

    _stack_pair_queries(q_ref, qs_ref, tq)

    def logits(j, c, raw, masked):
        bias = sc_ref[:, pl.ds(pl.multiple_of(c * kb, kb), kb)]
        return raw + jnp.concatenate([bias, bias], axis=0)

    _attend_pairs(qs_ref, k_ref, v_ref, out_ref, s_buf, stat_ref, lsum_ref, acc_ref, tq=tq, kb=kb,
                  n_plain=n_big, n_chunks=n_big, logits_fn=logits)


def _dsa_prompt(qi, kw, ki2, q, kb, vb, n, s, tq, kc, kbig):
    hb = q.shape[1]
    nq = s // tq
    n_sel = min(TOPK_MAX, s // 4)
    ks = min(kc, 4 * LANES)
    body = functools.partial(_dsa_body, tq=tq, kc=kc, ks=ks, kb=kbig, n_sel=n_sel)
    scratch = [pltpu.VMEM((tq, s), F32), pltpu.VMEM((H_IDX, tq, LANES), F32), pltpu.VMEM((H_IDX * tq, LANES), BF16)]
    scratch += _pair_scratch(hb // LANES, tq, s)
    one = pl.Buffered(1)
    qrow = lambda b, i: (b * nq + i, 0)
    return pl.pallas_call(
        body, grid=(n, nq),
        in_specs=[pl.BlockSpec((tq, hb), qrow), pl.BlockSpec((tq, LANES), qrow),
                  pl.BlockSpec((s, LANES), lambda b, i: (b, 0), pipeline_mode=one),
                  pl.BlockSpec((tq, hb), qrow),
                  pl.BlockSpec((s, hb), lambda b, i: (b, 0), pipeline_mode=one),
                  pl.BlockSpec((s, hb), lambda b, i: (b, 0), pipeline_mode=one)],
        out_specs=pl.BlockSpec((tq, hb), qrow),
        out_shape=jax.ShapeDtypeStruct((n * s, hb), BF16),
        scratch_shapes=scratch, compiler_params=_params(), name="dsa_prompt")(qi, kw, ki2, q, kb, vb)


def _page_specs(pool, li, pages_per_step, n_pages):
    def spec(r):
        def index(n, p, pt):
            return (li, pt[n, jnp.minimum(p * pages_per_step + r, n_pages - 1)]) + (0,) * (pool.ndim - 2)
        return pl.BlockSpec((1, 1) + pool.shape[2:], index)
    return [spec(r) for r in range(pages_per_step)]


def _idx_scores_body(pt_ref, qi_ref, w_ref, *rest, pages_per_step, n_main, page, t_new):
    page_refs = rest[:pages_per_step]
    new_ref, out_ref = rest[pages_per_step], rest[pages_per_step + 1]
    p = pl.program_id(1)
    qi = qi_ref[0]
    w = w_ref[0] * (H_IDX ** -0.5)

    def scores(ki_t):
        d = _dot(qi, ki_t.astype(BF16))
        contrib = jnp.maximum(d, 0.0) * w
        return jnp.sum(contrib.reshape(t_new, H_IDX, page), axis=1)

    @pl.when(p < n_main)
    def _():
        for r in range(pages_per_step):
            off = pl.multiple_of((p * pages_per_step + r) * page, page)
            out_ref[0, :, pl.ds(off, page)] = scores(page_refs[r][0, 0])

    @pl.when(p == n_main)
    def _():
        sc = scores(new_ref[0])
        t = lax.broadcasted_iota(jnp.int32, sc.shape, 0)
        j = lax.broadcasted_iota(jnp.int32, sc.shape, 1)
        out_ref[0, :, n_main * pages_per_step * page:] = jnp.where(j <= t, sc, NEG_INF)


def _idx_scores_sample(page_table, qi_rows, w_rows, ki_pool, li, ki_new_pages, pages_per_step):
    nseq, n_pages = page_table.shape
    page = ki_pool.shape[3]
    t_new = qi_rows.shape[1] // H_IDX
    n_main = n_pages // pages_per_step
    length = (n_pages + 1) * page
    body = functools.partial(_idx_scores_body, pages_per_step=pages_per_step, n_main=n_main, page=page, t_new=t_new)
    per_seq = lambda n, p, pt: (n, 0, 0)
    grid_spec = pltpu.PrefetchScalarGridSpec(
        num_scalar_prefetch=1, grid=(nseq, n_main + 1),
        in_specs=[pl.BlockSpec((1,) + qi_rows.shape[1:], per_seq), pl.BlockSpec((1,) + w_rows.shape[1:], per_seq)]
        + _page_specs(ki_pool, li, pages_per_step, n_pages)
        + [pl.BlockSpec((1,) + ki_new_pages.shape[1:], per_seq)],
        out_specs=pl.BlockSpec((1, t_new, length), per_seq))
    return pl.pallas_call(
        body, grid_spec=grid_spec, out_shape=jax.ShapeDtypeStruct((nseq, t_new, length), F32),
        compiler_params=_params(), name="idx_scores_sample")(
            page_table, qi_rows, w_rows, *([ki_pool] * pages_per_step), ki_new_pages)


def _topk_rows_body(sc_ref, out_ref, *, kc, n_sel):
    out_ref[...] = sc_ref[...]
    _topk_to_bias(out_ref, out_ref.shape[1] // kc, kc, n_sel, on=1.0, off=0.0)


def _topk_rows(scores, n_sel, rows, kc):
    r, length = scores.shape
    body = functools.partial(_topk_rows_body, kc=kc, n_sel=n_sel)
    return pl.pallas_call(
        body, grid=(r // rows,), in_specs=[pl.BlockSpec((rows, length), lambda i: (i, 0))],
        out_specs=pl.BlockSpec((rows, length), lambda i: (i, 0)),
        out_shape=jax.ShapeDtypeStruct((r, length), F32), compiler_params=_params(), name="topk_rows")(scores)


def _split3(x):
    hi = x.astype(BF16)
    r1 = x - hi.astype(F32)
    mid = r1.astype(BF16)
    lo = (r1 - mid.astype(F32)).astype(BF16)
    return hi, mid, lo


def _decay_bias_body(pt_ref, *rest, pages_per_step, n_pages, n_heads):
    page_refs = rest[:pages_per_step]
    new_ref, out_ref, lf_ref = rest[pages_per_step:]
    p = pl.program_id(1)
    n_rows, page = lf_ref.shape

    @pl.when(p == 0)
    def _():
        lf_ref[n_pages * n_heads:, :] = new_ref[0]

    for r in range(pages_per_step):
        row0 = pl.multiple_of((p * pages_per_step + r) * n_heads, n_heads)
        lf_ref[pl.ds(row0, n_heads), :] = page_refs[r][0, 0]

    @pl.when(p == pl.num_programs(1) - 1)
    def _():
        upto = jnp.where(lax.broadcasted_iota(jnp.int32, (page, page), 0)
                         <= lax.broadcasted_iota(jnp.int32, (page, page), 1), 1.0, 0.0).astype(BF16)
        src = lax.broadcasted_iota(jnp.int32, (n_rows, n_rows), 1)
        dst = lax.broadcasted_iota(jnp.int32, (n_rows, n_rows), 0)
        earlier_pages = jnp.where((src % n_heads) == (dst % n_heads),
                                  jnp.where(src // n_heads < dst // n_heads, 1.0, 0.0), 0.0).astype(BF16)
        cum = sum(_dot(x, upto) for x in _split3(lf_ref[...]))
        totals = jnp.broadcast_to(cum[:, page - 1:], cum.shape)
        cum = cum + sum(_dot(earlier_pages, x) for x in _split3(totals))
        out_ref[0] = -cum


def _decay_bias_sample(page_table, f_pool_t, li, f_new_t):
    nseq, n_pages = page_table.shape
    n_heads, page = f_pool_t.shape[2:]
    pages_per_step = _pages_per_step(n_pages, 32)
    n_rows = (n_pages + 1) * n_heads
    per_seq = lambda n, p, pt: (n, 0, 0)
    body = functools.partial(_decay_bias_body, pages_per_step=pages_per_step, n_pages=n_pages, n_heads=n_heads)
    grid_spec = pltpu.PrefetchScalarGridSpec(
        num_scalar_prefetch=1, grid=(nseq, n_pages // pages_per_step),
        in_specs=_page_specs(f_pool_t, li, pages_per_step, n_pages) + [pl.BlockSpec((1, n_heads, page), per_seq)],
        out_specs=pl.BlockSpec((1, n_rows, page), per_seq), scratch_shapes=[pltpu.VMEM((n_rows, page), F32)])
    return pl.pallas_call(
        body, grid_spec=grid_spec, out_shape=jax.ShapeDtypeStruct((nseq, n_rows, page), F32),
        compiler_params=_params(), name="decay_bias_sample")(
            page_table, *([f_pool_t] * pages_per_step), f_new_t)


def _paged_attn_body(pt_ref, q_ref, *rest, pages_per_step, n_main, page, t_new, n_heads, decay):
    it = iter(rest)
    k_refs = [next(it) for _ in range(pages_per_step)]
    v_refs = [next(it) for _ in range(pages_per_step)]
    knew_ref, vnew_ref = next(it), next(it)
    side_ref = next(it)
    out_ref = next(it)
    m_ref, l_ref, acc_ref = next(it), next(it), next(it)
    p = pl.program_id(1)
    rows = t_new * n_heads
    q = q_ref[0]

    @pl.when(p == 0)
    def _():
        m_ref[...] = jnp.full(m_ref.shape, M_INIT, F32)
        l_ref[...] = jnp.zeros(l_ref.shape, F32)
        acc_ref[...] = jnp.zeros(acc_ref.shape, F32)

    def page_logits(k_t, page_idx, new_page):
        s = _dot(q, k_t.astype(BF16))
        if decay:
            neg_cum = side_ref[0, pl.ds(pl.multiple_of(page_idx * n_heads, n_heads), n_heads), :]
            s = s + jnp.concatenate([neg_cum] * t_new, axis=0)
            if new_page:
                t = lax.broadcasted_iota(jnp.int32, s.shape, 0) // n_heads
                s = jnp.where(lax.broadcasted_iota(jnp.int32, s.shape, 1) <= t, s, NEG_INF)
            return s
        sel = side_ref[0, :, pl.ds(pl.multiple_of(page_idx * page, page), page)]
        sel_rows = jnp.broadcast_to(sel[:, None, :], (t_new, n_heads, page)).reshape(rows, page)
        return jnp.where(sel_rows > 0.5, s, NEG_INF)

    def update(logits, values_t):
        m = m_ref[...]
        m_new = m
        for s in logits:
            m_new = jnp.maximum(m_new, jnp.max(s, axis=1, keepdims=True))
        alpha = jnp.exp(m - m_new)
        l_new = alpha * l_ref[...]
        acc = alpha * acc_ref[...]
        for s, v_t in zip(logits, values_t):
            pr = jnp.exp(s - m_new)
            l_new = l_new + jnp.sum(pr, axis=1, keepdims=True)
            acc = acc + _dot_nt(pr.astype(BF16), v_t.astype(BF16))
        l_ref[...] = l_new
        acc_ref[...] = acc
        m_ref[...] = m_new

    @pl.when(p < n_main)
    def _():
        update([page_logits(k_refs[r][0, 0], p * pages_per_step + r, False) for r in range(pages_per_step)],
               [v_refs[r][0, 0] for r in range(pages_per_step)])

    @pl.when(p == n_main)
    def _():
        update([page_logits(knew_ref[0], n_main * pages_per_step, True)], [vnew_ref[0]])
        o = acc_ref[...] / l_ref[...]
        head_of_row = lax.broadcasted_iota(jnp.int32, o.shape, 0) % n_heads
        head_of_lane = lax.broadcasted_iota(jnp.int32, o.shape, 1) // HEAD_DIM
        o = jnp.where(head_of_row == head_of_lane, o, 0.0)
        out_ref[0] = jnp.sum(o.reshape(t_new, n_heads, o.shape[1]), axis=1).astype(BF16)


def _paged_attn(page_table, q_rows, k_pool_t, v_pool_t, li, k_new_t, v_new_t, pages_per_step,
                neg_cum=None, selected=None):
    nseq, n_pages = page_table.shape
    width, page = k_pool_t.shape[2:]
    n_heads = width // HEAD_DIM
    rows = q_rows.shape[1]
    t_new = rows // n_heads
    n_main = n_pages // pages_per_step
    decay = neg_cum is not None
    side = neg_cum if decay else selected
    per_seq = lambda n, p, pt: (n, 0, 0)
    body = functools.partial(_paged_attn_body, pages_per_step=pages_per_step, n_main=n_main, page=page,
                             t_new=t_new, n_heads=n_heads, decay=decay)
    args = [page_table, q_rows] + [k_pool_t] * pages_per_step + [v_pool_t] * pages_per_step + [k_new_t, v_new_t, side]
    in_specs = [pl.BlockSpec((1,) + q_rows.shape[1:], per_seq)]
    in_specs += _page_specs(k_pool_t, li, pages_per_step, n_pages) + _page_specs(v_pool_t, li, pages_per_step, n_pages)
    in_specs += [pl.BlockSpec((1, width, page), per_seq), pl.BlockSpec((1, width, page), per_seq),
                 pl.BlockSpec((1,) + side.shape[1:], per_seq)]
    scratch = [pltpu.VMEM((rows, 1), F32), pltpu.VMEM((rows, 1), F32), pltpu.VMEM((rows, width), F32)]
    grid_spec = pltpu.PrefetchScalarGridSpec(
        num_scalar_prefetch=1, grid=(nseq, n_main + 1), in_specs=in_specs,
        out_specs=pl.BlockSpec((1, t_new, width), per_seq), scratch_shapes=scratch)
    return pl.pallas_call(
        body, grid_spec=grid_spec, out_shape=jax.ShapeDtypeStruct((nseq, t_new, width), BF16),
        compiler_params=_params(), name="paged_attn_decay" if decay else "paged_attn_select")(*args)


def _rope_tables(pos):
    half = HEAD_DIM // 2
    inv = ROPE_THETA ** (-jnp.arange(half, dtype=F32) / half)
    ang = pos.astype(F32)[:, None] * inv[None, :]
    cos, sin = jnp.cos(ang), jnp.sin(ang)
    reps = LANES // HEAD_DIM
    return jnp.tile(jnp.concatenate([cos, cos], axis=1), (1, reps)), jnp.tile(jnp.concatenate([-sin, sin], axis=1), (1, reps))


def _block_diag(w):
    g, bw, _ = w.shape
    eye = jnp.eye(g, dtype=w.dtype)
    return (w[:, :, None, :] * eye[:, None, :, None]).reshape(g * bw, g * bw)


def _pad_cols(w, width):
    return jnp.pad(w, ((0, 0), (0, width - w.shape[1])))


def _to_time_major(a):
    return jnp.swapaxes(a, 0, 1).reshape((a.shape[0] * a.shape[1],) + a.shape[2:])


def _from_time_major(a, n):
    return jnp.swapaxes(a.reshape((a.shape[0] // n, n) + a.shape[1:]), 0, 1)


def _expand_heads(a, n, n_heads):
    width = a.shape[1]
    per_seq = _from_time_major(a, n)
    mask = (jnp.arange(width)[None, :] // HEAD_DIM) == jnp.arange(n_heads)[:, None]
    out = jnp.where(mask[None, None], per_seq[:, :, None, :], jnp.zeros((), a.dtype))
    return out.reshape(n, -1, width)


def _new_pages_t(a, n, page):
    return jnp.swapaxes(_new_pages(a, n, page), 1, 2)


def _key_minor(pool):
    moved = jnp.moveaxis(pool, 2, -1)
    return moved.reshape(moved.shape[:2] + (-1, moved.shape[-1]))


def _new_pages(a, n, page):
    per_seq = _from_time_major(a, n)
    return jnp.pad(per_seq, ((0, 0), (0, page - per_seq.shape[1]), (0, 0)))


def _heads_from_key_minor(a):
    n, width, s = a.shape
    return jnp.transpose(a.reshape(n, width // HEAD_DIM, HEAD_DIM, s), (0, 3, 1, 2))


def _pages_per_step(n_pages, pref):
    while n_pages % pref:
        pref //= 2
    return pref


def _pick_tile(m, pref):
    t = min(m, pref)
    while m % t:
        t //= 2
    return t


def kernel(x_prompt, x_sample, cache_k_b, cache_v_b, cache_kidx_b, state_conv_a, cache_k_c, cache_v_c, cache_logf_c, state_conv_d, state_lru_h, page_table, g_ff1, w_ff1_gate, w_ff1_up, w_ff1_down, g_mix, g_ff2, w_ff2_gate, w_ff2_up, w_ff2_down, w_in_e, conv_a_w, w_out_e, w_in_o, b_f, conv_d_w, conv_d_b, w_rg_a, b_rg_a, w_rg_x, b_rg_x, lam, w_out_o, g_final):
    batch, seq, d_model = x_prompt.shape
    nseq, t_new, _ = x_sample.shape
    depth = g_ff1.shape[0]
    n_pool, page = cache_k_b.shape[1], cache_k_b.shape[2]
    n_pages = page_table.shape[1]
    past = n_pages * page
    wa = conv_a_w.shape[2]
    hb = w_out_e.shape[1] - wa
    wd = conv_d_w.shape[2]
    hc = w_out_o.shape[1] - wd
    n_heads_c = hc // HEAD_DIM
    pages_per_step = _pages_per_step(n_pages, 16)
    idx_pages_per_step = _pages_per_step(n_pages, 32)

    mp, ms = batch * seq, nseq * t_new
    tm_p = _pick_tile(seq, 512)
    xp = x_prompt.reshape(mp, d_model)
    xs = _to_time_major(x_sample)
    cos_p, sin_p = _rope_tables(jnp.tile(jnp.arange(seq), batch))
    cos_s, sin_s = _rope_tables(past + jnp.repeat(jnp.arange(t_new), nseq))
    row2 = lambda v: v.reshape(1, -1)
    bf = lambda w: w.astype(BF16)

    outs_p, outs_s = {}, {}
    for l in range(depth):
        ffn1 = (row2(g_ff1[l]), bf(w_ff1_gate[l]), bf(w_ff1_up[l]), bf(w_ff1_down[l]))
        ffn2 = (row2(g_ff2[l]), bf(w_ff2_gate[l]), bf(w_ff2_up[l]), bf(w_ff2_down[l]))
        xp = _ffn(xp, *ffn1, tm=tm_p)
        xs = _ffn(xs, *ffn1, tm=ms)
        g = row2(g_mix[l])
        if l % 2 == 0:
            e = l // 2
            w = w_in_e[e]
            o_q, o_k, o_v, o_qi, o_kw = 3 * wa, 3 * wa + hb, 3 * wa + 2 * hb, 3 * wa + 3 * hb, 3 * wa + 3 * hb + H_IDX * D_IDX
            weights = (bf(w[:, :o_q]), bf(w[:, o_q:o_k]), bf(w[:, o_k:o_v]), bf(w[:, o_v:o_qi]), bf(w[:, o_qi:o_kw]),
                       bf(_pad_cols(w[:, o_kw:], LANES)))
            w_out = (bf(w_out_e[e][:wa]), bf(w_out_e[e][wa:]))

            hist_p = jnp.zeros((batch, conv_a_w.shape[1] - 1, wa), F32)
            ya, q, k32, kb, v32, vb, qi, kw, ki2, cstate = _proj_even(
                xp, g, weights, cos_p, sin_p, conv_a_w[e], hist_p, tm_p, seq // tm_p, 1, True)
            yb = _dsa_prompt(qi, kw, ki2, q, kb, vb, batch, seq, _pick_tile(seq, 256), _pick_tile(seq, 512),
                             _pick_tile(seq, 1024))
            outs_p.setdefault("k_b", []).append(_heads_from_key_minor(k32))
            outs_p.setdefault("v_b", []).append(_heads_from_key_minor(v32))
            outs_p.setdefault("kidx_b", []).append(kw[:, :D_IDX].reshape(batch, seq, D_IDX))
            outs_p.setdefault("conv_a", []).append(cstate)
            mix_p = (ya, yb, *w_out)

            hist_s = _to_time_major(state_conv_a[e])[None]
            ya, q, k32, kb, v32, vb, qi, kw, ki2, cstate = _proj_even(
                xs, g, weights, cos_s, sin_s, conv_a_w[e], hist_s, ms, 1, nseq, False)
            qi_rows = _from_time_major(qi, nseq).reshape(nseq, t_new * H_IDX, D_IDX)
            w_rows = _from_time_major(kw[:, D_IDX:D_IDX + H_IDX], nseq).reshape(nseq, t_new * H_IDX, 1)
            scores = _idx_scores_sample(page_table, qi_rows, w_rows, _key_minor(cache_kidx_b), e,
                                        _new_pages_t(kw[:, :D_IDX], nseq, page), idx_pages_per_step)
            length = scores.shape[2]
            n_sel = min(TOPK_MAX, (past + t_new) // 4)
            kc = page * 3 if (length // page) % 3 == 0 else page
            picked = _topk_rows(scores.reshape(ms, length), n_sel, _pick_tile(ms, 32), kc).reshape(nseq, t_new, length)
            yb = _paged_attn(page_table, _expand_heads(q, nseq, hb // HEAD_DIM), _key_minor(cache_k_b),
                             _key_minor(cache_v_b), e, _new_pages_t(k32, nseq, page), _new_pages_t(v32, nseq, page),
                             pages_per_step, selected=picked)
            yb = _to_time_major(yb)
            outs_s.setdefault("k_b", []).append(_from_time_major(k32, nseq).reshape(nseq, t_new, hb // HEAD_DIM, HEAD_DIM))
            outs_s.setdefault("v_b", []).append(_from_time_major(v32, nseq).reshape(nseq, t_new, hb // HEAD_DIM, HEAD_DIM))
            outs_s.setdefault("kidx_b", []).append(_from_time_major(kw[:, :D_IDX], nseq))
            outs_s.setdefault("conv_a", []).append(_from_time_major(cstate[0], nseq))
            mix_s = (ya, yb, *w_out)
        else:
            o = l // 2
            w = w_in_o[o]
            o_f, o_xd, o_gd = 3 * hc, 3 * hc + n_heads_c, 3 * hc + n_heads_c + wd
            weights = (bf(w[:, :o_f]), bf(_pad_cols(w[:, o_f:o_xd], LANES)), bf(w[:, o_xd:o_gd]), bf(w[:, o_gd:]))
            small = (_pad_cols(row2(b_f[o]), LANES), conv_d_w[o], row2(conv_d_b[o]), bf(_block_diag(w_rg_a[o])),
                     row2(b_rg_a[o]), bf(_block_diag(w_rg_x[o])), row2(b_rg_x[o]), row2(lam[o]))
            w_out = (bf(w_out_o[o][:hc]), bf(w_out_o[o][hc:]))

            hist_p = jnp.zeros((batch, conv_d_w.shape[1] - 1, wd), F32)
            h0_p = jnp.zeros((batch, 1, wd), F32)
            q, k32, kb, v32, vb, logf, yd, cstate, hlast = _proj_odd(
                xp, g, weights, small, hist_p, h0_p, tm_p, seq // tm_p, 1, 0, True)
            cum_t = _cumsum_heads(logf, batch, seq, _pick_tile(seq, 512))
            yc = _fox_prompt(q, kb, vb, cum_t, batch, seq, _pick_tile(seq, 512), _pick_tile(seq, 1024))
            outs_p.setdefault("k_c", []).append(_heads_from_key_minor(k32))
            outs_p.setdefault("v_c", []).append(_heads_from_key_minor(v32))
            outs_p.setdefault("logf_c", []).append(logf.reshape(batch, seq, n_heads_c))
            outs_p.setdefault("conv_d", []).append(cstate)
            outs_p.setdefault("lru_h", []).append(hlast[:, 0])
            mix_p = (yc, yd, *w_out)

            hist_s = _to_time_major(state_conv_d[o])[None]
            h0_s = state_lru_h[o][None]
            q, k32, kb, v32, vb, logf, yd, cstate, hlast = _proj_odd(
                xs, g, weights, small, hist_s, h0_s, ms, 1, nseq, past, False)
            neg_cum = _decay_bias_sample(page_table, _key_minor(cache_logf_c), o, _new_pages_t(logf, nseq, page))
            yc = _paged_attn(page_table, _expand_heads(q, nseq, n_heads_c), _key_minor(cache_k_c),
                             _key_minor(cache_v_c), o, _new_pages_t(k32, nseq, page), _new_pages_t(v32, nseq, page),
                             pages_per_step, neg_cum=neg_cum)
            yc = _to_time_major(yc)
            outs_s.setdefault("k_c", []).append(_from_time_major(k32, nseq).reshape(nseq, t_new, n_heads_c, HEAD_DIM))
            outs_s.setdefault("v_c", []).append(_from_time_major(v32, nseq).reshape(nseq, t_new, n_heads_c, HEAD_DIM))
            outs_s.setdefault("logf_c", []).append(_from_time_major(logf, nseq))
            outs_s.setdefault("conv_d", []).append(_from_time_major(cstate[0], nseq))
            outs_s.setdefault("lru_h", []).append(hlast[0])
            mix_s = (yc, yd, *w_out)
        gfin = row2(g_final) if l == depth - 1 else None
        xp = _ffn(xp, *ffn2, tm=tm_p, mix=mix_p, g_final=gfin)
        xs = _ffn(xs, *ffn2, tm=ms, mix=mix_s, g_final=gfin)

    names = ("k_b", "v_b", "kidx_b", "conv_a", "k_c", "v_c", "logf_c", "conv_d", "lru_h")
    st_p = tuple(jnp.stack(outs_p[nm]) for nm in names)
    st_s = tuple(jnp.stack(outs_s[nm]) for nm in names)
    return (xp.reshape(batch, seq, d_model), _from_time_major(xs, nseq)) + st_p + st_s
```

```python
import functools

import jax
import jax.numpy as jnp
import numpy as np
from jax import lax
from jax.experimental import pallas as pl
from jax.experimental.pallas import tpu as pltpu

F32 = jnp.float32
BF16 = jnp.bfloat16

HEAD_DIM = 64
D_IDX = 64
H_IDX = 8
TOPK_MAX = 256
LRU_BLOCKS = 8
RG_C = 8.0
ROPE_THETA = 10000.0
EPS = 1e-6

LANES = 128
SUBLANES = 8
VMEM_LIMIT = 56 * 1024 * 1024
NEG_INF = float("-inf")
M_INIT = -1e30
BISECT_ITERS = 400
TIE_CHECK_FIRST = 10
TIE_CHECK_EVERY = 4
TOPK_ROW_BLOCK = 128

_NT = (((1,), (1,)), ((), ()))


def _params():
    return pltpu.CompilerParams(vmem_limit_bytes=VMEM_LIMIT)


def _const_spec(shape):
    nd = len(shape)
    return pl.BlockSpec(shape, lambda *_: (0,) * nd, pipeline_mode=pl.Buffered(1))


def _dot(a, b):
    return jnp.dot(a, b, preferred_element_type=F32)


def _dot_nt(a, b, precision=None):
    return lax.dot_general(a, b, _NT, preferred_element_type=F32, precision=precision)


def _rmsnorm(x, g):
    ms = jnp.mean(x * x, axis=-1, keepdims=True)
    return x * lax.rsqrt(ms + EPS) * g


def _ffn_body(*refs, n_chunks, fc, has_mix, has_final):
    it = iter(refs)
    x_ref = next(it)
    if has_mix:
        ya_ref, yb_ref, woa_ref, wob_ref = next(it), next(it), next(it), next(it)
    g_ref, wg_ref, wu_ref, wd_ref = next(it), next(it), next(it), next(it)
    if has_final:
        gf_ref = next(it)
    out_ref = next(it)

    x = x_ref[...]
    if has_mix:
        x = x + _dot(ya_ref[...], woa_ref[...]) + _dot(yb_ref[...], wob_ref[...])
    h = _rmsnorm(x, g_ref[...]).astype(BF16)
    acc = None
    for c in range(n_chunks):
        gate = _dot(h, wg_ref[:, c * fc:(c + 1) * fc])
        up = _dot(h, wu_ref[:, c * fc:(c + 1) * fc])
        a = (gate * jax.nn.sigmoid(gate) * up).astype(BF16)
        part = _dot(a, wd_ref[c * fc:(c + 1) * fc, :])
        acc = part if acc is None else acc + part
    x = x + 0.5 * acc
    if has_final:
        x = _rmsnorm(x, gf_ref[...])
    out_ref[...] = x


def _ffn(x, g, wg, wu, wd, tm, mix=None, g_final=None):
    m, d = x.shape
    f = wg.shape[1]
    n_chunks = 2 if (f // 2) % LANES == 0 else 1
    fc = f // n_chunks
    row = lambda i: (i, 0)
    args, specs = [x], [pl.BlockSpec((tm, d), row)]
    if mix is not None:
        ya, yb, woa, wob = mix
        args += [ya, yb, woa, wob]
        specs += [pl.BlockSpec((tm, ya.shape[1]), row), pl.BlockSpec((tm, yb.shape[1]), row),
                  _const_spec(woa.shape), _const_spec(wob.shape)]
    args += [g, wg, wu, wd]
    specs += [_const_spec(g.shape), _const_spec(wg.shape), _const_spec(wu.shape), _const_spec(wd.shape)]
    if g_final is not None:
        args.append(g_final)
        specs.append(_const_spec(g_final.shape))
    body = functools.partial(_ffn_body, n_chunks=n_chunks, fc=fc, has_mix=mix is not None,
                             has_final=g_final is not None)
    return pl.pallas_call(
        body, grid=(m // tm,), in_specs=specs, out_specs=pl.BlockSpec((tm, d), row),
        out_shape=jax.ShapeDtypeStruct((m, d), F32), compiler_params=_params(), name="ffn")(*args)


def _rope_tile(xt, cos, sin_signed):
    lane = lax.broadcasted_iota(jnp.int32, xt.shape, 1)
    first_half = (lane % HEAD_DIM) < (HEAD_DIM // 2)
    partner = jnp.where(first_half, pltpu.roll(xt, LANES - HEAD_DIM // 2, 1), pltpu.roll(xt, HEAD_DIM // 2, 1))
    return xt * cos + partner * sin_signed


def _store_f32(ref, cols, value, key_minor):
    if key_minor:
        ref[0, cols, :] = value.T
    else:
        ref[:, cols] = value


def _kv_out(m, width, tm, tiles_per_seq, key_minor):
    if not key_minor:
        return jax.ShapeDtypeStruct((m, width), F32), pl.BlockSpec((tm, width), lambda i: (i, 0))
    shape = (m // (tm * tiles_per_seq), width, tm * tiles_per_seq)
    return (jax.ShapeDtypeStruct(shape, F32),
            pl.BlockSpec((1, width, tm), lambda i: (i // tiles_per_seq, 0, i % tiles_per_seq)))


def _hist_rows(width, stride):
    need = (width - 1) * stride
    pad = -(-need // SUBLANES) * SUBLANES
    return need, pad


def _causal_conv(u, w_ref, ubuf, hist_ref, state_ref, first_tile, width, stride):
    tm = u.shape[0]
    need, pad = _hist_rows(width, stride)

    @pl.when(first_tile)
    def _():
        ubuf[pad - need:pad, :] = hist_ref[0]

    @pl.when(jnp.logical_not(first_tile))
    def _():
        ubuf[0:pad, :] = ubuf[tm:tm + pad, :]

    ubuf[pad:pad + tm, :] = u
    y = None
    for j in range(width):
        start = pad - (width - 1 - j) * stride
        term = w_ref[j:j + 1, :] * ubuf[start:start + tm, :]
        y = term if y is None else y + term
    state_ref[0] = ubuf[pad + tm - need:pad + tm, :]
    return y


def _proj_even_body(x_ref, g_ref, wc_ref, wq_ref, wk_ref, wv_ref, wqi_ref, wkw_ref, cos_ref, sin_ref,
                    cw_ref, hist_ref,
                    ya_ref, q_ref, k32_ref, kb_ref, v32_ref, vb_ref, qi_ref, kw_ref, ki2_ref, state_ref,
                    ubuf, *, tiles_per_seq, stride, wa, kv_key_minor):
    first_tile = (pl.program_id(0) % tiles_per_seq) == 0
    h = _rmsnorm(x_ref[...], g_ref[...]).astype(BF16)
    cos = cos_ref[...]
    sin = sin_ref[...]
    scale = HEAD_DIM ** -0.5

    zc = _dot(h, wc_ref[...])
    xin, gb, gc = zc[:, :wa], zc[:, wa:2 * wa], zc[:, 2 * wa:]
    yconv = _causal_conv(gc * xin, cw_ref, ubuf, hist_ref, state_ref, first_tile, cw_ref.shape[0], stride)
    ya_ref[...] = (gb * yconv).astype(BF16)

    n_tiles = wq_ref.shape[1] // LANES
    q = _dot(h, wq_ref[...])
    k = _dot(h, wk_ref[...])
    qi = _dot(h, wqi_ref[...])
    for t in range(n_tiles):
        sl = slice(t * LANES, (t + 1) * LANES)
        q_ref[:, sl] = (_rope_tile(q[:, sl], cos, sin) * scale).astype(BF16)
        kt = _rope_tile(k[:, sl], cos, sin)
        _store_f32(k32_ref, sl, kt, kv_key_minor)
        kb_ref[:, sl] = kt.astype(BF16)
        qi_ref[:, sl] = (_rope_tile(qi[:, sl], cos, sin) * (D_IDX ** -0.5)).astype(BF16)
    v = _dot(h, wv_ref[...])
    _store_f32(v32_ref, slice(None), v, kv_key_minor)
    vb_ref[...] = v.astype(BF16)

    kw = _dot(h, wkw_ref[...])
    lane = lax.broadcasted_iota(jnp.int32, kw.shape, 1)
    is_ki = lane < D_IDX
    kw = jnp.where(is_ki, _rope_tile(kw, cos, sin), kw)
    kw_ref[...] = kw
    ki2_ref[...] = jnp.where(is_ki, kw, pltpu.roll(kw, D_IDX, 1)).astype(BF16)


def _proj_even(x, g, w, cos, sin, conv_w, hist, tm, tiles_per_seq, stride, kv_key_minor):
    m, d = x.shape
    wc, wq, wk, wv, wqi, wkw = w
    wa = wc.shape[1] // 3
    hb = wq.shape[1]
    width = conv_w.shape[0]
    need, pad = _hist_rows(width, stride)
    nb = hist.shape[0]
    row = lambda i: (i, 0)
    seq = lambda i: (i // tiles_per_seq, 0, 0)
    in_specs = [pl.BlockSpec((tm, d), row), _const_spec(g.shape)]
    in_specs += [_const_spec(a.shape) for a in (wc, wq, wk, wv, wqi, wkw)]
    in_specs += [pl.BlockSpec((tm, LANES), row), pl.BlockSpec((tm, LANES), row), _const_spec(conv_w.shape),
                 pl.BlockSpec((1, need, wa), seq)]
    out_shape = [jax.ShapeDtypeStruct((m, wa), BF16), jax.ShapeDtypeStruct((m, hb), BF16),
                 jax.ShapeDtypeStruct((m, hb), F32), jax.ShapeDtypeStruct((m, hb), BF16),
                 jax.ShapeDtypeStruct((m, hb), F32), jax.ShapeDtypeStruct((m, hb), BF16),
                 jax.ShapeDtypeStruct((m, hb), BF16), jax.ShapeDtypeStruct((m, LANES), F32),
                 jax.ShapeDtypeStruct((m, LANES), BF16), jax.ShapeDtypeStruct((nb, need, wa), F32)]
    out_specs = [pl.BlockSpec((tm, wa), row)] + [pl.BlockSpec((tm, hb), row)] * 6
    out_specs += [pl.BlockSpec((tm, LANES), row), pl.BlockSpec((tm, LANES), row), pl.BlockSpec((1, need, wa), seq)]
    for pos in (2, 4):
        out_shape[pos], out_specs[pos] = _kv_out(m, hb, tm, tiles_per_seq, kv_key_minor)
    body = functools.partial(_proj_even_body, tiles_per_seq=tiles_per_seq, stride=stride, wa=wa,
                             kv_key_minor=kv_key_minor)
    return pl.pallas_call(
        body, grid=(m // tm,), in_specs=in_specs, out_specs=out_specs, out_shape=out_shape,
        scratch_shapes=[pltpu.VMEM((pad + tm, wa), F32)], compiler_params=_params(), name="proj_even")(
            x, g, wc, wq, wk, wv, wqi, wkw, cos, sin, conv_w, hist)


def _shift_rows(x, d, fill):
    if d % SUBLANES == 0:
        return jnp.concatenate([jnp.full((d, x.shape[1]), fill, x.dtype), x[:x.shape[0] - d]], axis=0)
    row = lax.broadcasted_iota(jnp.int32, x.shape, 0)
    return jnp.where(row >= d, pltpu.roll(x, d, 0), fill)


def _proj_odd_body(x_ref, g_ref, wqkv_ref, wf_ref, wxd_ref, wgd_ref, bf_ref, cw_ref, cb_ref, wra_ref, bra_ref,
                   wrx_ref, brx_ref, lam_ref, hist_ref, h0_ref,
                   q_ref, k32_ref, kb_ref, v32_ref, vb_ref, logf_ref, yd_ref, state_ref, hlast_ref,
                   ubuf, hcar, *, tiles_per_seq, stride, pos_base, hc, kv_key_minor):
    tile_in_seq = pl.program_id(0) % tiles_per_seq
    first_tile = tile_in_seq == 0
    tm = x_ref.shape[0]
    h = _rmsnorm(x_ref[...], g_ref[...]).astype(BF16)

    z = _dot(h, wqkv_ref[...])
    q_ref[...] = (z[:, :hc] * (HEAD_DIM ** -0.5)).astype(BF16)
    k = z[:, hc:2 * hc]
    v = z[:, 2 * hc:]
    _store_f32(k32_ref, slice(None), k, kv_key_minor)
    kb_ref[...] = k.astype(BF16)
    _store_f32(v32_ref, slice(None), v, kv_key_minor)
    vb_ref[...] = v.astype(BF16)

    zf = _dot(h, wf_ref[...]) + bf_ref[...]
    logf = jnp.minimum(zf, 0.0) - jnp.log1p(jnp.exp(-jnp.abs(zf)))
    logf_ref[...] = logf[:, :logf_ref.shape[1]]

    xd = _dot(h, wxd_ref[...])
    gd = _dot(h, wgd_ref[...])
    xc = _causal_conv(xd, cw_ref, ubuf, hist_ref, state_ref, first_tile, cw_ref.shape[0], stride) + cb_ref[...]
    xcb = xc.astype(BF16)
    r = jax.nn.sigmoid(_dot(xcb, wra_ref[...]) + bra_ref[...])
    ig = jax.nn.sigmoid(_dot(xcb, wrx_ref[...]) + brx_ref[...])
    lam = lam_ref[...]
    softplus_neg_lam = jnp.maximum(-lam, 0.0) + jnp.log1p(jnp.exp(-jnp.abs(lam)))
    log_a = -RG_C * r * softplus_neg_lam
    a = jnp.exp(log_a)
    row = lax.broadcasted_iota(jnp.int32, (tm, 1), 0)
    pos = pos_base + tile_in_seq * (tm // stride) + row // stride
    mult = jnp.where(pos == 0, 1.0, jnp.sqrt(1.0 - jnp.exp(2.0 * log_a)))
    b = mult * ig * xc

    d = stride
    while d < tm:
        a_prev = _shift_rows(a, d, 1.0)
        b_prev = _shift_rows(b, d, 0.0)
        b = a * b_prev + b
        a = a * a_prev
        d *= 2

    @pl.when(first_tile)
    def _():
        hcar[...] = h0_ref[0]

    carry = hcar[...]
    if stride == 1:
        carry_rows = carry
    else:
        carry_rows = jnp.concatenate([carry] * (tm // stride), axis=0)
    hs = a * carry_rows + b
    hcar[...] = hs[tm - stride:, :]
    hlast_ref[0] = hs[tm - stride:, :]
    yd_ref[...] = (hs * jax.nn.gelu(gd)).astype(BF16)


def _proj_odd(x, g, w, small, hist, h0, tm, tiles_per_seq, stride, pos_base, kv_key_minor):
    m, d = x.shape
    wqkv, wf, wxd, wgd = w
    bf, conv_w, conv_b, wra, bra, wrx, brx, lam = small
    hc = wqkv.shape[1] // 3
    wd_ = wxd.shape[1]
    n_heads = hc // HEAD_DIM
    width = conv_w.shape[0]
    need, pad = _hist_rows(width, stride)
    nb = hist.shape[0]
    row = lambda i: (i, 0)
    seq = lambda i: (i // tiles_per_seq, 0, 0)
    ins = [x, g, wqkv, wf, wxd, wgd, bf, conv_w, conv_b, wra, bra, wrx, brx, lam, hist, h0]
    in_specs = [pl.BlockSpec((tm, d), row)] + [_const_spec(a.shape) for a in ins[1:14]]
    in_specs += [pl.BlockSpec((1, need, wd_), seq), pl.BlockSpec((1, stride, wd_), seq)]
    out_shape = [jax.ShapeDtypeStruct((m, hc), BF16), jax.ShapeDtypeStruct((m, hc), F32),
                 jax.ShapeDtypeStruct((m, hc), BF16), jax.ShapeDtypeStruct((m, hc), F32),
                 jax.ShapeDtypeStruct((m, hc), BF16), jax.ShapeDtypeStruct((m, n_heads), F32),
                 jax.ShapeDtypeStruct((m, wd_), BF16), jax.ShapeDtypeStruct((nb, need, wd_), F32),
                 jax.ShapeDtypeStruct((nb, stride, wd_), F32)]
    out_specs = [pl.BlockSpec((tm, hc), row)] * 5 + [pl.BlockSpec((tm, n_heads), row), pl.BlockSpec((tm, wd_), row),
                                                      pl.BlockSpec((1, need, wd_), seq),
                                                      pl.BlockSpec((1, stride, wd_), seq)]
    for pos in (1, 3):
        out_shape[pos], out_specs[pos] = _kv_out(m, hc, tm, tiles_per_seq, kv_key_minor)
    body = functools.partial(_proj_odd_body, tiles_per_seq=tiles_per_seq, stride=stride, pos_base=pos_base, hc=hc,
                             kv_key_minor=kv_key_minor)
    return pl.pallas_call(
        body, grid=(m // tm,), in_specs=in_specs, out_specs=out_specs, out_shape=out_shape,
        scratch_shapes=[pltpu.VMEM((pad + tm, wd_), F32), pltpu.VMEM((stride, wd_), F32)],
        compiler_params=_params(), name="proj_odd")(*ins)


def _cumsum_lanes(lf, carry):
    t, nh = lf.shape
    eye = (lax.broadcasted_iota(jnp.int32, (nh, nh), 0) == lax.broadcasted_iota(jnp.int32, (nh, nh), 1)).astype(F32)
    lf_t = _dot_nt(eye, lf, precision=lax.Precision.HIGHEST)
    tri = (lax.broadcasted_iota(jnp.int32, (t, t), 1) <= lax.broadcasted_iota(jnp.int32, (t, t), 0)).astype(F32)
    return _dot_nt(lf_t, tri, precision=lax.Precision.HIGHEST) + carry


def _cumsum_body(lf_ref, out_ref, carry):
    @pl.when(pl.program_id(1) == 0)
    def _():
        carry[...] = jnp.zeros_like(carry)

    cum = _cumsum_lanes(lf_ref[...], carry[:, 0:1])
    out_ref[0] = cum
    carry[...] = jnp.broadcast_to(cum[:, cum.shape[1] - 1:], carry.shape)


def _cumsum_heads(logf, n, s, tc):
    nh = logf.shape[1]
    tiles = s // tc
    return pl.pallas_call(
        _cumsum_body, grid=(n, tiles),
        in_specs=[pl.BlockSpec((tc, nh), lambda b, i: (b * tiles + i, 0))],
        out_specs=pl.BlockSpec((1, nh, tc), lambda b, i: (b, 0, i)),
        out_shape=jax.ShapeDtypeStruct((n, nh, s), F32),
        scratch_shapes=[pltpu.VMEM((nh, LANES), F32)], compiler_params=_params(), name="cumsum")(logf)


def _pair_mask(shape, parity):
    lane = lax.broadcasted_iota(jnp.int32, shape, 1)
    return (lane >= HEAD_DIM) == (parity == 1)


def _stack_pair_queries(q_ref, qs_ref, tq):
    for j in range(q_ref.shape[1] // LANES):
        q_pair = q_ref[:, j * LANES:(j + 1) * LANES]
        zero = jnp.zeros_like(q_pair)
        qs_ref[j, 0:tq, :] = jnp.where(_pair_mask(q_pair.shape, 0), q_pair, zero)
        qs_ref[j, tq:2 * tq, :] = jnp.where(_pair_mask(q_pair.shape, 1), q_pair, zero)


def _attend_pairs(qs_ref, k_ref, v_ref, out_ref, s_buf, stat_ref, lsum_ref, acc_ref, *, tq, kb, n_plain, n_chunks,
                  logits_fn):
    tiles = kb // LANES
    n_pairs = qs_ref.shape[0]

    def col_of(j):
        start = j * LANES
        return pl.ds(start if isinstance(j, int) else pl.multiple_of(start, LANES), LANES)

    def pass1(j, c, masked):
        rows = pl.ds(pl.multiple_of(c * kb, kb), kb)
        s = logits_fn(j, c, _dot_nt(qs_ref[j], k_ref[rows, col_of(j)]), masked)
        s_buf[:, rows] = s
        mt = s[:, 0:LANES]
        for t in range(1, tiles):
            mt = jnp.maximum(mt, s[:, t * LANES:(t + 1) * LANES])
        stat_ref[j % 2] = jnp.maximum(stat_ref[j % 2], mt)

    def pass2(j, c):
        rows = pl.ds(pl.multiple_of(c * kb, kb), kb)
        m_rows = jnp.concatenate([stat_ref[j % 2]] * tiles, axis=1)
        p = jnp.exp(s_buf[:, rows] - m_rows)
        part = p[:, 0:LANES]
        for t in range(1, tiles):
            part = part + p[:, t * LANES:(t + 1) * LANES]
        lsum_ref[...] += part
        acc_ref[...] += _dot(p.astype(BF16), v_ref[rows, col_of(j)])

    def sweep(first, second):
        def body(c, _, masked):
            if second is not None:
                pass2(second, c)
            if first is not None:
                pass1(first, c, masked)
            return 0
        if first is None:
            lax.fori_loop(0, n_chunks, functools.partial(body, masked=False), 0)
        else:
            lax.fori_loop(0, n_plain, functools.partial(body, masked=False), 0)
            lax.fori_loop(n_plain, n_chunks, functools.partial(body, masked=True), 0)

    def open_pair(j):
        stat_ref[j % 2] = jnp.full(stat_ref.shape[1:], NEG_INF, F32)

    def close_max(j):
        m = jnp.max(stat_ref[j % 2], axis=1, keepdims=True)
        stat_ref[j % 2] = jnp.broadcast_to(m, stat_ref.shape[1:])
        lsum_ref[...] = jnp.zeros(lsum_ref.shape, F32)
        acc_ref[...] = jnp.zeros(acc_ref.shape, F32)

    def finish(j):
        o = acc_ref[...] / jnp.sum(lsum_ref[...], axis=1, keepdims=True)
        even, odd = o[0:tq], o[tq:2 * tq]
        out_ref[:, col_of(j)] = jnp.where(_pair_mask(even.shape, 1), odd, even).astype(BF16)

    open_pair(0)
    sweep(0, None)
    close_max(0)

    def stage(j, _):
        open_pair(j)
        sweep(j, j - 1)
        finish(j - 1)
        close_max(j)
        return 0

    lax.fori_loop(1, n_pairs, stage, 0)
    sweep(None, n_pairs - 1)
    finish(n_pairs - 1)


def _pair_scratch(n_pairs, tq, s):
    return [pltpu.VMEM((n_pairs, 2 * tq, LANES), BF16), pltpu.VMEM((2 * tq, s), F32),
            pltpu.VMEM((2, 2 * tq, LANES), F32), pltpu.VMEM((2 * tq, LANES), F32), pltpu.VMEM((2 * tq, LANES), F32)]


def _fox_body(q_ref, k_ref, v_ref, cum_ref, out_ref, qs_ref, s_buf, stat_ref, lsum_ref, acc_ref, *, tq, kc):
    i = pl.program_id(1)
    n_full = (i * tq) // kc
    n_tot = ((i + 1) * tq + kc - 1) // kc
    _stack_pair_queries(q_ref, qs_ref, tq)

    def logits(j, c, raw, masked):
        rows = pl.ds(pl.multiple_of(c * kc, kc), kc)
        halves = []
        for parity in (0, 1):
            sh = raw[parity * tq:(parity + 1) * tq] - cum_ref[0, pl.ds(2 * j + parity, 1), rows]
            if masked:
                qpos = i * tq + lax.broadcasted_iota(jnp.int32, (tq, kc), 0)
                kpos = c * kc + lax.broadcasted_iota(jnp.int32, (tq, kc), 1)
                sh = jnp.where(kpos <= qpos, sh, NEG_INF)
            halves.append(sh)
        return jnp.concatenate(halves, axis=0)

    _attend_pairs(qs_ref, k_ref, v_ref, out_ref, s_buf, stat_ref, lsum_ref, acc_ref, tq=tq, kb=kc,
                  n_plain=n_full, n_chunks=n_tot, logits_fn=logits)


def _fox_prompt(q, kb, vb, cum_t, n, s, tq, kc):
    hc = q.shape[1]
    nh = cum_t.shape[1]
    nq = s // tq
    body = functools.partial(_fox_body, tq=tq, kc=kc)
    scratch = _pair_scratch(hc // LANES, tq, s)
    one = pl.Buffered(1)
    return pl.pallas_call(
        body, grid=(n, nq),
        in_specs=[pl.BlockSpec((tq, hc), lambda b, i: (b * nq + i, 0)),
                  pl.BlockSpec((s, hc), lambda b, i: (b, 0), pipeline_mode=one),
                  pl.BlockSpec((s, hc), lambda b, i: (b, 0), pipeline_mode=one),
                  pl.BlockSpec((1, nh, s), lambda b, i: (b, 0, 0), pipeline_mode=one)],
        out_specs=pl.BlockSpec((tq, hc), lambda b, i: (b * nq + i, 0)),
        out_shape=jax.ShapeDtypeStruct((n * s, hc), BF16), scratch_shapes=scratch,
        compiler_params=_params(), name="fox_prompt")(
            q, kb, vb, cum_t)


def _topk_to_bias(sc_ref, nkc, kc, n_sel, on=0.0, off=NEG_INF):
    rows = sc_ref.shape[0]
    n_walk, k_walk = nkc, kc
    tiles = k_walk // LANES
    pos_inf = float("inf")

    rb = min(rows, TOPK_ROW_BLOCK)

    def lane_fold(fn, init):
        parts = []
        for r0 in range(0, rows, rb):
            rs = slice(r0, r0 + rb)

            def body(c, acc, rs=rs):
                base = pl.multiple_of(c * k_walk, k_walk)
                for t in range(tiles):
                    acc = fn(acc, sc_ref[rs, pl.ds(base + t * LANES, LANES)], rs)
                return acc
            parts.append(lax.fori_loop(0, n_walk, body, jnp.full((rb, LANES), init, F32)))
        return jnp.concatenate(parts, axis=0)

    def count(pred_fn):
        part = lane_fold(lambda acc, x, rs: acc + jnp.where(pred_fn(x, rs), 1.0, 0.0), 0.0)
        return jnp.sum(part, axis=1, keepdims=True)

    def range_and_count():
        parts = []
        for r0 in range(0, rows, rb):
            def body(c, acc, r0=r0):
                hi_, lo_, n_ = acc
                base = pl.multiple_of(c * k_walk, k_walk)
                for t in range(tiles):
                    x = sc_ref[r0:r0 + rb, pl.ds(base + t * LANES, LANES)]
                    visible = x > NEG_INF
                    hi_ = jnp.maximum(hi_, x)
                    lo_ = jnp.minimum(lo_, jnp.where(visible, x, pos_inf))
                    n_ = n_ + jnp.where(visible, 1.0, 0.0)
                return hi_, lo_, n_
            seed = tuple(jnp.full((rb, LANES), v, F32) for v in (NEG_INF, pos_inf, 0.0))
            parts.append(lax.fori_loop(0, n_walk, body, seed))
        hi_, lo_, n_ = (jnp.concatenate([p[i] for p in parts], axis=0) for i in range(3))
        return (jnp.max(hi_, axis=1, keepdims=True), jnp.min(lo_, axis=1, keepdims=True),
                jnp.sum(n_, axis=1, keepdims=True))

    mx, mn, n_visible = range_and_count()

    def live_rows(lo, hi, cnt):
        mid = lo + 0.5 * (hi - lo)
        splittable = jnp.logical_and(mid > lo, mid < hi)
        return jnp.where(jnp.logical_and(cnt > n_sel, splittable), 1.0, 0.0)

    def halve(lo, hi, cnt):
        mid = lo + 0.5 * (hi - lo)
        midb = jnp.broadcast_to(mid, (rows, LANES))
        c_mid = count(lambda x, rs: x >= midb[rs])
        enough = c_mid >= n_sel
        return jnp.where(enough, mid, lo), jnp.where(enough, hi, mid), jnp.where(enough, c_mid, cnt)

    def close_tied(lo, hi, cnt):
        lob = jnp.broadcast_to(lo, (rows, LANES))
        low = jnp.min(lane_fold(lambda acc, x, rs: jnp.minimum(acc, jnp.where(x >= lob[rs], x, pos_inf)), pos_inf),
                      axis=1, keepdims=True)
        lowb = jnp.broadcast_to(low, (rows, LANES))
        tied = count(lambda x, rs: x > lowb[rs]) < n_sel
        return lo, jnp.where(tied, lo, hi), cnt

    def halve_twice(lo, hi, cnt):
        return halve(*halve(lo, hi, cnt))

    def bisect(carry):
        it, _, lo, hi, cnt = carry
        check = jnp.logical_and(it >= TIE_CHECK_FIRST, (it - TIE_CHECK_FIRST) % TIE_CHECK_EVERY == 0)
        lo, hi, cnt = lax.cond(check, close_tied, halve_twice, lo, hi, cnt)
        return it + 1, jnp.max(live_rows(lo, hi, cnt)), lo, hi, cnt

    def unfinished(carry):
        return jnp.logical_and(carry[0] < BISECT_ITERS, carry[1] > 0.0)

    _, _, lo, _, cnt = lax.while_loop(
        unfinished, bisect, (jnp.int32(0), jnp.max(live_rows(mn, mx, n_visible)), mn, mx, n_visible))
    lob = jnp.broadcast_to(lo, (rows, LANES))
    has_ties = jnp.max(cnt) > n_sel

    @pl.when(jnp.logical_not(has_ties))
    def _():
        def body(c, _):
            base = pl.multiple_of(c * k_walk, k_walk)
            for t in range(tiles):
                col = pl.ds(base + t * LANES, LANES)
                sc_ref[:, col] = jnp.where(sc_ref[:, col] >= lob, on, off)
            return 0
        lax.fori_loop(0, n_walk, body, 0)

    @pl.when(has_ties)
    def _():
        thr = jnp.min(lane_fold(lambda acc, x, rs: jnp.minimum(acc, jnp.where(x >= lob[rs], x, pos_inf)), pos_inf),
                      axis=1, keepdims=True)
        thrb = jnp.broadcast_to(thr, (rows, LANES))
        need = n_sel - count(lambda x, rs: x > thrb[rs])
        upper = (lax.broadcasted_iota(jnp.int32, (kc, kc), 0) <= lax.broadcasted_iota(jnp.int32, (kc, kc), 1))
        upper = jnp.where(upper, 1.0, 0.0).astype(BF16)

        def body(c, seen):
            col = pl.ds(pl.multiple_of(c * kc, kc), kc)
            x = sc_ref[:, col]
            eq = jnp.where(x == thr, 1.0, 0.0)
            rank = _dot(eq.astype(BF16), upper) + seen
            keep_eq = jnp.where(rank <= need, eq, 0.0)
            sc_ref[:, col] = jnp.where(x > thr, on, jnp.where(keep_eq > 0.5, on, off))
            return seen + jnp.sum(eq, axis=1, keepdims=True)
        lax.fori_loop(0, nkc, body, jnp.zeros((rows, 1), F32))


def _dsa_body(qi_ref, kw_ref, ki2_ref, q_ref, k_ref, v_ref, out_ref, sc_ref, wb_ref, qis_ref,
              qs_ref, s_buf, stat_ref, lsum_ref, acc_ref, *, tq, kc, ks, kb, n_sel):
    i = pl.program_id(1)
    nkc = ((i + 1) * tq + kc - 1) // kc

    kw = kw_ref[...]
    for h in range(H_IDX):
        wb_ref[h] = jnp.broadcast_to(kw[:, D_IDX + h:D_IDX + h + 1] * (H_IDX ** -0.5), (tq, LANES))
        j, parity = divmod(h, 2)
        qi_pair = qi_ref[:, j * LANES:(j + 1) * LANES]
        qis_ref[h * tq:(h + 1) * tq, :] = jnp.where(_pair_mask(qi_pair.shape, parity), qi_pair, jnp.zeros_like(qi_pair))
    qpos = i * tq + lax.broadcasted_iota(jnp.int32, (tq, LANES), 0)
    kiota = lax.broadcasted_iota(jnp.int32, (tq, LANES), 1)

    def score_chunk(c, _):
        base = pl.multiple_of(c * ks, ks)
        d = _dot_nt(qis_ref[...], ki2_ref[pl.ds(base, ks), :])
        for t in range(ks // LANES):
            acc = None
            for h in range(H_IDX):
                term = jnp.maximum(d[h * tq:(h + 1) * tq, t * LANES:(t + 1) * LANES], 0.0) * wb_ref[h]
                acc = term if acc is None else acc + term
            kpos = base + t * LANES + kiota
            sc_ref[:, pl.ds(base + t * LANES, LANES)] = jnp.where(kpos <= qpos, acc, NEG_INF)
        return 0

    n_big = ((i + 1) * tq + kb - 1) // kb
    n_score = ((i + 1) * tq + ks - 1) // ks
    lax.fori_loop(0, n_score, score_chunk, 0)

    def fill_hidden(c, _):
        sc_ref[:, pl.ds(pl.multiple_of(c * ks, ks), ks)] = jnp.full((tq, ks), NEG_INF, F32)
        return 0

    lax.fori_loop(n_score, n_big * (kb // ks), fill_hidden, 0)
    _topk_to_bias(sc_ref, nkc, kc, n_sel)

    _stack_pair_queries(q_ref, qs_ref, tq)

    def logits(j, c, raw, masked):
        bias = sc_ref[:, pl.ds(pl.multiple_of(c * kb, kb), kb)]
        return raw + jnp.concatenate([bias, bias], axis=0)

    _attend_pairs(qs_ref, k_ref, v_ref, out_ref, s_buf, stat_ref, lsum_ref, acc_ref, tq=tq, kb=kb,
                  n_plain=n_big, n_chunks=n_big, logits_fn=logits)


def _dsa_prompt(qi, kw, ki2, q, kb, vb, n, s, tq, kc, kbig):
    hb = q.shape[1]
    nq = s // tq
    n_sel = min(TOPK_MAX, s // 4)
    ks = min(kc, 4 * LANES)
    body = functools.partial(_dsa_body, tq=tq, kc=kc, ks=ks, kb=kbig, n_sel=n_sel)
    scratch = [pltpu.VMEM((tq, s), F32), pltpu.VMEM((H_IDX, tq, LANES), F32), pltpu.VMEM((H_IDX * tq, LANES), BF16)]
    scratch += _pair_scratch(hb // LANES, tq, s)
    one = pl.Buffered(1)
    qrow = lambda b, i: (b * nq + i, 0)
    return pl.pallas_call(
        body, grid=(n, nq),
        in_specs=[pl.BlockSpec((tq, hb), qrow), pl.BlockSpec((tq, LANES), qrow),
                  pl.BlockSpec((s, LANES), lambda b, i: (b, 0), pipeline_mode=one),
                  pl.BlockSpec((tq, hb), qrow),
                  pl.BlockSpec((s, hb), lambda b, i: (b, 0), pipeline_mode=one),
                  pl.BlockSpec((s, hb), lambda b, i: (b, 0), pipeline_mode=one)],
        out_specs=pl.BlockSpec((tq, hb), qrow),
        out_shape=jax.ShapeDtypeStruct((n * s, hb), BF16),
        scratch_shapes=scratch, compiler_params=_params(), name="dsa_prompt")(qi, kw, ki2, q, kb, vb)


def _page_specs(pool, li, pages_per_step, n_pages):
    def spec(r):
        def index(n, p, pt):
            return (li, pt[n, jnp.minimum(p * pages_per_step + r, n_pages - 1)]) + (0,) * (pool.ndim - 2)
        return pl.BlockSpec((1, 1) + pool.shape[2:], index)
    return [spec(r) for r in range(pages_per_step)]


def _idx_scores_body(pt_ref, qi_ref, w_ref, *rest, pages_per_step, n_main, page, t_new):
    page_refs = rest[:pages_per_step]
    new_ref, out_ref = rest[pages_per_step], rest[pages_per_step + 1]
    p = pl.program_id(1)
    qi = qi_ref[0]
    w = w_ref[0] * (H_IDX ** -0.5)

    def scores(ki_t):
        d = _dot(qi, ki_t.astype(BF16))
        contrib = jnp.maximum(d, 0.0) * w
        return jnp.sum(contrib.reshape(t_new, H_IDX, page), axis=1)

    @pl.when(p < n_main)
    def _():
        for r in range(pages_per_step):
            off = pl.multiple_of((p * pages_per_step + r) * page, page)
            out_ref[0, :, pl.ds(off, page)] = scores(page_refs[r][0, 0])

    @pl.when(p == n_main)
    def _():
        sc = scores(new_ref[0])
        t = lax.broadcasted_iota(jnp.int32, sc.shape, 0)
        j = lax.broadcasted_iota(jnp.int32, sc.shape, 1)
        out_ref[0, :, n_main * pages_per_step * page:] = jnp.where(j <= t, sc, NEG_INF)


def _idx_scores_sample(page_table, qi_rows, w_rows, ki_pool, li, ki_new_pages, pages_per_step):
    nseq, n_pages = page_table.shape
    page = ki_pool.shape[3]
    t_new = qi_rows.shape[1] // H_IDX
    n_main = n_pages // pages_per_step
    length = (n_pages + 1) * page
    body = functools.partial(_idx_scores_body, pages_per_step=pages_per_step, n_main=n_main, page=page, t_new=t_new)
    per_seq = lambda n, p, pt: (n, 0, 0)
    grid_spec = pltpu.PrefetchScalarGridSpec(
        num_scalar_prefetch=1, grid=(nseq, n_main + 1),
        in_specs=[pl.BlockSpec((1,) + qi_rows.shape[1:], per_seq), pl.BlockSpec((1,) + w_rows.shape[1:], per_seq)]
        + _page_specs(ki_pool, li, pages_per_step, n_pages)
        + [pl.BlockSpec((1,) + ki_new_pages.shape[1:], per_seq)],
        out_specs=pl.BlockSpec((1, t_new, length), per_seq))
    return pl.pallas_call(
        body, grid_spec=grid_spec, out_shape=jax.ShapeDtypeStruct((nseq, t_new, length), F32),
        compiler_params=_params(), name="idx_scores_sample")(
            page_table, qi_rows, w_rows, *([ki_pool] * pages_per_step), ki_new_pages)


def _topk_rows_body(sc_ref, out_ref, *, kc, n_sel):
    out_ref[...] = sc_ref[...]
    _topk_to_bias(out_ref, out_ref.shape[1] // kc, kc, n_sel, on=1.0, off=0.0)


def _topk_rows(scores, n_sel, rows, kc):
    r, length = scores.shape
    body = functools.partial(_topk_rows_body, kc=kc, n_sel=n_sel)
    return pl.pallas_call(
        body, grid=(r // rows,), in_specs=[pl.BlockSpec((rows, length), lambda i: (i, 0))],
        out_specs=pl.BlockSpec((rows, length), lambda i: (i, 0)),
        out_shape=jax.ShapeDtypeStruct((r, length), F32), compiler_params=_params(), name="topk_rows")(scores)


def _split3(x):
    hi = x.astype(BF16)
    r1 = x - hi.astype(F32)
    mid = r1.astype(BF16)
    lo = (r1 - mid.astype(F32)).astype(BF16)
    return hi, mid, lo


def _decay_bias_body(pt_ref, *rest, pages_per_step, n_pages, n_heads):
    page_refs = rest[:pages_per_step]
    new_ref, out_ref, lf_ref = rest[pages_per_step:]
    p = pl.program_id(1)
    n_rows, page = lf_ref.shape

    @pl.when(p == 0)
    def _():
        lf_ref[n_pages * n_heads:, :] = new_ref[0]

    for r in range(pages_per_step):
        row0 = pl.multiple_of((p * pages_per_step + r) * n_heads, n_heads)
        lf_ref[pl.ds(row0, n_heads), :] = page_refs[r][0, 0]

    @pl.when(p == pl.num_programs(1) - 1)
    def _():
        upto = jnp.where(lax.broadcasted_iota(jnp.int32, (page, page), 0)
                         <= lax.broadcasted_iota(jnp.int32, (page, page), 1), 1.0, 0.0).astype(BF16)
        src = lax.broadcasted_iota(jnp.int32, (n_rows, n_rows), 1)
        dst = lax.broadcasted_iota(jnp.int32, (n_rows, n_rows), 0)
        earlier_pages = jnp.where((src % n_heads) == (dst % n_heads),
                                  jnp.where(src // n_heads < dst // n_heads, 1.0, 0.0), 0.0).astype(BF16)
        cum = sum(_dot(x, upto) for x in _split3(lf_ref[...]))
        totals = jnp.broadcast_to(cum[:, page - 1:], cum.shape)
        cum = cum + sum(_dot(earlier_pages, x) for x in _split3(totals))
        out_ref[0] = -cum


def _decay_bias_sample(page_table, f_pool_t, li, f_new_t):
    nseq, n_pages = page_table.shape
    n_heads, page = f_pool_t.shape[2:]
    pages_per_step = _pages_per_step(n_pages, 32)
    n_rows = (n_pages + 1) * n_heads
    per_seq = lambda n, p, pt: (n, 0, 0)
    body = functools.partial(_decay_bias_body, pages_per_step=pages_per_step, n_pages=n_pages, n_heads=n_heads)
    grid_spec = pltpu.PrefetchScalarGridSpec(
        num_scalar_prefetch=1, grid=(nseq, n_pages // pages_per_step),
        in_specs=_page_specs(f_pool_t, li, pages_per_step, n_pages) + [pl.BlockSpec((1, n_heads, page), per_seq)],
        out_specs=pl.BlockSpec((1, n_rows, page), per_seq), scratch_shapes=[pltpu.VMEM((n_rows, page), F32)])
    return pl.pallas_call(
        body, grid_spec=grid_spec, out_shape=jax.ShapeDtypeStruct((nseq, n_rows, page), F32),
        compiler_params=_params(), name="decay_bias_sample")(
            page_table, *([f_pool_t] * pages_per_step), f_new_t)


def _paged_attn_body(pt_ref, q_ref, *rest, pages_per_step, n_main, page, t_new, n_heads, decay):
    it = iter(rest)
    k_refs = [next(it) for _ in range(pages_per_step)]
    v_refs = [next(it) for _ in range(pages_per_step)]
    knew_ref, vnew_ref = next(it), next(it)
    side_ref = next(it)
    out_ref = next(it)
    m_ref, l_ref, acc_ref = next(it), next(it), next(it)
    p = pl.program_id(1)
    rows = t_new * n_heads
    q = q_ref[0]

    @pl.when(p == 0)
    def _():
        m_ref[...] = jnp.full(m_ref.shape, M_INIT, F32)
        l_ref[...] = jnp.zeros(l_ref.shape, F32)
        acc_ref[...] = jnp.zeros(acc_ref.shape, F32)

    def page_logits(k_t, page_idx, new_page):
        s = _dot(q, k_t.astype(BF16))
        if decay:
            neg_cum = side_ref[0, pl.ds(pl.multiple_of(page_idx * n_heads, n_heads), n_heads), :]
            s = s + jnp.concatenate([neg_cum] * t_new, axis=0)
            if new_page:
                t = lax.broadcasted_iota(jnp.int32, s.shape, 0) // n_heads
                s = jnp.where(lax.broadcasted_iota(jnp.int32, s.shape, 1) <= t, s, NEG_INF)
            return s
        sel = side_ref[0, :, pl.ds(pl.multiple_of(page_idx * page, page), page)]
        sel_rows = jnp.broadcast_to(sel[:, None, :], (t_new, n_heads, page)).reshape(rows, page)
        return jnp.where(sel_rows > 0.5, s, NEG_INF)

    def update(logits, values_t):
        m = m_ref[...]
        m_new = m
        for s in logits:
            m_new = jnp.maximum(m_new, jnp.max(s, axis=1, keepdims=True))
        alpha = jnp.exp(m - m_new)
        l_new = alpha * l_ref[...]
        acc = alpha * acc_ref[...]
        for s, v_t in zip(logits, values_t):
            pr = jnp.exp(s - m_new)
            l_new = l_new + jnp.sum(pr, axis=1, keepdims=True)
            acc = acc + _dot_nt(pr.astype(BF16), v_t.astype(BF16))
        l_ref[...] = l_new
        acc_ref[...] = acc
        m_ref[...] = m_new

    @pl.when(p < n_main)
    def _():
        update([page_logits(k_refs[r][0, 0], p * pages_per_step + r, False) for r in range(pages_per_step)],
               [v_refs[r][0, 0] for r in range(pages_per_step)])

    @pl.when(p == n_main)
    def _():
        update([page_logits(knew_ref[0], n_main * pages_per_step, True)], [vnew_ref[0]])
        o = acc_ref[...] / l_ref[...]
        head_of_row = lax.broadcasted_iota(jnp.int32, o.shape, 0) % n_heads
        head_of_lane = lax.broadcasted_iota(jnp.int32, o.shape, 1) // HEAD_DIM
        o = jnp.where(head_of_row == head_of_lane, o, 0.0)
        out_ref[0] = jnp.sum(o.reshape(t_new, n_heads, o.shape[1]), axis=1).astype(BF16)


def _paged_attn(page_table, q_rows, k_pool_t, v_pool_t, li, k_new_t, v_new_t, pages_per_step,
                neg_cum=None, selected=None):
    nseq, n_pages = page_table.shape
    width, page = k_pool_t.shape[2:]
    n_heads = width // HEAD_DIM
    rows = q_rows.shape[1]
    t_new = rows // n_heads
    n_main = n_pages // pages_per_step
    decay = neg_cum is not None
    side = neg_cum if decay else selected
    per_seq = lambda n, p, pt: (n, 0, 0)
    body = functools.partial(_paged_attn_body, pages_per_step=pages_per_step, n_main=n_main, page=page,
                             t_new=t_new, n_heads=n_heads, decay=decay)
    args = [page_table, q_rows] + [k_pool_t] * pages_per_step + [v_pool_t] * pages_per_step + [k_new_t, v_new_t, side]
    in_specs = [pl.BlockSpec((1,) + q_rows.shape[1:], per_seq)]
    in_specs += _page_specs(k_pool_t, li, pages_per_step, n_pages) + _page_specs(v_pool_t, li, pages_per_step, n_pages)
    in_specs += [pl.BlockSpec((1, width, page), per_seq), pl.BlockSpec((1, width, page), per_seq),
                 pl.BlockSpec((1,) + side.shape[1:], per_seq)]
    scratch = [pltpu.VMEM((rows, 1), F32), pltpu.VMEM((rows, 1), F32), pltpu.VMEM((rows, width), F32)]
    grid_spec = pltpu.PrefetchScalarGridSpec(
        num_scalar_prefetch=1, grid=(nseq, n_main + 1), in_specs=in_specs,
        out_specs=pl.BlockSpec((1, t_new, width), per_seq), scratch_shapes=scratch)
    return pl.pallas_call(
        body, grid_spec=grid_spec, out_shape=jax.ShapeDtypeStruct((nseq, t_new, width), BF16),
        compiler_params=_params(), name="paged_attn_decay" if decay else "paged_attn_select")(*args)


def _rope_tables(pos):
    half = HEAD_DIM // 2
    inv = ROPE_THETA ** (-jnp.arange(half, dtype=F32) / half)
    ang = pos.astype(F32)[:, None] * inv[None, :]
    cos, sin = jnp.cos(ang), jnp.sin(ang)
    reps = LANES // HEAD_DIM
    return jnp.tile(jnp.concatenate([cos, cos], axis=1), (1, reps)), jnp.tile(jnp.concatenate([-sin, sin], axis=1), (1, reps))


def _block_diag(w):
    g, bw, _ = w.shape
    eye = jnp.eye(g, dtype=w.dtype)
    return (w[:, :, None, :] * eye[:, None, :, None]).reshape(g * bw, g * bw)


def _pad_cols(w, width):
    return jnp.pad(w, ((0, 0), (0, width - w.shape[1])))


def _to_time_major(a):
    return jnp.swapaxes(a, 0, 1).reshape((a.shape[0] * a.shape[1],) + a.shape[2:])


def _from_time_major(a, n):
    return jnp.swapaxes(a.reshape((a.shape[0] // n, n) + a.shape[1:]), 0, 1)


def _expand_heads(a, n, n_heads):
    width = a.shape[1]
    per_seq = _from_time_major(a, n)
    mask = (jnp.arange(width)[None, :] // HEAD_DIM) == jnp.arange(n_heads)[:, None]
    out = jnp.where(mask[None, None], per_seq[:, :, None, :], jnp.zeros((), a.dtype))
    return out.reshape(n, -1, width)


def _new_pages_t(a, n, page):
    return jnp.swapaxes(_new_pages(a, n, page), 1, 2)


def _key_minor(pool):
    moved = jnp.moveaxis(pool, 2, -1)
    return moved.reshape(moved.shape[:2] + (-1, moved.shape[-1]))


def _new_pages(a, n, page):
    per_seq = _from_time_major(a, n)
    return jnp.pad(per_seq, ((0, 0), (0, page - per_seq.shape[1]), (0, 0)))


def _heads_from_key_minor(a):
    n, width, s = a.shape
    return jnp.transpose(a.reshape(n, width // HEAD_DIM, HEAD_DIM, s), (0, 3, 1, 2))


def _pages_per_step(n_pages, pref):
    while n_pages % pref:
        pref //= 2
    return pref


def _pick_tile(m, pref):
    t = min(m, pref)
    while m % t:
        t //= 2
    return t


def kernel(x_prompt, x_sample, cache_k_b, cache_v_b, cache_kidx_b, state_conv_a, cache_k_c, cache_v_c, cache_logf_c, state_conv_d, state_lru_h, page_table, g_ff1, w_ff1_gate, w_ff1_up, w_ff1_down, g_mix, g_ff2, w_ff2_gate, w_ff2_up, w_ff2_down, w_in_e, conv_a_w, w_out_e, w_in_o, b_f, conv_d_w, conv_d_b, w_rg_a, b_rg_a, w_rg_x, b_rg_x, lam, w_out_o, g_final):
    batch, seq, d_model = x_prompt.shape
    nseq, t_new, _ = x_sample.shape
    depth = g_ff1.shape[0]
    n_pool, page = cache_k_b.shape[1], cache_k_b.shape[2]
    n_pages = page_table.shape[1]
    past = n_pages * page
    wa = conv_a_w.shape[2]
    hb = w_out_e.shape[1] - wa
    wd = conv_d_w.shape[2]
    hc = w_out_o.shape[1] - wd
    n_heads_c = hc // HEAD_DIM
    pages_per_step = _pages_per_step(n_pages, 16)
    idx_pages_per_step = _pages_per_step(n_pages, 32)

    mp, ms = batch * seq, nseq * t_new
    tm_p = _pick_tile(seq, 512)
    xp = x_prompt.reshape(mp, d_model)
    xs = _to_time_major(x_sample)
    cos_p, sin_p = _rope_tables(jnp.tile(jnp.arange(seq), batch))
    cos_s, sin_s = _rope_tables(past + jnp.repeat(jnp.arange(t_new), nseq))
    row2 = lambda v: v.reshape(1, -1)
    bf = lambda w: w.astype(BF16)

    outs_p, outs_s = {}, {}
    for l in range(depth):
        ffn1 = (row2(g_ff1[l]), bf(w_ff1_gate[l]), bf(w_ff1_up[l]), bf(w_ff1_down[l]))
        ffn2 = (row2(g_ff2[l]), bf(w_ff2_gate[l]), bf(w_ff2_up[l]), bf(w_ff2_down[l]))
        xp = _ffn(xp, *ffn1, tm=tm_p)
        xs = _ffn(xs, *ffn1, tm=ms)
        g = row2(g_mix[l])
        if l % 2 == 0:
            e = l // 2
            w = w_in_e[e]
            o_q, o_k, o_v, o_qi, o_kw = 3 * wa, 3 * wa + hb, 3 * wa + 2 * hb, 3 * wa + 3 * hb, 3 * wa + 3 * hb + H_IDX * D_IDX
            weights = (bf(w[:, :o_q]), bf(w[:, o_q:o_k]), bf(w[:, o_k:o_v]), bf(w[:, o_v:o_qi]), bf(w[:, o_qi:o_kw]),
                       bf(_pad_cols(w[:, o_kw:], LANES)))
            w_out = (bf(w_out_e[e][:wa]), bf(w_out_e[e][wa:]))

            hist_p = jnp.zeros((batch, conv_a_w.shape[1] - 1, wa), F32)
            ya, q, k32, kb, v32, vb, qi, kw, ki2, cstate = _proj_even(
                xp, g, weights, cos_p, sin_p, conv_a_w[e], hist_p, tm_p, seq // tm_p, 1, True)
            yb = _dsa_prompt(qi, kw, ki2, q, kb, vb, batch, seq, _pick_tile(seq, 256), _pick_tile(seq, 512),
                             _pick_tile(seq, 1024))
            outs_p.setdefault("k_b", []).append(_heads_from_key_minor(k32))
            outs_p.setdefault("v_b", []).append(_heads_from_key_minor(v32))
            outs_p.setdefault("kidx_b", []).append(kw[:, :D_IDX].reshape(batch, seq, D_IDX))
            outs_p.setdefault("conv_a", []).append(cstate)
            mix_p = (ya, yb, *w_out)

            hist_s = _to_time_major(state_conv_a[e])[None]
            ya, q, k32, kb, v32, vb, qi, kw, ki2, cstate = _proj_even(
                xs, g, weights, cos_s, sin_s, conv_a_w[e], hist_s, ms, 1, nseq, False)
            qi_rows = _from_time_major(qi, nseq).reshape(nseq, t_new * H_IDX, D_IDX)
            w_rows = _from_time_major(kw[:, D_IDX:D_IDX + H_IDX], nseq).reshape(nseq, t_new * H_IDX, 1)
            scores = _idx_scores_sample(page_table, qi_rows, w_rows, _key_minor(cache_kidx_b), e,
                                        _new_pages_t(kw[:, :D_IDX], nseq, page), idx_pages_per_step)
            length = scores.shape[2]
            n_sel = min(TOPK_MAX, (past + t_new) // 4)
            kc = page * 3 if (length // page) % 3 == 0 else page
            picked = _topk_rows(scores.reshape(ms, length), n_sel, _pick_tile(ms, 32), kc).reshape(nseq, t_new, length)
            yb = _paged_attn(page_table, _expand_heads(q, nseq, hb // HEAD_DIM), _key_minor(cache_k_b),
                             _key_minor(cache_v_b), e, _new_pages_t(k32, nseq, page), _new_pages_t(v32, nseq, page),
                             pages_per_step, selected=picked)
            yb = _to_time_major(yb)
            outs_s.setdefault("k_b", []).append(_from_time_major(k32, nseq).reshape(nseq, t_new, hb // HEAD_DIM, HEAD_DIM))
            outs_s.setdefault("v_b", []).append(_from_time_major(v32, nseq).reshape(nseq, t_new, hb // HEAD_DIM, HEAD_DIM))
            outs_s.setdefault("kidx_b", []).append(_from_time_major(kw[:, :D_IDX], nseq))
            outs_s.setdefault("conv_a", []).append(_from_time_major(cstate[0], nseq))
            mix_s = (ya, yb, *w_out)
        else:
            o = l // 2
            w = w_in_o[o]
            o_f, o_xd, o_gd = 3 * hc, 3 * hc + n_heads_c, 3 * hc + n_heads_c + wd
            weights = (bf(w[:, :o_f]), bf(_pad_cols(w[:, o_f:o_xd], LANES)), bf(w[:, o_xd:o_gd]), bf(w[:, o_gd:]))
            small = (_pad_cols(row2(b_f[o]), LANES), conv_d_w[o], row2(conv_d_b[o]), bf(_block_diag(w_rg_a[o])),
                     row2(b_rg_a[o]), bf(_block_diag(w_rg_x[o])), row2(b_rg_x[o]), row2(lam[o]))
            w_out = (bf(w_out_o[o][:hc]), bf(w_out_o[o][hc:]))

            hist_p = jnp.zeros((batch, conv_d_w.shape[1] - 1, wd), F32)
            h0_p = jnp.zeros((batch, 1, wd), F32)
            q, k32, kb, v32, vb, logf, yd, cstate, hlast = _proj_odd(
                xp, g, weights, small, hist_p, h0_p, tm_p, seq // tm_p, 1, 0, True)
            cum_t = _cumsum_heads(logf, batch, seq, _pick_tile(seq, 512))
            yc = _fox_prompt(q, kb, vb, cum_t, batch, seq, _pick_tile(seq, 512), _pick_tile(seq, 1024))
            outs_p.setdefault("k_c", []).append(_heads_from_key_minor(k32))
            outs_p.setdefault("v_c", []).append(_heads_from_key_minor(v32))
            outs_p.setdefault("logf_c", []).append(logf.reshape(batch, seq, n_heads_c))
            outs_p.setdefault("conv_d", []).append(cstate)
            outs_p.setdefault("lru_h", []).append(hlast[:, 0])
            mix_p = (yc, yd, *w_out)

            hist_s = _to_time_major(state_conv_d[o])[None]
            h0_s = state_lru_h[o][None]
            q, k32, kb, v32, vb, logf, yd, cstate, hlast = _proj_odd(
                xs, g, weights, small, hist_s, h0_s, ms, 1, nseq, past, False)
            neg_cum = _decay_bias_sample(page_table, _key_minor(cache_logf_c), o, _new_pages_t(logf, nseq, page))
            yc = _paged_attn(page_table, _expand_heads(q, nseq, n_heads_c), _key_minor(cache_k_c),
                             _key_minor(cache_v_c), o, _new_pages_t(k32, nseq, page), _new_pages_t(v32, nseq, page),
                             pages_per_step, neg_cum=neg_cum)
            yc = _to_time_major(yc)
            outs_s.setdefault("k_c", []).append(_from_time_major(k32, nseq).reshape(nseq, t_new, n_heads_c, HEAD_DIM))
            outs_s.setdefault("v_c", []).append(_from_time_major(v32, nseq).reshape(nseq, t_new, n_heads_c, HEAD_DIM))
            outs_s.setdefault("logf_c", []).append(_from_time_major(logf, nseq))
            outs_s.setdefault("conv_d", []).append(_from_time_major(cstate[0], nseq))
            outs_s.setdefault("lru_h", []).append(hlast[0])
            mix_s = (yc, yd, *w_out)
        gfin = row2(g_final) if l == depth - 1 else None
        xp = _ffn(xp, *ffn2, tm=tm_p, mix=mix_p, g_final=gfin)
        xs = _ffn(xs, *ffn2, tm=ms, mix=mix_s, g_final=gfin)

    names = ("k_b", "v_b", "kidx_b", "conv_a", "k_c", "v_c", "logf_c", "conv_d", "lru_h")
    st_p = tuple(jnp.stack(outs_p[nm]) for nm in names)
    st_s = tuple(jnp.stack(outs_s[nm]) for nm in names)
    return (xp.reshape(batch, seq, d_model), _from_time_major(xs, nseq)) + st_p + st_s
```

```python
import functools

import jax
import jax.numpy as jnp
import numpy as np
from jax import lax
from jax.experimental import pallas as pl
from jax.experimental.pallas import tpu as pltpu

F32 = jnp.float32
BF16 = jnp.bfloat16

HEAD_DIM = 64
D_IDX = 64
H_IDX = 8
TOPK_MAX = 256
LRU_BLOCKS = 8
RG_C = 8.0
ROPE_THETA = 10000.0
EPS = 1e-6

LANES = 128
SUBLANES = 8
VMEM_LIMIT = 56 * 1024 * 1024
NEG_INF = float("-inf")
M_INIT = -1e30
BISECT_ITERS = 400
TIE_CHECK_FIRST = 10
TIE_CHECK_EVERY = 4
TOPK_ROW_BLOCK = 128

_NT = (((1,), (1,)), ((), ()))


def _params():
    return pltpu.CompilerParams(vmem_limit_bytes=VMEM_LIMIT)


def _const_spec(shape):
    nd = len(shape)
    return pl.BlockSpec(shape, lambda *_: (0,) * nd, pipeline_mode=pl.Buffered(1))


def _dot(a, b):
    return jnp.dot(a, b, preferred_element_type=F32)


def _dot_nt(a, b, precision=None):
    return lax.dot_general(a, b, _NT, preferred_element_type=F32, precision=precision)


def _rmsnorm(x, g):
    ms = jnp.mean(x * x, axis=-1, keepdims=True)
    return x * lax.rsqrt(ms + EPS) * g


def _ffn_body(*refs, n_chunks, fc, has_mix, has_final):
    it = iter(refs)
    x_ref = next(it)
    if has_mix:
        ya_ref, yb_ref, woa_ref, wob_ref = next(it), next(it), next(it), next(it)
    g_ref, wg_ref, wu_ref, wd_ref = next(it), next(it), next(it), next(it)
    if has_final:
        gf_ref = next(it)
    out_ref = next(it)

    x = x_ref[...]
    if has_mix:
        x = x + _dot(ya_ref[...], woa_ref[...]) + _dot(yb_ref[...], wob_ref[...])
    h = _rmsnorm(x, g_ref[...]).astype(BF16)
    acc = None
    for c in range(n_chunks):
        gate = _dot(h, wg_ref[:, c * fc:(c + 1) * fc])
        up = _dot(h, wu_ref[:, c * fc:(c + 1) * fc])
        a = (gate * jax.nn.sigmoid(gate) * up).astype(BF16)
        part = _dot(a, wd_ref[c * fc:(c + 1) * fc, :])
        acc = part if acc is None else acc + part
    x = x + 0.5 * acc
    if has_final:
        x = _rmsnorm(x, gf_ref[...])
    out_ref[...] = x


def _ffn(x, g, wg, wu, wd, tm, mix=None, g_final=None):
    m, d = x.shape
    f = wg.shape[1]
    n_chunks = 2 if (f // 2) % LANES == 0 else 1
    fc = f // n_chunks
    row = lambda i: (i, 0)
    args, specs = [x], [pl.BlockSpec((tm, d), row)]
    if mix is not None:
        ya, yb, woa, wob = mix
        args += [ya, yb, woa, wob]
        specs += [pl.BlockSpec((tm, ya.shape[1]), row), pl.BlockSpec((tm, yb.shape[1]), row),
                  _const_spec(woa.shape), _const_spec(wob.shape)]
    args += [g, wg, wu, wd]
    specs += [_const_spec(g.shape), _const_spec(wg.shape), _const_spec(wu.shape), _const_spec(wd.shape)]
    if g_final is not None:
        args.append(g_final)
        specs.append(_const_spec(g_final.shape))
    body = functools.partial(_ffn_body, n_chunks=n_chunks, fc=fc, has_mix=mix is not None,
                             has_final=g_final is not None)
    return pl.pallas_call(
        body, grid=(m // tm,), in_specs=specs, out_specs=pl.BlockSpec((tm, d), row),
        out_shape=jax.ShapeDtypeStruct((m, d), F32), compiler_params=_params(), name="ffn")(*args)


def _rope_tile(xt, cos, sin_signed):
    lane = lax.broadcasted_iota(jnp.int32, xt.shape, 1)
    first_half = (lane % HEAD_DIM) < (HEAD_DIM // 2)
    partner = jnp.where(first_half, pltpu.roll(xt, LANES - HEAD_DIM // 2, 1), pltpu.roll(xt, HEAD_DIM // 2, 1))
    return xt * cos + partner * sin_signed


def _store_f32(ref, cols, value, key_minor):
    if key_minor:
        ref[0, cols, :] = value.T
    else:
        ref[:, cols] = value


def _kv_out(m, width, tm, tiles_per_seq, key_minor):
    if not key_minor:
        return jax.ShapeDtypeStruct((m, width), F32), pl.BlockSpec((tm, width), lambda i: (i, 0))
    shape = (m // (tm * tiles_per_seq), width, tm * tiles_per_seq)
    return (jax.ShapeDtypeStruct(shape, F32),
            pl.BlockSpec((1, width, tm), lambda i: (i // tiles_per_seq, 0, i % tiles_per_seq)))


def _hist_rows(width, stride):
    need = (width - 1) * stride
    pad = -(-need // SUBLANES) * SUBLANES
    return need, pad


def _causal_conv(u, w_ref, ubuf, hist_ref, state_ref, first_tile, width, stride):
    tm = u.shape[0]
    need, pad = _hist_rows(width, stride)

    @pl.when(first_tile)
    def _():
        ubuf[pad - need:pad, :] = hist_ref[0]

    @pl.when(jnp.logical_not(first_tile))
    def _():
        ubuf[0:pad, :] = ubuf[tm:tm + pad, :]

    ubuf[pad:pad + tm, :] = u
    y = None
    for j in range(width):
        start = pad - (width - 1 - j) * stride
        term = w_ref[j:j + 1, :] * ubuf[start:start + tm, :]
        y = term if y is None else y + term
    state_ref[0] = ubuf[pad + tm - need:pad + tm, :]
    return y


def _proj_even_body(x_ref, g_ref, wc_ref, wq_ref, wk_ref, wv_ref, wqi_ref, wkw_ref, cos_ref, sin_ref,
                    cw_ref, hist_ref,
                    ya_ref, q_ref, k32_ref, kb_ref, v32_ref, vb_ref, qi_ref, kw_ref, ki2_ref, state_ref,
                    ubuf, *, tiles_per_seq, stride, wa, kv_key_minor):
    first_tile = (pl.program_id(0) % tiles_per_seq) == 0
    h = _rmsnorm(x_ref[...], g_ref[...]).astype(BF16)
    cos = cos_ref[...]
    sin = sin_ref[...]
    scale = HEAD_DIM ** -0.5

    zc = _dot(h, wc_ref[...])
    xin, gb, gc = zc[:, :wa], zc[:, wa:2 * wa], zc[:, 2 * wa:]
    yconv = _causal_conv(gc * xin, cw_ref, ubuf, hist_ref, state_ref, first_tile, cw_ref.shape[0], stride)
    ya_ref[...] = (gb * yconv).astype(BF16)

    n_tiles = wq_ref.shape[1] // LANES
    q = _dot(h, wq_ref[...])
    k = _dot(h, wk_ref[...])
    qi = _dot(h, wqi_ref[...])
    for t in range(n_tiles):
        sl = slice(t * LANES, (t + 1) * LANES)
        q_ref[:, sl] = (_rope_tile(q[:, sl], cos, sin) * scale).astype(BF16)
        kt = _rope_tile(k[:, sl], cos, sin)
        _store_f32(k32_ref, sl, kt, kv_key_minor)
        kb_ref[:, sl] = kt.astype(BF16)
        qi_ref[:, sl] = (_rope_tile(qi[:, sl], cos, sin) * (D_IDX ** -0.5)).astype(BF16)
    v = _dot(h, wv_ref[...])
    _store_f32(v32_ref, slice(None), v, kv_key_minor)
    vb_ref[...] = v.astype(BF16)

    kw = _dot(h, wkw_ref[...])
    lane = lax.broadcasted_iota(jnp.int32, kw.shape, 1)
    is_ki = lane < D_IDX
    kw = jnp.where(is_ki, _rope_tile(kw, cos, sin), kw)
    kw_ref[...] = kw
    ki2_ref[...] = jnp.where(is_ki, kw, pltpu.roll(kw, D_IDX, 1)).astype(BF16)


def _proj_even(x, g, w, cos, sin, conv_w, hist, tm, tiles_per_seq, stride, kv_key_minor):
    m, d = x.shape
    wc, wq, wk, wv, wqi, wkw = w
    wa = wc.shape[1] // 3
    hb = wq.shape[1]
    width = conv_w.shape[0]
    need, pad = _hist_rows(width, stride)
    nb = hist.shape[0]
    row = lambda i: (i, 0)
    seq = lambda i: (i // tiles_per_seq, 0, 0)
    in_specs = [pl.BlockSpec((tm, d), row), _const_spec(g.shape)]
    in_specs += [_const_spec(a.shape) for a in (wc, wq, wk, wv, wqi, wkw)]
    in_specs += [pl.BlockSpec((tm, LANES), row), pl.BlockSpec((tm, LANES), row), _const_spec(conv_w.shape),
                 pl.BlockSpec((1, need, wa), seq)]
    out_shape = [jax.ShapeDtypeStruct((m, wa), BF16), jax.ShapeDtypeStruct((m, hb), BF16),
                 jax.ShapeDtypeStruct((m, hb), F32), jax.ShapeDtypeStruct((m, hb), BF16),
                 jax.ShapeDtypeStruct((m, hb), F32), jax.ShapeDtypeStruct((m, hb), BF16),
                 jax.ShapeDtypeStruct((m, hb), BF16), jax.ShapeDtypeStruct((m, LANES), F32),
                 jax.ShapeDtypeStruct((m, LANES), BF16), jax.ShapeDtypeStruct((nb, need, wa), F32)]
    out_specs = [pl.BlockSpec((tm, wa), row)] + [pl.BlockSpec((tm, hb), row)] * 6
    out_specs += [pl.BlockSpec((tm, LANES), row), pl.BlockSpec((tm, LANES), row), pl.BlockSpec((1, need, wa), seq)]
    for pos in (2, 4):
        out_shape[pos], out_specs[pos] = _kv_out(m, hb, tm, tiles_per_seq, kv_key_minor)
    body = functools.partial(_proj_even_body, tiles_per_seq=tiles_per_seq, stride=stride, wa=wa,
                             kv_key_minor=kv_key_minor)
    return pl.pallas_call(
        body, grid=(m // tm,), in_specs=in_specs, out_specs=out_specs, out_shape=out_shape,
        scratch_shapes=[pltpu.VMEM((pad + tm, wa), F32)], compiler_params=_params(), name="proj_even")(
            x, g, wc, wq, wk, wv, wqi, wkw, cos, sin, conv_w, hist)


def _shift_rows(x, d, fill):
    if d % SUBLANES == 0:
        return jnp.concatenate([jnp.full((d, x.shape[1]), fill, x.dtype), x[:x.shape[0] - d]], axis=0)
    row = lax.broadcasted_iota(jnp.int32, x.shape, 0)
    return jnp.where(row >= d, pltpu.roll(x, d, 0), fill)


def _proj_odd_body(x_ref, g_ref, wqkv_ref, wf_ref, wxd_ref, wgd_ref, bf_ref, cw_ref, cb_ref, wra_ref, bra_ref,
                   wrx_ref, brx_ref, lam_ref, hist_ref, h0_ref,
                   q_ref, k32_ref, kb_ref, v32_ref, vb_ref, logf_ref, yd_ref, state_ref, hlast_ref,
                   ubuf, hcar, *, tiles_per_seq, stride, pos_base, hc, kv_key_minor):
    tile_in_seq = pl.program_id(0) % tiles_per_seq
    first_tile = tile_in_seq == 0
    tm = x_ref.shape[0]
    h = _rmsnorm(x_ref[...], g_ref[...]).astype(BF16)

    z = _dot(h, wqkv_ref[...])
    q_ref[...] = (z[:, :hc] * (HEAD_DIM ** -0.5)).astype(BF16)
    k = z[:, hc:2 * hc]
    v = z[:, 2 * hc:]
    _store_f32(k32_ref, slice(None), k, kv_key_minor)
    kb_ref[...] = k.astype(BF16)
    _store_f32(v32_ref, slice(None), v, kv_key_minor)
    vb_ref[...] = v.astype(BF16)

    zf = _dot(h, wf_ref[...]) + bf_ref[...]
    logf = jnp.minimum(zf, 0.0) - jnp.log1p(jnp.exp(-jnp.abs(zf)))
    logf_ref[...] = logf[:, :logf_ref.shape[1]]

    xd = _dot(h, wxd_ref[...])
    gd = _dot(h, wgd_ref[...])
    xc = _causal_conv(xd, cw_ref, ubuf, hist_ref, state_ref, first_tile, cw_ref.shape[0], stride) + cb_ref[...]
    xcb = xc.astype(BF16)
    r = jax.nn.sigmoid(_dot(xcb, wra_ref[...]) + bra_ref[...])
    ig = jax.nn.sigmoid(_dot(xcb, wrx_ref[...]) + brx_ref[...])
    lam = lam_ref[...]
    softplus_neg_lam = jnp.maximum(-lam, 0.0) + jnp.log1p(jnp.exp(-jnp.abs(lam)))
    log_a = -RG_C * r * softplus_neg_lam
    a = jnp.exp(log_a)
    row = lax.broadcasted_iota(jnp.int32, (tm, 1), 0)
    pos = pos_base + tile_in_seq * (tm // stride) + row // stride
    mult = jnp.where(pos == 0, 1.0, jnp.sqrt(1.0 - jnp.exp(2.0 * log_a)))
    b = mult * ig * xc

    d = stride
    while d < tm:
        a_prev = _shift_rows(a, d, 1.0)
        b_prev = _shift_rows(b, d, 0.0)
        b = a * b_prev + b
        a = a * a_prev
        d *= 2

    @pl.when(first_tile)
    def _():
        hcar[...] = h0_ref[0]

    carry = hcar[...]
    if stride == 1:
        carry_rows = carry
    else:
        carry_rows = jnp.concatenate([carry] * (tm // stride), axis=0)
    hs = a * carry_rows + b
    hcar[...] = hs[tm - stride:, :]
    hlast_ref[0] = hs[tm - stride:, :]
    yd_ref[...] = (hs * jax.nn.gelu(gd)).astype(BF16)


def _proj_odd(x, g, w, small, hist, h0, tm, tiles_per_seq, stride, pos_base, kv_key_minor):
    m, d = x.shape
    wqkv, wf, wxd, wgd = w
    bf, conv_w, conv_b, wra, bra, wrx, brx, lam = small
    hc = wqkv.shape[1] // 3
    wd_ = wxd.shape[1]
    n_heads = hc // HEAD_DIM
    width = conv_w.shape[0]
    need, pad = _hist_rows(width, stride)
    nb = hist.shape[0]
    row = lambda i: (i, 0)
    seq = lambda i: (i // tiles_per_seq, 0, 0)
    ins = [x, g, wqkv, wf, wxd, wgd, bf, conv_w, conv_b, wra, bra, wrx, brx, lam, hist, h0]
    in_specs = [pl.BlockSpec((tm, d), row)] + [_const_spec(a.shape) for a in ins[1:14]]
    in_specs += [pl.BlockSpec((1, need, wd_), seq), pl.BlockSpec((1, stride, wd_), seq)]
    out_shape = [jax.ShapeDtypeStruct((m, hc), BF16), jax.ShapeDtypeStruct((m, hc), F32),
                 jax.ShapeDtypeStruct((m, hc), BF16), jax.ShapeDtypeStruct((m, hc), F32),
                 jax.ShapeDtypeStruct((m, hc), BF16), jax.ShapeDtypeStruct((m, n_heads), F32),
                 jax.ShapeDtypeStruct((m, wd_), BF16), jax.ShapeDtypeStruct((nb, need, wd_), F32),
                 jax.ShapeDtypeStruct((nb, stride, wd_), F32)]
    out_specs = [pl.BlockSpec((tm, hc), row)] * 5 + [pl.BlockSpec((tm, n_heads), row), pl.BlockSpec((tm, wd_), row),
                                                      pl.BlockSpec((1, need, wd_), seq),
                                                      pl.BlockSpec((1, stride, wd_), seq)]
    for pos in (1, 3):
        out_shape[pos], out_specs[pos] = _kv_out(m, hc, tm, tiles_per_seq, kv_key_minor)
    body = functools.partial(_proj_odd_body, tiles_per_seq=tiles_per_seq, stride=stride, pos_base=pos_base, hc=hc,
                             kv_key_minor=kv_key_minor)
    return pl.pallas_call(
        body, grid=(m // tm,), in_specs=in_specs, out_specs=out_specs, out_shape=out_shape,
        scratch_shapes=[pltpu.VMEM((pad + tm, wd_), F32), pltpu.VMEM((stride, wd_), F32)],
        compiler_params=_params(), name="proj_odd")(*ins)


def _cumsum_lanes(lf, carry):
    t, nh = lf.shape
    eye = (lax.broadcasted_iota(jnp.int32, (nh, nh), 0) == lax.broadcasted_iota(jnp.int32, (nh, nh), 1)).astype(F32)
    lf_t = _dot_nt(eye, lf, precision=lax.Precision.HIGHEST)
    tri = (lax.broadcasted_iota(jnp.int32, (t, t), 1) <= lax.broadcasted_iota(jnp.int32, (t, t), 0)).astype(F32)
    return _dot_nt(lf_t, tri, precision=lax.Precision.HIGHEST) + carry


def _cumsum_body(lf_ref, out_ref, carry):
    @pl.when(pl.program_id(1) == 0)
    def _():
        carry[...] = jnp.zeros_like(carry)

    cum = _cumsum_lanes(lf_ref[...], carry[:, 0:1])
    out_ref[0] = cum
    carry[...] = jnp.broadcast_to(cum[:, cum.shape[1] - 1:], carry.shape)


def _cumsum_heads(logf, n, s, tc):
    nh = logf.shape[1]
    tiles = s // tc
    return pl.pallas_call(
        _cumsum_body, grid=(n, tiles),
        in_specs=[pl.BlockSpec((tc, nh), lambda b, i: (b * tiles + i, 0))],
        out_specs=pl.BlockSpec((1, nh, tc), lambda b, i: (b, 0, i)),
        out_shape=jax.ShapeDtypeStruct((n, nh, s), F32),
        scratch_shapes=[pltpu.VMEM((nh, LANES), F32)], compiler_params=_params(), name="cumsum")(logf)


def _pair_mask(shape, parity):
    lane = lax.broadcasted_iota(jnp.int32, shape, 1)
    return (lane >= HEAD_DIM) == (parity == 1)


def _stack_pair_queries(q_ref, qs_ref, tq):
    for j in range(q_ref.shape[1] // LANES):
        q_pair = q_ref[:, j * LANES:(j + 1) * LANES]
        zero = jnp.zeros_like(q_pair)
        qs_ref[j, 0:tq, :] = jnp.where(_pair_mask(q_pair.shape, 0), q_pair, zero)
        qs_ref[j, tq:2 * tq, :] = jnp.where(_pair_mask(q_pair.shape, 1), q_pair, zero)


def _attend_pairs(qs_ref, k_ref, v_ref, out_ref, s_buf, stat_ref, lsum_ref, acc_ref, *, tq, kb, n_plain, n_chunks,
                  logits_fn):
    tiles = kb // LANES
    n_pairs = qs_ref.shape[0]

    def col_of(j):
        start = j * LANES
        return pl.ds(start if isinstance(j, int) else pl.multiple_of(start, LANES), LANES)

    def pass1(j, c, masked):
        rows = pl.ds(pl.multiple_of(c * kb, kb), kb)
        s = logits_fn(j, c, _dot_nt(qs_ref[j], k_ref[rows, col_of(j)]), masked)
        s_buf[:, rows] = s
        mt = s[:, 0:LANES]
        for t in range(1, tiles):
            mt = jnp.maximum(mt, s[:, t * LANES:(t + 1) * LANES])
        stat_ref[j % 2] = jnp.maximum(stat_ref[j % 2], mt)

    def pass2(j, c):
        rows = pl.ds(pl.multiple_of(c * kb, kb), kb)
        m_rows = jnp.concatenate([stat_ref[j % 2]] * tiles, axis=1)
        p = jnp.exp(s_buf[:, rows] - m_rows)
        part = p[:, 0:LANES]
        for t in range(1, tiles):
            part = part + p[:, t * LANES:(t + 1) * LANES]
        lsum_ref[...] += part
        acc_ref[...] += _dot(p.astype(BF16), v_ref[rows, col_of(j)])

    def sweep(first, second):
        def body(c, _, masked):
            if second is not None:
                pass2(second, c)
            if first is not None:
                pass1(first, c, masked)
            return 0
        if first is None:
            lax.fori_loop(0, n_chunks, functools.partial(body, masked=False), 0)
        else:
            lax.fori_loop(0, n_plain, functools.partial(body, masked=False), 0)
            lax.fori_loop(n_plain, n_chunks, functools.partial(body, masked=True), 0)

    def open_pair(j):
        stat_ref[j % 2] = jnp.full(stat_ref.shape[1:], NEG_INF, F32)

    def close_max(j):
        m = jnp.max(stat_ref[j % 2], axis=1, keepdims=True)
        stat_ref[j % 2] = jnp.broadcast_to(m, stat_ref.shape[1:])
        lsum_ref[...] = jnp.zeros(lsum_ref.shape, F32)
        acc_ref[...] = jnp.zeros(acc_ref.shape, F32)

    def finish(j):
        o = acc_ref[...] / jnp.sum(lsum_ref[...], axis=1, keepdims=True)
        even, odd = o[0:tq], o[tq:2 * tq]
        out_ref[:, col_of(j)] = jnp.where(_pair_mask(even.shape, 1), odd, even).astype(BF16)

    open_pair(0)
    sweep(0, None)
    close_max(0)

    def stage(j, _):
        open_pair(j)
        sweep(j, j - 1)
        finish(j - 1)
        close_max(j)
        return 0

    lax.fori_loop(1, n_pairs, stage, 0)
    sweep(None, n_pairs - 1)
    finish(n_pairs - 1)


def _pair_scratch(n_pairs, tq, s):
    return [pltpu.VMEM((n_pairs, 2 * tq, LANES), BF16), pltpu.VMEM((2 * tq, s), F32),
            pltpu.VMEM((2, 2 * tq, LANES), F32), pltpu.VMEM((2 * tq, LANES), F32), pltpu.VMEM((2 * tq, LANES), F32)]


def _fox_body(q_ref, k_ref, v_ref, cum_ref, out_ref, qs_ref, s_buf, stat_ref, lsum_ref, acc_ref, *, tq, kc):
    i = pl.program_id(1)
    n_full = (i * tq) // kc
    n_tot = ((i + 1) * tq + kc - 1) // kc
    _stack_pair_queries(q_ref, qs_ref, tq)

    def logits(j, c, raw, masked):
        rows = pl.ds(pl.multiple_of(c * kc, kc), kc)
        halves = []
        for parity in (0, 1):
            sh = raw[parity * tq:(parity + 1) * tq] - cum_ref[0, pl.ds(2 * j + parity, 1), rows]
            if masked:
                qpos = i * tq + lax.broadcasted_iota(jnp.int32, (tq, kc), 0)
                kpos = c * kc + lax.broadcasted_iota(jnp.int32, (tq, kc), 1)
                sh = jnp.where(kpos <= qpos, sh, NEG_INF)
            halves.append(sh)
        return jnp.concatenate(halves, axis=0)

    _attend_pairs(qs_ref, k_ref, v_ref, out_ref, s_buf, stat_ref, lsum_ref, acc_ref, tq=tq, kb=kc,
                  n_plain=n_full, n_chunks=n_tot, logits_fn=logits)


def _fox_prompt(q, kb, vb, cum_t, n, s, tq, kc):
    hc = q.shape[1]
    nh = cum_t.shape[1]
    nq = s // tq
    body = functools.partial(_fox_body, tq=tq, kc=kc)
    scratch = _pair_scratch(hc // LANES, tq, s)
    one = pl.Buffered(1)
    return pl.pallas_call(
        body, grid=(n, nq),
        in_specs=[pl.BlockSpec((tq, hc), lambda b, i: (b * nq + i, 0)),
                  pl.BlockSpec((s, hc), lambda b, i: (b, 0), pipeline_mode=one),
                  pl.BlockSpec((s, hc), lambda b, i: (b, 0), pipeline_mode=one),
                  pl.BlockSpec((1, nh, s), lambda b, i: (b, 0, 0), pipeline_mode=one)],
        out_specs=pl.BlockSpec((tq, hc), lambda b, i: (b * nq + i, 0)),
        out_shape=jax.ShapeDtypeStruct((n * s, hc), BF16), scratch_shapes=scratch,
        compiler_params=_params(), name="fox_prompt")(
            q, kb, vb, cum_t)


def _topk_to_bias(sc_ref, nkc, kc, n_sel, on=0.0, off=NEG_INF):
    rows = sc_ref.shape[0]
    n_walk, k_walk = nkc, kc
    tiles = k_walk // LANES
    pos_inf = float("inf")

    rb = min(rows, TOPK_ROW_BLOCK)

    def lane_fold(fn, init):
        parts = []
        for r0 in range(0, rows, rb):
            rs = slice(r0, r0 + rb)

            def body(c, acc, rs=rs):
                base = pl.multiple_of(c * k_walk, k_walk)
                for t in range(tiles):
                    acc = fn(acc, sc_ref[rs, pl.ds(base + t * LANES, LANES)], rs)
                return acc
            parts.append(lax.fori_loop(0, n_walk, body, jnp.full((rb, LANES), init, F32)))
        return jnp.concatenate(parts, axis=0)

    def count(pred_fn):
        part = lane_fold(lambda acc, x, rs: acc + jnp.where(pred_fn(x, rs), 1.0, 0.0), 0.0)
        return jnp.sum(part, axis=1, keepdims=True)

    def range_and_count():
        parts = []
        for r0 in range(0, rows, rb):
            def body(c, acc, r0=r0):
                hi_, lo_, n_ = acc
                base = pl.multiple_of(c * k_walk, k_walk)
                for t in range(tiles):
                    x = sc_ref[r0:r0 + rb, pl.ds(base + t * LANES, LANES)]
                    visible = x > NEG_INF
                    hi_ = jnp.maximum(hi_, x)
                    lo_ = jnp.minimum(lo_, jnp.where(visible, x, pos_inf))
                    n_ = n_ + jnp.where(visible, 1.0, 0.0)
                return hi_, lo_, n_
            seed = tuple(jnp.full((rb, LANES), v, F32) for v in (NEG_INF, pos_inf, 0.0))
            parts.append(lax.fori_loop(0, n_walk, body, seed))
        hi_, lo_, n_ = (jnp.concatenate([p[i] for p in parts], axis=0) for i in range(3))
        return (jnp.max(hi_, axis=1, keepdims=True), jnp.min(lo_, axis=1, keepdims=True),
                jnp.sum(n_, axis=1, keepdims=True))

    mx, mn, n_visible = range_and_count()

    def live_rows(lo, hi, cnt):
        mid = lo + 0.5 * (hi - lo)
        splittable = jnp.logical_and(mid > lo, mid < hi)
        return jnp.where(jnp.logical_and(cnt > n_sel, splittable), 1.0, 0.0)

    def halve(lo, hi, cnt):
        mid = lo + 0.5 * (hi - lo)
        midb = jnp.broadcast_to(mid, (rows, LANES))
        c_mid = count(lambda x, rs: x >= midb[rs])
        enough = c_mid >= n_sel
        return jnp.where(enough, mid, lo), jnp.where(enough, hi, mid), jnp.where(enough, c_mid, cnt)

    def close_tied(lo, hi, cnt):
        lob = jnp.broadcast_to(lo, (rows, LANES))
        low = jnp.min(lane_fold(lambda acc, x, rs: jnp.minimum(acc, jnp.where(x >= lob[rs], x, pos_inf)), pos_inf),
                      axis=1, keepdims=True)
        lowb = jnp.broadcast_to(low, (rows, LANES))
        tied = count(lambda x, rs: x > lowb[rs]) < n_sel
        return lo, jnp.where(tied, lo, hi), cnt

    def halve_twice(lo, hi, cnt):
        return halve(*halve(lo, hi, cnt))

    def bisect(carry):
        it, _, lo, hi, cnt = carry
        check = jnp.logical_and(it >= TIE_CHECK_FIRST, (it - TIE_CHECK_FIRST) % TIE_CHECK_EVERY == 0)
        lo, hi, cnt = lax.cond(check, close_tied, halve_twice, lo, hi, cnt)
        return it + 1, jnp.max(live_rows(lo, hi, cnt)), lo, hi, cnt

    def unfinished(carry):
        return jnp.logical_and(carry[0] < BISECT_ITERS, carry[1] > 0.0)

    _, _, lo, _, cnt = lax.while_loop(
        unfinished, bisect, (jnp.int32(0), jnp.max(live_rows(mn, mx, n_visible)), mn, mx, n_visible))
    lob = jnp.broadcast_to(lo, (rows, LANES))
    has_ties = jnp.max(cnt) > n_sel

    @pl.when(jnp.logical_not(has_ties))
    def _():
        def body(c, _):
            base = pl.multiple_of(c * k_walk, k_walk)
            for t in range(tiles):
                col = pl.ds(base + t * LANES, LANES)
                sc_ref[:, col] = jnp.where(sc_ref[:, col] >= lob, on, off)
            return 0
        lax.fori_loop(0, n_walk, body, 0)

    @pl.when(has_ties)
    def _():
        thr = jnp.min(lane_fold(lambda acc, x, rs: jnp.minimum(acc, jnp.where(x >= lob[rs], x, pos_inf)), pos_inf),
                      axis=1, keepdims=True)
        thrb = jnp.broadcast_to(thr, (rows, LANES))
        need = n_sel - count(lambda x, rs: x > thrb[rs])
        upper = (lax.broadcasted_iota(jnp.int32, (kc, kc), 0) <= lax.broadcasted_iota(jnp.int32, (kc, kc), 1))
        upper = jnp.where(upper, 1.0, 0.0).astype(BF16)

        def body(c, seen):
            col = pl.ds(pl.multiple_of(c * kc, kc), kc)
            x = sc_ref[:, col]
            eq = jnp.where(x == thr, 1.0, 0.0)
            rank = _dot(eq.astype(BF16), upper) + seen
            keep_eq = jnp.where(rank <= need, eq, 0.0)
            sc_ref[:, col] = jnp.where(x > thr, on, jnp.where(keep_eq > 0.5, on, off))
            return seen + jnp.sum(eq, axis=1, keepdims=True)
        lax.fori_loop(0, nkc, body, jnp.zeros((rows, 1), F32))


def _dsa_body(qi_ref, kw_ref, ki2_ref, q_ref, k_ref, v_ref, out_ref, sc_ref, wb_ref, qis_ref,
              qs_ref, s_buf, stat_ref, lsum_ref, acc_ref, *, tq, kc, ks, kb, n_sel):
    i = pl.program_id(1)
    nkc = ((i + 1) * tq + kc - 1) // kc

    kw = kw_ref[...]
    for h in range(H_IDX):
        wb_ref[h] = jnp.broadcast_to(kw[:, D_IDX + h:D_IDX + h + 1] * (H_IDX ** -0.5), (tq, LANES))
        j, parity = divmod(h, 2)
        qi_pair = qi_ref[:, j * LANES:(j + 1) * LANES]
        qis_ref[h * tq:(h + 1) * tq, :] = jnp.where(_pair_mask(qi_pair.shape, parity), qi_pair, jnp.zeros_like(qi_pair))
    qpos = i * tq + lax.broadcasted_iota(jnp.int32, (tq, LANES), 0)
    kiota = lax.broadcasted_iota(jnp.int32, (tq, LANES), 1)

    def score_chunk(c, _):
        base = pl.multiple_of(c * ks, ks)
        d = _dot_nt(qis_ref[...], ki2_ref[pl.ds(base, ks), :])
        for t in range(ks // LANES):
            acc = None
            for h in range(H_IDX):
                term = jnp.maximum(d[h * tq:(h + 1) * tq, t * LANES:(t + 1) * LANES], 0.0) * wb_ref[h]
                acc = term if acc is None else acc + term
            kpos = base + t * LANES + kiota
            sc_ref[:, pl.ds(base + t * LANES, LANES)] = jnp.where(kpos <= qpos, acc, NEG_INF)
        return 0

    n_big = ((i + 1) * tq + kb - 1) // kb
    n_score = ((i + 1) * tq + ks - 1) // ks
    lax.fori_loop(0, n_score, score_chunk, 0)

    def fill_hidden(c, _):
        sc_ref[:, pl.ds(pl.multiple_of(c * ks, ks), ks)] = jnp.full((tq, ks), NEG_INF, F32)
        return 0

    lax.fori_loop(n_score, n_big * (kb // ks), fill_hidden, 0)
    _topk_to_bias(sc_ref, nkc, kc, n_sel)

    _stack_pair_queries(q_ref, qs_ref, tq)

    def logits(j, c, raw, masked):
        bias = sc_ref[:, pl.ds(pl.multiple_of(c * kb, kb), kb)]
        return raw + jnp.concatenate([bias, bias], axis=0)

    _attend_pairs(qs_ref, k_ref, v_ref, out_ref, s_buf, stat_ref, lsum_ref, acc_ref, tq=tq, kb=kb,
                  n_plain=n_big, n_chunks=n_big, logits_fn=logits)


def _dsa_prompt(qi, kw, ki2, q, kb, vb, n, s, tq, kc, kbig):
    hb = q.shape[1]
    nq = s // tq
    n_sel = min(TOPK_MAX, s // 4)
    ks = min(kc, 4 * LANES)
    body = functools.partial(_dsa_body, tq=tq, kc=kc, ks=ks, kb=kbig, n_sel=n_sel)
    scratch = [pltpu.VMEM((tq, s), F32), pltpu.VMEM((H_IDX, tq, LANES), F32), pltpu.VMEM((H_IDX * tq, LANES), BF16)]
    scratch += _pair_scratch(hb // LANES, tq, s)
    one = pl.Buffered(1)
    qrow = lambda b, i: (b * nq + i, 0)
    return pl.pallas_call(
        body, grid=(n, nq),
        in_specs=[pl.BlockSpec((tq, hb), qrow), pl.BlockSpec((tq, LANES), qrow),
                  pl.BlockSpec((s, LANES), lambda b, i: (b, 0), pipeline_mode=one),
                  pl.BlockSpec((tq, hb), qrow),
                  pl.BlockSpec((s, hb), lambda b, i: (b, 0), pipeline_mode=one),
                  pl.BlockSpec((s, hb), lambda b, i: (b, 0), pipeline_mode=one)],
        out_specs=pl.BlockSpec((tq, hb), qrow),
        out_shape=jax.ShapeDtypeStruct((n * s, hb), BF16),
        scratch_shapes=scratch, compiler_params=_params(), name="dsa_prompt")(qi, kw, ki2, q, kb, vb)


def _page_specs(pool, li, pages_per_step, n_pages):
    def spec(r):
        def index(n, p, pt):
            return (li, pt[n, jnp.minimum(p * pages_per_step + r, n_pages - 1)]) + (0,) * (pool.ndim - 2)
        return pl.BlockSpec((1, 1) + pool.shape[2:], index)
    return [spec(r) for r in range(pages_per_step)]


def _idx_scores_body(pt_ref, qi_ref, w_ref, *rest, pages_per_step, n_main, page, t_new):
    page_refs = rest[:pages_per_step]
    new_ref, out_ref = rest[pages_per_step], rest[pages_per_step + 1]
    p = pl.program_id(1)
    qi = qi_ref[0]
    w = w_ref[0] * (H_IDX ** -0.5)

    def scores(ki_t):
        d = _dot(qi, ki_t.astype(BF16))
        contrib = jnp.maximum(d, 0.0) * w
        return jnp.sum(contrib.reshape(t_new, H_IDX, page), axis=1)

    @pl.when(p < n_main)
    def _():
        for r in range(pages_per_step):
            off = pl.multiple_of((p * pages_per_step + r) * page, page)
            out_ref[0, :, pl.ds(off, page)] = scores(page_refs[r][0, 0])

    @pl.when(p == n_main)
    def _():
        sc = scores(new_ref[0])
        t = lax.broadcasted_iota(jnp.int32, sc.shape, 0)
        j = lax.broadcasted_iota(jnp.int32, sc.shape, 1)
        out_ref[0, :, n_main * pages_per_step * page:] = jnp.where(j <= t, sc, NEG_INF)


def _idx_scores_sample(page_table, qi_rows, w_rows, ki_pool, li, ki_new_pages, pages_per_step):
    nseq, n_pages = page_table.shape
    page = ki_pool.shape[3]
    t_new = qi_rows.shape[1] // H_IDX
    n_main = n_pages // pages_per_step
    length = (n_pages + 1) * page
    body = functools.partial(_idx_scores_body, pages_per_step=pages_per_step, n_main=n_main, page=page, t_new=t_new)
    per_seq = lambda n, p, pt: (n, 0, 0)
    grid_spec = pltpu.PrefetchScalarGridSpec(
        num_scalar_prefetch=1, grid=(nseq, n_main + 1),
        in_specs=[pl.BlockSpec((1,) + qi_rows.shape[1:], per_seq), pl.BlockSpec((1,) + w_rows.shape[1:], per_seq)]
        + _page_specs(ki_pool, li, pages_per_step, n_pages)
        + [pl.BlockSpec((1,) + ki_new_pages.shape[1:], per_seq)],
        out_specs=pl.BlockSpec((1, t_new, length), per_seq))
    return pl.pallas_call(
        body, grid_spec=grid_spec, out_shape=jax.ShapeDtypeStruct((nseq, t_new, length), F32),
        compiler_params=_params(), name="idx_scores_sample")(
            page_table, qi_rows, w_rows, *([ki_pool] * pages_per_step), ki_new_pages)


def _topk_rows_body(sc_ref, out_ref, *, kc, n_sel):
    out_ref[...] = sc_ref[...]
    _topk_to_bias(out_ref, out_ref.shape[1] // kc, kc, n_sel, on=1.0, off=0.0)


def _topk_rows(scores, n_sel, rows, kc):
    r, length = scores.shape
    body = functools.partial(_topk_rows_body, kc=kc, n_sel=n_sel)
    return pl.pallas_call(
        body, grid=(r // rows,), in_specs=[pl.BlockSpec((rows, length), lambda i: (i, 0))],
        out_specs=pl.BlockSpec((rows, length), lambda i: (i, 0)),
        out_shape=jax.ShapeDtypeStruct((r, length), F32), compiler_params=_params(), name="topk_rows")(scores)


def _split3(x):
    hi = x.astype(BF16)
    r1 = x - hi.astype(F32)
    mid = r1.astype(BF16)
    lo = (r1 - mid.astype(F32)).astype(BF16)
    return hi, mid, lo


def _decay_bias_body(pt_ref, *rest, pages_per_step, n_pages, n_heads):
    page_refs = rest[:pages_per_step]
    new_ref, upto_ref, earlier_ref, out_ref, lf_ref = rest[pages_per_step:]
    p = pl.program_id(1)
    n_rows, page = lf_ref.shape

    @pl.when(p == 0)
    def _():
        lf_ref[n_pages * n_heads:, :] = new_ref[0]

    for r in range(pages_per_step):
        row0 = pl.multiple_of((p * pages_per_step + r) * n_heads, n_heads)
        lf_ref[pl.ds(row0, n_heads), :] = page_refs[r][0, 0]

    @pl.when(p == pl.num_programs(1) - 1)
    def _():
        upto = upto_ref[...]
        earlier_pages = earlier_ref[...]
        cum = sum(_dot(x, upto) for x in _split3(lf_ref[...]))
        totals = jnp.broadcast_to(cum[:, page - 1:], cum.shape)
        cum = cum + sum(_dot(earlier_pages, x) for x in _split3(totals))
        out_ref[0] = -cum


def _decay_bias_sample(page_table, f_pool_t, li, f_new_t):
    nseq, n_pages = page_table.shape
    n_heads, page = f_pool_t.shape[2:]
    pages_per_step = _pages_per_step(n_pages, 32)
    n_rows = (n_pages + 1) * n_heads
    per_seq = lambda n, p, pt: (n, 0, 0)
    body = functools.partial(_decay_bias_body, pages_per_step=pages_per_step, n_pages=n_pages, n_heads=n_heads)
    key = jnp.arange(page)
    upto = (key[:, None] <= key[None, :]).astype(BF16)
    row = jnp.arange(n_rows)
    earlier = ((row[None, :] % n_heads == row[:, None] % n_heads)
               & (row[None, :] // n_heads < row[:, None] // n_heads)).astype(BF16)
    grid_spec = pltpu.PrefetchScalarGridSpec(
        num_scalar_prefetch=1, grid=(nseq, n_pages // pages_per_step),
        in_specs=_page_specs(f_pool_t, li, pages_per_step, n_pages)
        + [pl.BlockSpec((1, n_heads, page), per_seq), _const_spec(upto.shape), _const_spec(earlier.shape)],
        out_specs=pl.BlockSpec((1, n_rows, page), per_seq), scratch_shapes=[pltpu.VMEM((n_rows, page), F32)])
    return pl.pallas_call(
        body, grid_spec=grid_spec, out_shape=jax.ShapeDtypeStruct((nseq, n_rows, page), F32),
        compiler_params=_params(), name="decay_bias_sample")(
            page_table, *([f_pool_t] * pages_per_step), f_new_t, upto, earlier)


def _paged_attn_body(pt_ref, q_ref, *rest, pages_per_step, n_main, page, t_new, n_heads, decay):
    it = iter(rest)
    k_refs = [next(it) for _ in range(pages_per_step)]
    v_refs = [next(it) for _ in range(pages_per_step)]
    knew_ref, vnew_ref = next(it), next(it)
    side_ref = next(it)
    out_ref = next(it)
    m_ref, l_ref, acc_ref = next(it), next(it), next(it)
    p = pl.program_id(1)
    rows = t_new * n_heads
    q = q_ref[0]

    @pl.when(p == 0)
    def _():
        m_ref[...] = jnp.full(m_ref.shape, M_INIT, F32)
        l_ref[...] = jnp.zeros(l_ref.shape, F32)
        acc_ref[...] = jnp.zeros(acc_ref.shape, F32)

    def page_logits(k_t, page_idx, new_page):
        s = _dot(q, k_t.astype(BF16))
        if decay:
            neg_cum = side_ref[0, pl.ds(pl.multiple_of(page_idx * n_heads, n_heads), n_heads), :]
            s = s + jnp.concatenate([neg_cum] * t_new, axis=0)
            if new_page:
                t = lax.broadcasted_iota(jnp.int32, s.shape, 0) // n_heads
                s = jnp.where(lax.broadcasted_iota(jnp.int32, s.shape, 1) <= t, s, NEG_INF)
            return s
        sel = side_ref[0, :, pl.ds(pl.multiple_of(page_idx * page, page), page)]
        sel_rows = jnp.broadcast_to(sel[:, None, :], (t_new, n_heads, page)).reshape(rows, page)
        return jnp.where(sel_rows > 0.5, s, NEG_INF)

    def update(logits, values_t):
        m = m_ref[...]
        m_new = m
        for s in logits:
            m_new = jnp.maximum(m_new, jnp.max(s, axis=1, keepdims=True))
        alpha = jnp.exp(m - m_new)
        l_new = alpha * l_ref[...]
        acc = alpha * acc_ref[...]
        for s, v_t in zip(logits, values_t):
            pr = jnp.exp(s - m_new)
            l_new = l_new + jnp.sum(pr, axis=1, keepdims=True)
            acc = acc + _dot_nt(pr.astype(BF16), v_t.astype(BF16))
        l_ref[...] = l_new
        acc_ref[...] = acc
        m_ref[...] = m_new

    @pl.when(p < n_main)
    def _():
        update([page_logits(k_refs[r][0, 0], p * pages_per_step + r, False) for r in range(pages_per_step)],
               [v_refs[r][0, 0] for r in range(pages_per_step)])

    @pl.when(p == n_main)
    def _():
        update([page_logits(knew_ref[0], n_main * pages_per_step, True)], [vnew_ref[0]])
        o = acc_ref[...] / l_ref[...]
        head_of_row = lax.broadcasted_iota(jnp.int32, o.shape, 0) % n_heads
        head_of_lane = lax.broadcasted_iota(jnp.int32, o.shape, 1) // HEAD_DIM
        o = jnp.where(head_of_row == head_of_lane, o, 0.0)
        out_ref[0] = jnp.sum(o.reshape(t_new, n_heads, o.shape[1]), axis=1).astype(BF16)


def _paged_attn(page_table, q_rows, k_pool_t, v_pool_t, li, k_new_t, v_new_t, pages_per_step,
                neg_cum=None, selected=None):
    nseq, n_pages = page_table.shape
    width, page = k_pool_t.shape[2:]
    n_heads = width // HEAD_DIM
    rows = q_rows.shape[1]
    t_new = rows // n_heads
    n_main = n_pages // pages_per_step
    decay = neg_cum is not None
    side = neg_cum if decay else selected
    per_seq = lambda n, p, pt: (n, 0, 0)
    body = functools.partial(_paged_attn_body, pages_per_step=pages_per_step, n_main=n_main, page=page,
                             t_new=t_new, n_heads=n_heads, decay=decay)
    args = [page_table, q_rows] + [k_pool_t] * pages_per_step + [v_pool_t] * pages_per_step + [k_new_t, v_new_t, side]
    in_specs = [pl.BlockSpec((1,) + q_rows.shape[1:], per_seq)]
    in_specs += _page_specs(k_pool_t, li, pages_per_step, n_pages) + _page_specs(v_pool_t, li, pages_per_step, n_pages)
    in_specs += [pl.BlockSpec((1, width, page), per_seq), pl.BlockSpec((1, width, page), per_seq),
                 pl.BlockSpec((1,) + side.shape[1:], per_seq)]
    scratch = [pltpu.VMEM((rows, 1), F32), pltpu.VMEM((rows, 1), F32), pltpu.VMEM((rows, width), F32)]
    grid_spec = pltpu.PrefetchScalarGridSpec(
        num_scalar_prefetch=1, grid=(nseq, n_main + 1), in_specs=in_specs,
        out_specs=pl.BlockSpec((1, t_new, width), per_seq), scratch_shapes=scratch)
    return pl.pallas_call(
        body, grid_spec=grid_spec, out_shape=jax.ShapeDtypeStruct((nseq, t_new, width), BF16),
        compiler_params=_params(), name="paged_attn_decay" if decay else "paged_attn_select")(*args)


def _rope_tables(pos):
    half = HEAD_DIM // 2
    inv = ROPE_THETA ** (-jnp.arange(half, dtype=F32) / half)
    ang = pos.astype(F32)[:, None] * inv[None, :]
    cos, sin = jnp.cos(ang), jnp.sin(ang)
    reps = LANES // HEAD_DIM
    return jnp.tile(jnp.concatenate([cos, cos], axis=1), (1, reps)), jnp.tile(jnp.concatenate([-sin, sin], axis=1), (1, reps))


def _block_diag(w):
    g, bw, _ = w.shape
    eye = jnp.eye(g, dtype=w.dtype)
    return (w[:, :, None, :] * eye[:, None, :, None]).reshape(g * bw, g * bw)


def _pad_cols(w, width):
    return jnp.pad(w, ((0, 0), (0, width - w.shape[1])))


def _to_time_major(a):
    return jnp.swapaxes(a, 0, 1).reshape((a.shape[0] * a.shape[1],) + a.shape[2:])


def _from_time_major(a, n):
    return jnp.swapaxes(a.reshape((a.shape[0] // n, n) + a.shape[1:]), 0, 1)


def _expand_heads(a, n, n_heads):
    width = a.shape[1]
    per_seq = _from_time_major(a, n)
    mask = (jnp.arange(width)[None, :] // HEAD_DIM) == jnp.arange(n_heads)[:, None]
    out = jnp.where(mask[None, None], per_seq[:, :, None, :], jnp.zeros((), a.dtype))
    return out.reshape(n, -1, width)


def _new_pages_t(a, n, page):
    return jnp.swapaxes(_new_pages(a, n, page), 1, 2)


def _key_minor(pool):
    moved = jnp.moveaxis(pool, 2, -1)
    return moved.reshape(moved.shape[:2] + (-1, moved.shape[-1]))


def _new_pages(a, n, page):
    per_seq = _from_time_major(a, n)
    return jnp.pad(per_seq, ((0, 0), (0, page - per_seq.shape[1]), (0, 0)))


def _heads_from_key_minor(a):
    n, width, s = a.shape
    return jnp.transpose(a.reshape(n, width // HEAD_DIM, HEAD_DIM, s), (0, 3, 1, 2))


def _pages_per_step(n_pages, pref):
    while n_pages % pref:
        pref //= 2
    return pref


def _pick_tile(m, pref):
    t = min(m, pref)
    while m % t:
        t //= 2
    return t


def kernel(x_prompt, x_sample, cache_k_b, cache_v_b, cache_kidx_b, state_conv_a, cache_k_c, cache_v_c, cache_logf_c, state_conv_d, state_lru_h, page_table, g_ff1, w_ff1_gate, w_ff1_up, w_ff1_down, g_mix, g_ff2, w_ff2_gate, w_ff2_up, w_ff2_down, w_in_e, conv_a_w, w_out_e, w_in_o, b_f, conv_d_w, conv_d_b, w_rg_a, b_rg_a, w_rg_x, b_rg_x, lam, w_out_o, g_final):
    batch, seq, d_model = x_prompt.shape
    nseq, t_new, _ = x_sample.shape
    depth = g_ff1.shape[0]
    n_pool, page = cache_k_b.shape[1], cache_k_b.shape[2]
    n_pages = page_table.shape[1]
    past = n_pages * page
    wa = conv_a_w.shape[2]
    hb = w_out_e.shape[1] - wa
    wd = conv_d_w.shape[2]
    hc = w_out_o.shape[1] - wd
    n_heads_c = hc // HEAD_DIM
    pages_per_step = _pages_per_step(n_pages, 16)
    idx_pages_per_step = _pages_per_step(n_pages, 32)

    mp, ms = batch * seq, nseq * t_new
    tm_p = _pick_tile(seq, 512)
    xp = x_prompt.reshape(mp, d_model)
    xs = _to_time_major(x_sample)
    cos_p, sin_p = _rope_tables(jnp.tile(jnp.arange(seq), batch))
    cos_s, sin_s = _rope_tables(past + jnp.repeat(jnp.arange(t_new), nseq))
    row2 = lambda v: v.reshape(1, -1)
    bf = lambda w: w.astype(BF16)

    outs_p, outs_s = {}, {}
    for l in range(depth):
        ffn1 = (row2(g_ff1[l]), bf(w_ff1_gate[l]), bf(w_ff1_up[l]), bf(w_ff1_down[l]))
        ffn2 = (row2(g_ff2[l]), bf(w_ff2_gate[l]), bf(w_ff2_up[l]), bf(w_ff2_down[l]))
        xp = _ffn(xp, *ffn1, tm=tm_p)
        xs = _ffn(xs, *ffn1, tm=ms)
        g = row2(g_mix[l])
        if l % 2 == 0:
            e = l // 2
            w = w_in_e[e]
            o_q, o_k, o_v, o_qi, o_kw = 3 * wa, 3 * wa + hb, 3 * wa + 2 * hb, 3 * wa + 3 * hb, 3 * wa + 3 * hb + H_IDX * D_IDX
            weights = (bf(w[:, :o_q]), bf(w[:, o_q:o_k]), bf(w[:, o_k:o_v]), bf(w[:, o_v:o_qi]), bf(w[:, o_qi:o_kw]),
                       bf(_pad_cols(w[:, o_kw:], LANES)))
            w_out = (bf(w_out_e[e][:wa]), bf(w_out_e[e][wa:]))

            hist_p = jnp.zeros((batch, conv_a_w.shape[1] - 1, wa), F32)
            ya, q, k32, kb, v32, vb, qi, kw, ki2, cstate = _proj_even(
                xp, g, weights, cos_p, sin_p, conv_a_w[e], hist_p, tm_p, seq // tm_p, 1, True)
            yb = _dsa_prompt(qi, kw, ki2, q, kb, vb, batch, seq, _pick_tile(seq, 256), _pick_tile(seq, 512),
                             _pick_tile(seq, 1024))
            outs_p.setdefault("k_b", []).append(_heads_from_key_minor(k32))
            outs_p.setdefault("v_b", []).append(_heads_from_key_minor(v32))
            outs_p.setdefault("kidx_b", []).append(kw[:, :D_IDX].reshape(batch, seq, D_IDX))
            outs_p.setdefault("conv_a", []).append(cstate)
            mix_p = (ya, yb, *w_out)

            hist_s = _to_time_major(state_conv_a[e])[None]
            ya, q, k32, kb, v32, vb, qi, kw, ki2, cstate = _proj_even(
                xs, g, weights, cos_s, sin_s, conv_a_w[e], hist_s, ms, 1, nseq, False)
            qi_rows = _from_time_major(qi, nseq).reshape(nseq, t_new * H_IDX, D_IDX)
            w_rows = _from_time_major(kw[:, D_IDX:D_IDX + H_IDX], nseq).reshape(nseq, t_new * H_IDX, 1)
            scores = _idx_scores_sample(page_table, qi_rows, w_rows, _key_minor(cache_kidx_b), e,
                                        _new_pages_t(kw[:, :D_IDX], nseq, page), idx_pages_per_step)
            length = scores.shape[2]
            n_sel = min(TOPK_MAX, (past + t_new) // 4)
            kc = page * 3 if (length // page) % 3 == 0 else page
            picked = _topk_rows(scores.reshape(ms, length), n_sel, _pick_tile(ms, 32), kc).reshape(nseq, t_new, length)
            yb = _paged_attn(page_table, _expand_heads(q, nseq, hb // HEAD_DIM), _key_minor(cache_k_b),
                             _key_minor(cache_v_b), e, _new_pages_t(k32, nseq, page), _new_pages_t(v32, nseq, page),
                             pages_per_step, selected=picked)
            yb = _to_time_major(yb)
            outs_s.setdefault("k_b", []).append(_from_time_major(k32, nseq).reshape(nseq, t_new, hb // HEAD_DIM, HEAD_DIM))
            outs_s.setdefault("v_b", []).append(_from_time_major(v32, nseq).reshape(nseq, t_new, hb // HEAD_DIM, HEAD_DIM))
            outs_s.setdefault("kidx_b", []).append(_from_time_major(kw[:, :D_IDX], nseq))
            outs_s.setdefault("conv_a", []).append(_from_time_major(cstate[0], nseq))
            mix_s = (ya, yb, *w_out)
        else:
            o = l // 2
            w = w_in_o[o]
            o_f, o_xd, o_gd = 3 * hc, 3 * hc + n_heads_c, 3 * hc + n_heads_c + wd
            weights = (bf(w[:, :o_f]), bf(_pad_cols(w[:, o_f:o_xd], LANES)), bf(w[:, o_xd:o_gd]), bf(w[:, o_gd:]))
            small = (_pad_cols(row2(b_f[o]), LANES), conv_d_w[o], row2(conv_d_b[o]), bf(_block_diag(w_rg_a[o])),
                     row2(b_rg_a[o]), bf(_block_diag(w_rg_x[o])), row2(b_rg_x[o]), row2(lam[o]))
            w_out = (bf(w_out_o[o][:hc]), bf(w_out_o[o][hc:]))

            hist_p = jnp.zeros((batch, conv_d_w.shape[1] - 1, wd), F32)
            h0_p = jnp.zeros((batch, 1, wd), F32)
            q, k32, kb, v32, vb, logf, yd, cstate, hlast = _proj_odd(
                xp, g, weights, small, hist_p, h0_p, tm_p, seq // tm_p, 1, 0, True)
            cum_t = _cumsum_heads(logf, batch, seq, _pick_tile(seq, 512))
            yc = _fox_prompt(q, kb, vb, cum_t, batch, seq, _pick_tile(seq, 512), _pick_tile(seq, 1024))
            outs_p.setdefault("k_c", []).append(_heads_from_key_minor(k32))
            outs_p.setdefault("v_c", []).append(_heads_from_key_minor(v32))
            outs_p.setdefault("logf_c", []).append(logf.reshape(batch, seq, n_heads_c))
            outs_p.setdefault("conv_d", []).append(cstate)
            outs_p.setdefault("lru_h", []).append(hlast[:, 0])
            mix_p = (yc, yd, *w_out)

            hist_s = _to_time_major(state_conv_d[o])[None]
            h0_s = state_lru_h[o][None]
            q, k32, kb, v32, vb, logf, yd, cstate, hlast = _proj_odd(
                xs, g, weights, small, hist_s, h0_s, ms, 1, nseq, past, False)
            neg_cum = _decay_bias_sample(page_table, _key_minor(cache_logf_c), o, _new_pages_t(logf, nseq, page))
            yc = _paged_attn(page_table, _expand_heads(q, nseq, n_heads_c), _key_minor(cache_k_c),
                             _key_minor(cache_v_c), o, _new_pages_t(k32, nseq, page), _new_pages_t(v32, nseq, page),
                             pages_per_step, neg_cum=neg_cum)
            yc = _to_time_major(yc)
            outs_s.setdefault("k_c", []).append(_from_time_major(k32, nseq).reshape(nseq, t_new, n_heads_c, HEAD_DIM))
            outs_s.setdefault("v_c", []).append(_from_time_major(v32, nseq).reshape(nseq, t_new, n_heads_c, HEAD_DIM))
            outs_s.setdefault("logf_c", []).append(_from_time_major(logf, nseq))
            outs_s.setdefault("conv_d", []).append(_from_time_major(cstate[0], nseq))
            outs_s.setdefault("lru_h", []).append(hlast[0])
            mix_s = (yc, yd, *w_out)
        gfin = row2(g_final) if l == depth - 1 else None
        xp = _ffn(xp, *ffn2, tm=tm_p, mix=mix_p, g_final=gfin)
        xs = _ffn(xs, *ffn2, tm=ms, mix=mix_s, g_final=gfin)

    names = ("k_b", "v_b", "kidx_b", "conv_a", "k_c", "v_c", "logf_c", "conv_d", "lru_h")
    st_p = tuple(jnp.stack(outs_p[nm]) for nm in names)
    st_s = tuple(jnp.stack(outs_s[nm]) for nm in names)
    return (xp.reshape(batch, seq, d_model), _from_time_major(xs, nseq)) + st_p + st_s
```
